```python
import jax, jax.numpy as jnp
from jax import lax
import numpy as np

D_MODEL = 2048
BATCH = 8
SEQ = 4096
DEPTH = 2

HA_Q = 8
HA_KV = 2
HEAD_DIM = 128
WINDOW = 128
BLOCK = 128
HB = 8
QK_NOPE = 128
QK_ROPE = 64
V_DIM = 128
Q_RANK = 512
KV_RANK = 512
D_FF = 5632
ROPE_THETA = 10000.0
EPS = 1e-6
N_MOD = 9

COLS_QA = HA_Q * HEAD_DIM
COLS_KA = HA_KV * HEAD_DIM
COLS_VA = HA_KV * HEAD_DIM
COLS_CQ = Q_RANK
COLS_CKV = KV_RANK
COLS_KR = QK_ROPE
COLS_GATE = 2 * D_MODEL
IN_SIZES = (COLS_QA, COLS_KA, COLS_VA, COLS_CQ, COLS_CKV, COLS_KR, COLS_GATE)
IN_COLS = sum(IN_SIZES)
IN_SPLITS = tuple(int(s) for s in np.cumsum(IN_SIZES)[:-1])

kernel_name = "hybrid_gated_swa_mla_macaron_adaln"


def rmsnorm(x, g):
    xf = x.astype(jnp.float32)
    y = xf * lax.rsqrt(jnp.mean(xf * xf, axis=-1, keepdims=True) + EPS)
    return (y * g.astype(jnp.float32)).astype(x.dtype)


def modulate(h, shift, scale):
    return h * (1 + scale[:, None, :]) + shift[:, None, :]


def rope_tables(positions, dim):
    freqs = ROPE_THETA ** (-jnp.arange(0, dim, 2, dtype=jnp.float32) / dim)
    ang = positions.astype(jnp.float32)[..., None] * freqs
    return jnp.cos(ang), jnp.sin(ang)


def apply_rope(x, cos, sin):
    xf = x.astype(jnp.float32)
    x1, x2 = jnp.split(xf, 2, axis=-1)
    return jnp.concatenate([x1 * cos - x2 * sin, x2 * cos + x1 * sin], axis=-1).astype(x.dtype)


def swiglu(h, w_gu, w_d):
    gu = h @ w_gu
    g, u = jnp.split(gu, 2, axis=-1)
    return (jax.nn.silu(g) * u) @ w_d


def window_gqa_sink(q, k, v, sink):
    B, S = q.shape[0], q.shape[1]
    nb = S // BLOCK
    G = HA_Q // HA_KV
    qb = q.reshape(B, nb, BLOCK, HA_KV, G, HEAD_DIM)
    pad = ((0, 0), (BLOCK, BLOCK), (0, 0), (0, 0))
    kp = jnp.pad(k, pad).reshape(B, nb + 2, BLOCK, HA_KV, HEAD_DIM)
    vp = jnp.pad(v, pad).reshape(B, nb + 2, BLOCK, HA_KV, HEAD_DIM)
    kb = jnp.concatenate([kp[:, :-2], kp[:, 1:-1], kp[:, 2:]], axis=2)
    vb = jnp.concatenate([vp[:, :-2], vp[:, 1:-1], vp[:, 2:]], axis=2)
    s = jnp.einsum('bnqhgd,bnkhd->bnhgqk', qb, kb).astype(jnp.float32) * (HEAD_DIM ** -0.5)
    r = jnp.arange(BLOCK)[:, None]
    j = jnp.arange(3 * BLOCK)[None, :]
    rel_ok = jnp.abs(j - BLOCK - r) <= WINDOW
    kpos = jnp.arange(nb)[:, None] * BLOCK - BLOCK + jnp.arange(3 * BLOCK)[None, :]
    pos_ok = (kpos >= 0) & (kpos < S)
    mask = rel_ok[None, :, :] & pos_ok[:, None, :]
    s = jnp.where(mask[None, :, None, None, :, :], s, -1e30)
    sk = sink.astype(jnp.float32).reshape(HA_KV, G)[None, None, :, :, None, None]
    m = jnp.maximum(jnp.max(s, axis=-1, keepdims=True), sk)
    p = jnp.exp(s - m)
    denom = jnp.sum(p, axis=-1, keepdims=True) + jnp.exp(sk - m)
    p = (p / denom).astype(v.dtype)
    o = jnp.einsum('bnhgqk,bnkhd->bnqhgd', p, vb)
    return o.reshape(B, S, HA_Q * HEAD_DIM)


def mla_attention(q_nope, q_rope, k_nope, k_rope, v):
    B, S = q_nope.shape[0], q_nope.shape[1]
    nb = S // BLOCK
    scale = (QK_NOPE + QK_ROPE) ** -0.5
    qn = q_nope.reshape(B, nb, BLOCK, HB, QK_NOPE).transpose(1, 0, 2, 3, 4)
    qr = q_rope.reshape(B, nb, BLOCK, HB, QK_ROPE).transpose(1, 0, 2, 3, 4)

    def one_block(args):
        qn_b, qr_b = args
        s = (jnp.einsum('bqhd,bkhd->bhqk', qn_b, k_nope)
             + jnp.einsum('bqhr,bkr->bhqk', qr_b, k_rope)).astype(jnp.float32) * scale
        p = jax.nn.softmax(s, axis=-1).astype(v.dtype)
        return jnp.einsum('bhqk,bkhd->bqhd', p, v)

    o = lax.map(one_block, (qn, qr))
    return o.transpose(1, 0, 2, 3, 4).reshape(B, S, HB * V_DIM)


def token_mixer(h, cos_a, sin_a, cos_r, sin_r, w_in, b_gate, sink, g_cq, g_ckv,
                w_uq, w_ukv, w_oa, w_ob, w_out):
    B, S, _ = h.shape
    qa, ka, va, cq, ckv, kr, gates = jnp.split(h @ w_in, IN_SPLITS, axis=-1)
    qa = apply_rope(qa.reshape(B, S, HA_Q, HEAD_DIM), cos_a, sin_a)
    ka = apply_rope(ka.reshape(B, S, HA_KV, HEAD_DIM), cos_a, sin_a)
    va = va.reshape(B, S, HA_KV, HEAD_DIM)
    y_a = window_gqa_sink(qa, ka, va, sink) @ w_oa
    q = (rmsnorm(cq, g_cq) @ w_uq).reshape(B, S, HB, QK_NOPE + QK_ROPE)
    q_nope, q_rope = q[..., :QK_NOPE], q[..., QK_NOPE:]
    q_rope = apply_rope(q_rope, cos_r[:, :, None, :], sin_r[:, :, None, :])
    kv = (rmsnorm(ckv, g_ckv) @ w_ukv).reshape(B, S, HB, QK_NOPE + V_DIM)
    k_nope, v_b = kv[..., :QK_NOPE], kv[..., QK_NOPE:]
    k_rope = apply_rope(kr, cos_r, sin_r)
    y_b = mla_attention(q_nope, q_rope, k_nope, k_rope, v_b) @ w_ob
    g_a, g_b = jnp.split(jax.nn.sigmoid(gates + b_gate), 2, axis=-1)
    return (g_a * y_a + g_b * y_b) @ w_out


def setup_inputs(seed: int = 0) -> dict:
    key = jax.random.key(seed)
    ks = jax.random.split(key, 24)
    f32 = jnp.float32
    D = D_MODEL

    def nrm(k, shape, s):
        return jax.random.normal(k, shape, f32) * s

    x = nrm(ks[0], (BATCH, SEQ, D), 1.0)
    c = nrm(ks[1], (BATCH, D), 1.0)
    offsets = jax.random.randint(ks[2], (BATCH, 1), 0, 1024, dtype=jnp.int32)
    positions = offsets + jnp.arange(SEQ, dtype=jnp.int32)[None, :]
    return {
        "x": x,
        "c": c,
        "positions": positions,
        "norm_g": 1.0 + nrm(ks[3], (DEPTH, 3, D), 0.1),
        "w_ada": nrm(ks[4], (DEPTH, D, N_MOD * D), 0.5 * D ** -0.5),
        "b_ada": nrm(ks[5], (DEPTH, N_MOD * D), 0.1),
        "w_ffn1_gu": nrm(ks[6], (DEPTH, D, 2 * D_FF), D ** -0.5),
        "w_ffn1_d": nrm(ks[7], (DEPTH, D_FF, D), D_FF ** -0.5),
        "w_ffn2_gu": nrm(ks[8], (DEPTH, D, 2 * D_FF), D ** -0.5),
        "w_ffn2_d": nrm(ks[9], (DEPTH, D_FF, D), D_FF ** -0.5),
        "w_in": nrm(ks[10], (DEPTH, D, IN_COLS), D ** -0.5),
        "b_gate": nrm(ks[11], (DEPTH, 2 * D), 0.1),
        "sink": nrm(ks[12], (DEPTH, HA_Q), 1.0),
        "g_cq": 1.0 + nrm(ks[13], (DEPTH, Q_RANK), 0.1),
        "g_ckv": 1.0 + nrm(ks[14], (DEPTH, KV_RANK), 0.1),
        "w_uq": nrm(ks[15], (DEPTH, Q_RANK, HB * (QK_NOPE + QK_ROPE)), Q_RANK ** -0.5),
        "w_ukv": nrm(ks[16], (DEPTH, KV_RANK, HB * (QK_NOPE + V_DIM)), KV_RANK ** -0.5),
        "w_oa": nrm(ks[17], (DEPTH, HA_Q * HEAD_DIM, D), (HA_Q * HEAD_DIM) ** -0.5),
        "w_ob": nrm(ks[18], (DEPTH, HB * V_DIM, D), (HB * V_DIM) ** -0.5),
        "w_out": nrm(ks[19], (DEPTH, D, D), D ** -0.5),
        "g_final": 1.0 + nrm(ks[20], (D,), 0.1),
    }


def reference(x, c, positions, norm_g, w_ada, b_ada, w_ffn1_gu, w_ffn1_d, w_ffn2_gu, w_ffn2_d,
              w_in, b_gate, sink, g_cq, g_ckv, w_uq, w_ukv, w_oa, w_ob, w_out, g_final):
    cos_a, sin_a = rope_tables(positions, HEAD_DIM)
    cos_a, sin_a = cos_a[:, :, None, :], sin_a[:, :, None, :]
    cos_r, sin_r = rope_tables(positions, QK_ROPE)
    c_act = jax.nn.silu(c)
    for l in range(DEPTH):
        sh1, sc1, gt1, sh2, sc2, gt2, sh3, sc3, gt3 = jnp.split(c_act @ w_ada[l] + b_ada[l], N_MOD, axis=-1)
        h = modulate(rmsnorm(x, norm_g[l, 0]), sh1, sc1)
        x = x + 0.5 * gt1[:, None, :] * swiglu(h, w_ffn1_gu[l], w_ffn1_d[l])
        h = modulate(rmsnorm(x, norm_g[l, 1]), sh2, sc2)
        x = x + gt2[:, None, :] * token_mixer(h, cos_a, sin_a, cos_r, sin_r, w_in[l], b_gate[l], sink[l],
                                              g_cq[l], g_ckv[l], w_uq[l], w_ukv[l], w_oa[l], w_ob[l], w_out[l])
        h = modulate(rmsnorm(x, norm_g[l, 2]), sh3, sc3)
        x = x + 0.5 * gt3[:, None, :] * swiglu(h, w_ffn2_gu[l], w_ffn2_d[l])
    return rmsnorm(x, g_final)
```

```python
import functools

import jax
import jax.numpy as jnp
import numpy as np
from jax import lax
from jax.experimental import pallas as pl
from jax.experimental.pallas import tpu as pltpu

D_MODEL = 2048
BATCH = 8
SEQ = 4096
DEPTH = 2
HA_Q = 8
HA_KV = 2
GROUP = HA_Q // HA_KV
HEAD_DIM = 128
WINDOW = 128
BLOCK = 128
HB = 8
QK_NOPE = 128
QK_ROPE = 64
V_DIM = 128
Q_RANK = 512
KV_RANK = 512
D_FF = 5632
ROPE_THETA = 10000.0
EPS = 1e-6
N_MOD = 9
TOKENS = BATCH * SEQ

COLS_QA = HA_Q * HEAD_DIM
COLS_KA = HA_KV * HEAD_DIM
COLS_VA = HA_KV * HEAD_DIM
OFF_CQ = COLS_QA + COLS_KA + COLS_VA
OFF_CKV = OFF_CQ + Q_RANK
OFF_KR = OFF_CKV + KV_RANK
OFF_GATE = OFF_KR + QK_ROPE

LANES = 128
MLA_PAD = 256
N_ROPE_TABLES = 5
VMEM_LIMIT = 56 * 1024 * 1024

BF16 = jnp.bfloat16
F32 = jnp.float32


def _params(sem):
    return pltpu.CompilerParams(dimension_semantics=sem, vmem_limit_bytes=VMEM_LIMIT)


def _dot(a, b):
    return jnp.dot(a, b, preferred_element_type=F32)


def _dot_nt(a, b):
    return lax.dot_general(a, b, (((1,), (1,)), ((), ())), preferred_element_type=F32)


def _norm_mod(x, g, shift, scale):
    y = x * lax.rsqrt(jnp.mean(x * x, axis=-1, keepdims=True) + EPS)
    return (y * g) * (1.0 + scale) + shift


def _rope_table_kernel(pos_ref, freq_ref, o_ref):
    ang = pos_ref[...].astype(F32) * freq_ref[...]
    cs = jnp.cos(ang)
    sn = jnp.sin(ang)
    lane = lax.broadcasted_iota(jnp.int32, cs.shape, 1)
    cs64 = pltpu.roll(cs, 64, 1)
    sn64 = pltpu.roll(sn, 64, 1)
    cs96 = pltpu.roll(cs, 96, 1)
    sn96 = pltpu.roll(sn, 96, 1)
    zero = jnp.zeros_like(cs)
    o_ref[:, 0 * LANES:1 * LANES] = jnp.where(lane < 64, cs, cs64)
    o_ref[:, 1 * LANES:2 * LANES] = jnp.where(lane < 64, -sn, sn64)
    o_ref[:, 2 * LANES:3 * LANES] = jnp.where(lane < 32, cs64, jnp.where(lane < 64, cs96, zero))
    o_ref[:, 3 * LANES:4 * LANES] = jnp.where(lane < 32, -sn64, zero)
    o_ref[:, 4 * LANES:5 * LANES] = jnp.where((lane >= 32) & (lane < 64), sn96, zero)


def _rope_tables(positions):
    tm = 1024
    fa = ROPE_THETA ** (-jnp.arange(0, HEAD_DIM, 2, dtype=F32) / HEAD_DIM)
    fb = ROPE_THETA ** (-jnp.arange(0, QK_ROPE, 2, dtype=F32) / QK_ROPE)
    freq = jnp.concatenate([fa, fb, jnp.zeros((32,), F32)])[None, :]
    pos = positions.reshape(TOKENS, 1)
    return pl.pallas_call(
        _rope_table_kernel,
        grid=(TOKENS // tm,),
        in_specs=[pl.BlockSpec((tm, 1), lambda i: (i, 0)),
                  pl.BlockSpec((1, LANES), lambda i: (0, 0))],
        out_specs=pl.BlockSpec((tm, N_ROPE_TABLES * LANES), lambda i: (i, 0)),
        out_shape=jax.ShapeDtypeStruct((TOKENS, N_ROPE_TABLES * LANES), F32),
        compiler_params=_params(("parallel",)),
        name="rope_tables",
    )(pos, freq)


def _ada_kernel(c_ref, w_ref, b_ref, o_ref):
    c = c_ref[...]
    c_act = (c * jax.nn.sigmoid(c)).astype(BF16)
    o_ref[...] = _dot(c_act, w_ref[...].astype(BF16)) + b_ref[...]


def _ada_mod(c, w_ada, b_ada):
    tn = 1024
    n = N_MOD * D_MODEL
    return pl.pallas_call(
        _ada_kernel,
        grid=(DEPTH, n // tn),
        in_specs=[pl.BlockSpec((BATCH, D_MODEL), lambda l, j: (0, 0)),
                  pl.BlockSpec((None, D_MODEL, tn), lambda l, j: (l, 0, j)),
                  pl.BlockSpec((None, 1, tn), lambda l, j: (l, 0, j))],
        out_specs=pl.BlockSpec((None, BATCH, tn), lambda l, j: (l, 0, j)),
        out_shape=jax.ShapeDtypeStruct((DEPTH, BATCH, n), F32),
        compiler_params=_params(("parallel", "parallel")),
        name="ada_mod",
    )(c, w_ada, b_ada.reshape(DEPTH, 1, n))


def _mod_spec(tm, k):
    return pl.BlockSpec((None, 1, D_MODEL), lambda i, *_: (i * tm // SEQ, 0, k))


def _ffn_up_kernel(x_ref, g_ref, sh_ref, sc_ref, wg_ref, wu_ref, o_ref, h_ref):
    @pl.when(pl.program_id(1) == 0)
    def _():
        h_ref[...] = _norm_mod(x_ref[...], g_ref[...], sh_ref[...], sc_ref[...]).astype(BF16)

    h = h_ref[...]
    g = _dot(h, wg_ref[...])
    u = _dot(h, wu_ref[...])
    o_ref[...] = (g * jax.nn.sigmoid(g) * u).astype(BF16)


def _ffn_up(x, g, mod, k_shift, w_gu):
    tm, tn = 1024, 512
    nj = D_FF // tn
    return pl.pallas_call(
        _ffn_up_kernel,
        grid=(TOKENS // tm, nj),
        in_specs=[pl.BlockSpec((tm, D_MODEL), lambda i, j: (i, 0)),
                  pl.BlockSpec((1, D_MODEL), lambda i, j: (0, 0)),
                  _mod_spec(tm, k_shift), _mod_spec(tm, k_shift + 1),
                  pl.BlockSpec((D_MODEL, tn), lambda i, j: (0, j)),
                  pl.BlockSpec((D_MODEL, tn), lambda i, j: (0, j + nj))],
        out_specs=pl.BlockSpec((tm, tn), lambda i, j: (i, j)),
        out_shape=jax.ShapeDtypeStruct((TOKENS, D_FF), BF16),
        scratch_shapes=[pltpu.VMEM((tm, D_MODEL), BF16)],
        compiler_params=_params(("parallel", "arbitrary")),
        name="ffn_up",
    )(x, g, mod, mod, w_gu, w_gu)


def _ffn_down_kernel(a_ref, w_ref, x_ref, gt_ref, o_ref):
    y = _dot(a_ref[...], w_ref[...])
    o_ref[...] = x_ref[...] + (0.5 * gt_ref[...]) * y


def _ffn_down(act, w_d, x, mod, k_gate):
    tm, tn = 1024, 256
    nj = D_MODEL // tn
    return pl.pallas_call(
        _ffn_down_kernel,
        grid=(TOKENS // tm, nj),
        in_specs=[pl.BlockSpec((tm, D_FF), lambda i, j: (i, 0)),
                  pl.BlockSpec((D_FF, tn), lambda i, j: (0, j)),
                  pl.BlockSpec((tm, tn), lambda i, j: (i, j)),
                  pl.BlockSpec((None, 1, tn), lambda i, j: (i * tm // SEQ, 0, k_gate * nj + j))],
        out_specs=pl.BlockSpec((tm, tn), lambda i, j: (i, j)),
        out_shape=jax.ShapeDtypeStruct((TOKENS, D_MODEL), F32),
        compiler_params=_params(("parallel", "arbitrary")),
        name="ffn_down",
    )(act, w_d, x, mod)


def _rms(x, g):
    return (x * lax.rsqrt(jnp.mean(x * x, axis=-1, keepdims=True) + EPS)) * g


def _in_proj_kernel(x_ref, g_ref, sh_ref, sc_ref, tab_ref, wx_ref, gcq_ref, gckv_ref, wuq_ref, wukv_ref,
                    qa_ref, ka_ref, va_ref, qp_ref, kp_ref, vb_ref):
    h = _norm_mod(x_ref[...], g_ref[...], sh_ref[...], sc_ref[...]).astype(BF16)
    cos_a = tab_ref[:, 0 * LANES:1 * LANES]
    sin_a = tab_ref[:, 1 * LANES:2 * LANES]
    cos_b = tab_ref[:, 2 * LANES:3 * LANES]
    sin_lo = tab_ref[:, 3 * LANES:4 * LANES]
    sin_hi = tab_ref[:, 4 * LANES:5 * LANES]

    def rope_a(t):
        return t * cos_a + pltpu.roll(t, 64, 1) * sin_a

    def rope_b(t):
        return t * cos_b + pltpu.roll(t, 96, 1) * sin_lo + pltpu.roll(t, 32, 1) * sin_hi

    for hd in range(HA_Q):
        sl = slice(hd * HEAD_DIM, (hd + 1) * HEAD_DIM)
        qa_ref[:, sl] = rope_a(_dot(h, wx_ref[:, sl])).astype(BF16)
    for hd in range(HA_KV):
        sl = slice(hd * HEAD_DIM, (hd + 1) * HEAD_DIM)
        ka_ref[:, sl] = rope_a(_dot(h, wx_ref[:, COLS_QA + hd * HEAD_DIM:COLS_QA + (hd + 1) * HEAD_DIM])).astype(BF16)
    va_ref[...] = _dot(h, wx_ref[:, COLS_QA + COLS_KA:OFF_CQ]).astype(BF16)

    cq = _rms(_dot(h, wx_ref[:, OFF_CQ:OFF_CKV]), gcq_ref[...]).astype(BF16)
    for hd in range(HB):
        base = hd * MLA_PAD
        qp_ref[:, base:base + QK_NOPE] = _dot(cq, wuq_ref[:, base:base + QK_NOPE]).astype(BF16)
        qp_ref[:, base + QK_NOPE:base + MLA_PAD] = rope_b(
            _dot(cq, wuq_ref[:, base + QK_NOPE:base + MLA_PAD])).astype(BF16)

    ckv = _rms(_dot(h, wx_ref[:, OFF_CKV:OFF_KR]), gckv_ref[...]).astype(BF16)
    kr = rope_b(_dot(h, wx_ref[:, OFF_KR:OFF_KR + LANES])).astype(BF16)
    for hd in range(HB):
        base = hd * MLA_PAD
        kp_ref[:, base:base + QK_NOPE] = _dot(ckv, wukv_ref[:, hd * QK_NOPE:(hd + 1) * QK_NOPE]).astype(BF16)
        kp_ref[:, base + QK_NOPE:base + MLA_PAD] = kr
    vb_ref[...] = _dot(ckv, wukv_ref[:, HB * QK_NOPE:]).astype(BF16)


def _in_proj(x, g, mod, k_shift, tables, w_x, g_cq, g_ckv, w_uq, w_ukv):
    tm = 256
    const = lambda i: (0, 0)
    row = lambda i: (i, 0)
    nx = w_x.shape[1]
    outs = [COLS_QA, COLS_KA, COLS_VA, HB * MLA_PAD, HB * MLA_PAD, HB * V_DIM]
    return pl.pallas_call(
        _in_proj_kernel,
        grid=(TOKENS // tm,),
        in_specs=[pl.BlockSpec((tm, D_MODEL), row),
                  pl.BlockSpec((1, D_MODEL), const),
                  _mod_spec(tm, k_shift), _mod_spec(tm, k_shift + 1),
                  pl.BlockSpec((tm, N_ROPE_TABLES * LANES), row),
                  pl.BlockSpec((D_MODEL, nx), const),
                  pl.BlockSpec((1, Q_RANK), const),
                  pl.BlockSpec((1, KV_RANK), const),
                  pl.BlockSpec((Q_RANK, HB * MLA_PAD), const),
                  pl.BlockSpec((KV_RANK, HB * (QK_NOPE + V_DIM)), const)],
        out_specs=[pl.BlockSpec((tm, n), row) for n in outs],
        out_shape=[jax.ShapeDtypeStruct((TOKENS, n), BF16) for n in outs],
        compiler_params=_params(("parallel",)),
        name="in_proj",
    )(x, g, mod, mod, tables, w_x, g_cq, g_ckv, w_uq, w_ukv)


def _win_attn_kernel(sink_ref, q_ref, k_ref, v_ref, o_ref):
    hk = pl.program_id(1)
    n = pl.program_id(2)
    nb = SEQ // BLOCK
    start = pl.multiple_of(jnp.clip(n - 1, 0, nb - 3) * BLOCK, BLOCK)
    kw = k_ref[pl.ds(start, 3 * BLOCK), :]
    vw = v_ref[pl.ds(start, 3 * BLOCK), :]
    qpos = n * BLOCK + lax.broadcasted_iota(jnp.int32, (BLOCK, 3 * BLOCK), 0)
    kpos = start + lax.broadcasted_iota(jnp.int32, (BLOCK, 3 * BLOCK), 1)
    mask = jnp.abs(kpos - qpos) <= WINDOW
    for g in range(GROUP):
        sl = slice(g * HEAD_DIM, (g + 1) * HEAD_DIM)
        s = _dot_nt(q_ref[:, sl], kw) * (HEAD_DIM ** -0.5)
        s = jnp.where(mask, s, -1e30)
        sk = sink_ref[hk * GROUP + g]
        m = jnp.maximum(jnp.max(s, axis=-1, keepdims=True), sk)
        p = jnp.exp(s - m)
        denom = jnp.sum(p, axis=-1, keepdims=True) + jnp.exp(sk - m)
        p = (p / denom).astype(BF16)
        o_ref[:, sl] = _dot(p, vw).astype(BF16)


def _win_attn(qa, ka, va, sink):
    nb = SEQ // BLOCK
    gw = GROUP * HEAD_DIM
    return pl.pallas_call(
        _win_attn_kernel,
        grid=(BATCH, HA_KV, nb),
        in_specs=[pl.BlockSpec(memory_space=pltpu.SMEM),
                  pl.BlockSpec((BLOCK, gw), lambda b, h, n: (b * nb + n, h)),
                  pl.BlockSpec((SEQ, HEAD_DIM), lambda b, h, n: (b, h)),
                  pl.BlockSpec((SEQ, HEAD_DIM), lambda b, h, n: (b, h))],
        out_specs=pl.BlockSpec((BLOCK, gw), lambda b, h, n: (b * nb + n, h)),
        out_shape=jax.ShapeDtypeStruct((TOKENS, COLS_QA), BF16),
        compiler_params=_params(("parallel", "parallel", "arbitrary")),
        name="win_attn",
    )(sink, qa, ka, va)


def _mla_attn_kernel(q_ref, k_ref, v_ref, o_ref):
    scale = (QK_NOPE + QK_ROPE) ** -0.5
    s = _dot_nt(q_ref[...], k_ref[...]) * scale
    m = jnp.max(s, axis=-1, keepdims=True)
    p = jnp.exp(s - m)
    l = jnp.sum(p, axis=-1, keepdims=True)
    o = _dot(p.astype(BF16), v_ref[...])
    o_ref[...] = (o / l).astype(BF16)


def _mla_attn(q_pad, k_pad, v_b):
    tq = 256
    nq = SEQ // tq
    return pl.pallas_call(
        _mla_attn_kernel,
        grid=(BATCH, HB, nq),
        in_specs=[pl.BlockSpec((tq, MLA_PAD), lambda b, h, i: (b * nq + i, h)),
                  pl.BlockSpec((SEQ, MLA_PAD), lambda b, h, i: (b, h)),
                  pl.BlockSpec((SEQ, V_DIM), lambda b, h, i: (b, h))],
        out_specs=pl.BlockSpec((tq, V_DIM), lambda b, h, i: (b * nq + i, h)),
        out_shape=jax.ShapeDtypeStruct((TOKENS, HB * V_DIM), BF16),
        compiler_params=_params(("parallel", "parallel", "arbitrary")),
        name="mla_attn",
    )(q_pad, k_pad, v_b)


def _mix_out_kernel(x_ref, g_ref, sh_ref, sc_ref, gt_ref, oa_ref, ob_ref, woa_ref, wob_ref,
                    wga_ref, wgb_ref, bga_ref, bgb_ref, wout_ref, o_ref, h_ref, acc_ref):
    n = pl.program_id(1)

    @pl.when(n == 0)
    def _():
        h_ref[...] = _norm_mod(x_ref[...], g_ref[...], sh_ref[...], sc_ref[...]).astype(BF16)
        acc_ref[...] = jnp.zeros_like(acc_ref)

    h = h_ref[...]
    y_a = _dot(oa_ref[...], woa_ref[...])
    y_b = _dot(ob_ref[...], wob_ref[...])
    g_a = jax.nn.sigmoid(_dot(h, wga_ref[...]) + bga_ref[...])
    g_b = jax.nn.sigmoid(_dot(h, wgb_ref[...]) + bgb_ref[...])
    mix = (g_a * y_a + g_b * y_b).astype(BF16)
    acc_ref[...] += _dot(mix, wout_ref[...])

    @pl.when(n == pl.num_programs(1) - 1)
    def _():
        o_ref[...] = x_ref[...] + gt_ref[...] * acc_ref[...]


def _mix_out(x, g, mod, k_shift, o_a, o_b, w_oa, w_ob, w_gate, b_gate, w_out):
    tm, tc = 512, 512
    nc = D_MODEL // tc
    row = lambda i, n: (i, 0)
    col = lambda i, n: (0, n)
    col_hi = lambda i, n: (0, n + nc)
    return pl.pallas_call(
        _mix_out_kernel,
        grid=(TOKENS // tm, nc),
        in_specs=[pl.BlockSpec((tm, D_MODEL), row),
                  pl.BlockSpec((1, D_MODEL), lambda i, n: (0, 0)),
                  _mod_spec(tm, k_shift), _mod_spec(tm, k_shift + 1), _mod_spec(tm, k_shift + 2),
                  pl.BlockSpec((tm, COLS_QA), row),
                  pl.BlockSpec((tm, HB * V_DIM), row),
                  pl.BlockSpec((COLS_QA, tc), col),
                  pl.BlockSpec((HB * V_DIM, tc), col),
                  pl.BlockSpec((D_MODEL, tc), col),
                  pl.BlockSpec((D_MODEL, tc), col_hi),
                  pl.BlockSpec((1, tc), col),
                  pl.BlockSpec((1, tc), col_hi),
                  pl.BlockSpec((tc, D_MODEL), lambda i, n: (n, 0))],
        out_specs=pl.BlockSpec((tm, D_MODEL), row),
        out_shape=jax.ShapeDtypeStruct((TOKENS, D_MODEL), F32),
        scratch_shapes=[pltpu.VMEM((tm, D_MODEL), BF16), pltpu.VMEM((tm, D_MODEL), F32)],
        compiler_params=_params(("parallel", "arbitrary")),
        name="mix_out",
    )(x, g, mod, mod, mod, o_a, o_b, w_oa, w_ob, w_gate, w_gate, b_gate, b_gate, w_out)


def _final_norm_kernel(x_ref, g_ref, o_ref):
    o_ref[...] = _rms(x_ref[...], g_ref[...])


def _final_norm(x, g):
    tm = 512
    return pl.pallas_call(
        _final_norm_kernel,
        grid=(TOKENS // tm,),
        in_specs=[pl.BlockSpec((tm, D_MODEL), lambda i: (i, 0)),
                  pl.BlockSpec((1, D_MODEL), lambda i: (0, 0))],
        out_specs=pl.BlockSpec((tm, D_MODEL), lambda i: (i, 0)),
        out_shape=jax.ShapeDtypeStruct((TOKENS, D_MODEL), F32),
        compiler_params=_params(("parallel",)),
        name="final_norm",
    )(x, g)


def _pad_heads(w, n_heads, width, pad_to):
    k = w.shape[0]
    w = w.reshape(k, n_heads, width)
    w = jnp.pad(w, ((0, 0), (0, 0), (0, pad_to - width)))
    return w.reshape(k, n_heads * pad_to)


def _layer_weights(l, w_in, w_uq, w_ukv):
    w = w_in[l]
    kr_pad = jnp.pad(w[:, OFF_KR:OFF_GATE], ((0, 0), (0, LANES - QK_ROPE)))
    w_x = jnp.concatenate([w[:, :OFF_KR], kr_pad], axis=1).astype(BF16)
    w_gate = w[:, OFF_GATE:].astype(BF16)
    w_uq_pad = _pad_heads(w_uq[l], HB, QK_NOPE + QK_ROPE, MLA_PAD).astype(BF16)
    ukv = w_ukv[l].reshape(KV_RANK, HB, QK_NOPE + V_DIM)
    w_ukv_perm = jnp.concatenate([ukv[:, :, :QK_NOPE].reshape(KV_RANK, HB * QK_NOPE),
                                  ukv[:, :, QK_NOPE:].reshape(KV_RANK, HB * V_DIM)], axis=1).astype(BF16)
    return w_x, w_gate, w_uq_pad, w_ukv_perm


def kernel(x, c, positions, norm_g, w_ada, b_ada, w_ffn1_gu, w_ffn1_d, w_ffn2_gu, w_ffn2_d, w_in, b_gate, sink,
           g_cq, g_ckv, w_uq, w_ukv, w_oa, w_ob, w_out, g_final):
    tables = _rope_tables(positions)
    mods = _ada_mod(c, w_ada, b_ada)
    xt = x.reshape(TOKENS, D_MODEL)
    for l in range(DEPTH):
        mod = mods[l].reshape(BATCH, 1, N_MOD * D_MODEL)
        g = norm_g[l]
        w_x, w_gate, w_uq_pad, w_ukv_perm = _layer_weights(l, w_in, w_uq, w_ukv)

        act = _ffn_up(xt, g[0:1], mod, 0, w_ffn1_gu[l].astype(BF16))
        xt = _ffn_down(act, w_ffn1_d[l].astype(BF16), xt, mod, 2)

        qa, ka, va, q_pad, k_pad, v_b = _in_proj(xt, g[1:2], mod, 3, tables, w_x, g_cq[l][None, :],
                                                 g_ckv[l][None, :], w_uq_pad, w_ukv_perm)
        o_a = _win_attn(qa, ka, va, sink[l])
        o_b = _mla_attn(q_pad, k_pad, v_b)
        xt = _mix_out(xt, g[1:2], mod, 3, o_a, o_b, w_oa[l].astype(BF16), w_ob[l].astype(BF16), w_gate,
                      b_gate[l][None, :], w_out[l].astype(BF16))

        act = _ffn_up(xt, g[2:3], mod, 6, w_ffn2_gu[l].astype(BF16))
        xt = _ffn_down(act, w_ffn2_d[l].astype(BF16), xt, mod, 8)
    out = _final_norm(xt, g_final[None, :])
    return out.reshape(BATCH, SEQ, D_MODEL)
```

```python
import functools

import jax
import jax.numpy as jnp
import numpy as np
from jax import lax
from jax.experimental import pallas as pl
from jax.experimental.pallas import tpu as pltpu

D_MODEL = 2048
BATCH = 8
SEQ = 4096
DEPTH = 2
HA_Q = 8
HA_KV = 2
GROUP = HA_Q // HA_KV
HEAD_DIM = 128
WINDOW = 128
BLOCK = 128
HB = 8
QK_NOPE = 128
QK_ROPE = 64
V_DIM = 128
Q_RANK = 512
KV_RANK = 512
D_FF = 5632
ROPE_THETA = 10000.0
EPS = 1e-6
N_MOD = 9
TOKENS = BATCH * SEQ

COLS_QA = HA_Q * HEAD_DIM
COLS_KA = HA_KV * HEAD_DIM
COLS_VA = HA_KV * HEAD_DIM
OFF_CQ = COLS_QA + COLS_KA + COLS_VA
OFF_CKV = OFF_CQ + Q_RANK
OFF_KR = OFF_CKV + KV_RANK
OFF_GATE = OFF_KR + QK_ROPE

LANES = 128
MLA_PAD = 256
N_ROPE_TABLES = 5
VMEM_LIMIT = 56 * 1024 * 1024

BF16 = jnp.bfloat16
F32 = jnp.float32


def _params(sem):
    return pltpu.CompilerParams(dimension_semantics=sem, vmem_limit_bytes=VMEM_LIMIT)


def _dot(a, b):
    return jnp.dot(a, b, preferred_element_type=F32)


def _dot_nt(a, b):
    return lax.dot_general(a, b, (((1,), (1,)), ((), ())), preferred_element_type=F32)


def _norm_mod(x, g, shift, scale):
    y = x * lax.rsqrt(jnp.mean(x * x, axis=-1, keepdims=True) + EPS)
    return (y * g) * (1.0 + scale) + shift


def _rope_table_kernel(pos_ref, freq_ref, o_ref):
    ang = pos_ref[...].astype(F32) * freq_ref[...]
    cs = jnp.cos(ang)
    sn = jnp.sin(ang)
    lane = lax.broadcasted_iota(jnp.int32, cs.shape, 1)
    cs64 = pltpu.roll(cs, 64, 1)
    sn64 = pltpu.roll(sn, 64, 1)
    cs96 = pltpu.roll(cs, 96, 1)
    sn96 = pltpu.roll(sn, 96, 1)
    zero = jnp.zeros_like(cs)
    o_ref[:, 0 * LANES:1 * LANES] = jnp.where(lane < 64, cs, cs64)
    o_ref[:, 1 * LANES:2 * LANES] = jnp.where(lane < 64, -sn, sn64)
    o_ref[:, 2 * LANES:3 * LANES] = jnp.where(lane < 32, cs64, jnp.where(lane < 64, cs96, zero))
    o_ref[:, 3 * LANES:4 * LANES] = jnp.where(lane < 32, -sn64, zero)
    o_ref[:, 4 * LANES:5 * LANES] = jnp.where((lane >= 32) & (lane < 64), sn96, zero)


def _rope_tables(positions):
    tm = 1024
    fa = ROPE_THETA ** (-jnp.arange(0, HEAD_DIM, 2, dtype=F32) / HEAD_DIM)
    fb = ROPE_THETA ** (-jnp.arange(0, QK_ROPE, 2, dtype=F32) / QK_ROPE)
    freq = jnp.concatenate([fa, fb, jnp.zeros((32,), F32)])[None, :]
    pos = positions.reshape(TOKENS, 1)
    return pl.pallas_call(
        _rope_table_kernel,
        grid=(TOKENS // tm,),
        in_specs=[pl.BlockSpec((tm, 1), lambda i: (i, 0)),
                  pl.BlockSpec((1, LANES), lambda i: (0, 0))],
        out_specs=pl.BlockSpec((tm, N_ROPE_TABLES * LANES), lambda i: (i, 0)),
        out_shape=jax.ShapeDtypeStruct((TOKENS, N_ROPE_TABLES * LANES), F32),
        compiler_params=_params(("parallel",)),
        name="rope_tables",
    )(pos, freq)


def _ada_kernel(c_ref, w_ref, b_ref, o_ref):
    c = c_ref[...]
    c_act = (c * jax.nn.sigmoid(c)).astype(BF16)
    o_ref[...] = _dot(c_act, w_ref[...].astype(BF16)) + b_ref[...]


def _ada_mod(c, w_ada, b_ada):
    tn = 1024
    n = N_MOD * D_MODEL
    return pl.pallas_call(
        _ada_kernel,
        grid=(DEPTH, n // tn),
        in_specs=[pl.BlockSpec((BATCH, D_MODEL), lambda l, j: (0, 0)),
                  pl.BlockSpec((None, D_MODEL, tn), lambda l, j: (l, 0, j)),
                  pl.BlockSpec((None, 1, tn), lambda l, j: (l, 0, j))],
        out_specs=pl.BlockSpec((None, BATCH, tn), lambda l, j: (l, 0, j)),
        out_shape=jax.ShapeDtypeStruct((DEPTH, BATCH, n), F32),
        compiler_params=_params(("parallel", "parallel")),
        name="ada_mod",
    )(c, w_ada, b_ada.reshape(DEPTH, 1, n))


def _mod_spec(tm, k):
    return pl.BlockSpec((None, 1, D_MODEL), lambda i, *_: (i * tm // SEQ, 0, k))


def _ffn_up_kernel(x_ref, g_ref, sh_ref, sc_ref, wg_ref, wu_ref, o_ref, h_ref):
    @pl.when(pl.program_id(1) == 0)
    def _():
        h_ref[...] = _norm_mod(x_ref[...], g_ref[...], sh_ref[...], sc_ref[...]).astype(BF16)

    h = h_ref[...]
    g = _dot(h, wg_ref[...])
    u = _dot(h, wu_ref[...])
    o_ref[...] = (g * jax.nn.sigmoid(g) * u).astype(BF16)


def _ffn_up(x, g, mod, k_shift, w_gu):
    tm, tn = 1024, 512
    nj = D_FF // tn
    return pl.pallas_call(
        _ffn_up_kernel,
        grid=(TOKENS // tm, nj),
        in_specs=[pl.BlockSpec((tm, D_MODEL), lambda i, j: (i, 0)),
                  pl.BlockSpec((1, D_MODEL), lambda i, j: (0, 0)),
                  _mod_spec(tm, k_shift), _mod_spec(tm, k_shift + 1),
                  pl.BlockSpec((D_MODEL, tn), lambda i, j: (0, j)),
                  pl.BlockSpec((D_MODEL, tn), lambda i, j: (0, j + nj))],
        out_specs=pl.BlockSpec((tm, tn), lambda i, j: (i, j)),
        out_shape=jax.ShapeDtypeStruct((TOKENS, D_FF), BF16),
        scratch_shapes=[pltpu.VMEM((tm, D_MODEL), BF16)],
        compiler_params=_params(("parallel", "arbitrary")),
        name="ffn_up",
    )(x, g, mod, mod, w_gu, w_gu)


def _ffn_down_kernel(a_ref, w_ref, x_ref, gt_ref, o_ref):
    y = _dot(a_ref[...], w_ref[...])
    o_ref[...] = x_ref[...] + (0.5 * gt_ref[...]) * y


def _ffn_down(act, w_d, x, mod, k_gate):
    tm, tn = 1024, 256
    nj = D_MODEL // tn
    return pl.pallas_call(
        _ffn_down_kernel,
        grid=(TOKENS // tm, nj),
        in_specs=[pl.BlockSpec((tm, D_FF), lambda i, j: (i, 0)),
                  pl.BlockSpec((D_FF, tn), lambda i, j: (0, j)),
                  pl.BlockSpec((tm, tn), lambda i, j: (i, j)),
                  pl.BlockSpec((None, 1, tn), lambda i, j: (i * tm // SEQ, 0, k_gate * nj + j))],
        out_specs=pl.BlockSpec((tm, tn), lambda i, j: (i, j)),
        out_shape=jax.ShapeDtypeStruct((TOKENS, D_MODEL), F32),
        compiler_params=_params(("parallel", "arbitrary")),
        name="ffn_down",
    )(act, w_d, x, mod)


def _rms(x, g):
    return (x * lax.rsqrt(jnp.mean(x * x, axis=-1, keepdims=True) + EPS)) * g


def _in_proj_kernel(x_ref, g_ref, sh_ref, sc_ref, tab_ref, wx_ref, gcq_ref, gckv_ref, wuq_ref, wukv_ref,
                    qa_ref, ka_ref, va_ref, qp_ref, kp_ref, vb_ref):
    h = _norm_mod(x_ref[...], g_ref[...], sh_ref[...], sc_ref[...]).astype(BF16)
    cos_a = tab_ref[:, 0 * LANES:1 * LANES]
    sin_a = tab_ref[:, 1 * LANES:2 * LANES]
    cos_b = tab_ref[:, 2 * LANES:3 * LANES]
    sin_lo = tab_ref[:, 3 * LANES:4 * LANES]
    sin_hi = tab_ref[:, 4 * LANES:5 * LANES]

    def rope_a(t):
        return t * cos_a + pltpu.roll(t, 64, 1) * sin_a

    def rope_b(t):
        return t * cos_b + pltpu.roll(t, 96, 1) * sin_lo + pltpu.roll(t, 32, 1) * sin_hi

    proj = _dot(h, wx_ref[...])

    for hd in range(HA_Q):
        sl = slice(hd * HEAD_DIM, (hd + 1) * HEAD_DIM)
        qa_ref[:, sl] = rope_a(proj[:, sl]).astype(BF16)
    for hd in range(HA_KV):
        sl = slice(hd * HEAD_DIM, (hd + 1) * HEAD_DIM)
        ka_ref[:, sl] = rope_a(proj[:, COLS_QA + hd * HEAD_DIM:COLS_QA + (hd + 1) * HEAD_DIM]).astype(BF16)
    va_ref[...] = proj[:, COLS_QA + COLS_KA:OFF_CQ].astype(BF16)

    cq = _rms(proj[:, OFF_CQ:OFF_CKV], gcq_ref[...]).astype(BF16)
    q_all = _dot(cq, wuq_ref[...])
    for hd in range(HB):
        base = hd * MLA_PAD
        qp_ref[base:base + QK_NOPE, :] = q_all[:, base:base + QK_NOPE].T.astype(BF16)
        qp_ref[base + QK_NOPE:base + MLA_PAD, :] = rope_b(q_all[:, base + QK_NOPE:base + MLA_PAD]).T.astype(BF16)

    ckv = _rms(proj[:, OFF_CKV:OFF_KR], gckv_ref[...]).astype(BF16)
    kv_all = _dot(ckv, wukv_ref[...])
    kr = rope_b(proj[:, OFF_KR:OFF_KR + LANES]).astype(BF16)
    for hd in range(HB):
        base = hd * MLA_PAD
        kp_ref[:, base:base + QK_NOPE] = kv_all[:, hd * QK_NOPE:(hd + 1) * QK_NOPE].astype(BF16)
        kp_ref[:, base + QK_NOPE:base + MLA_PAD] = kr
        vsl = slice(HB * QK_NOPE + hd * V_DIM, HB * QK_NOPE + (hd + 1) * V_DIM)
        vb_ref[hd * V_DIM:(hd + 1) * V_DIM, :] = kv_all[:, vsl].T.astype(BF16)


def _in_proj(x, g, mod, k_shift, tables, w_x, g_cq, g_ckv, w_uq, w_ukv):
    tm = 256
    const = lambda i: (0, 0)
    row = lambda i: (i, 0)
    nx = w_x.shape[1]
    col = lambda i: (0, i)
    outs = [(COLS_QA, True), (COLS_KA, True), (COLS_VA, True), (HB * MLA_PAD, False), (HB * MLA_PAD, True),
            (HB * V_DIM, False)]
    out_specs = [pl.BlockSpec((tm, n), row) if tok else pl.BlockSpec((n, tm), col) for n, tok in outs]
    out_shape = [jax.ShapeDtypeStruct((TOKENS, n) if tok else (n, TOKENS), BF16) for n, tok in outs]
    return pl.pallas_call(
        _in_proj_kernel,
        grid=(TOKENS // tm,),
        in_specs=[pl.BlockSpec((tm, D_MODEL), row),
                  pl.BlockSpec((1, D_MODEL), const),
                  _mod_spec(tm, k_shift), _mod_spec(tm, k_shift + 1),
                  pl.BlockSpec((tm, N_ROPE_TABLES * LANES), row),
                  pl.BlockSpec((D_MODEL, nx), const),
                  pl.BlockSpec((1, Q_RANK), const),
                  pl.BlockSpec((1, KV_RANK), const),
                  pl.BlockSpec((Q_RANK, HB * MLA_PAD), const),
                  pl.BlockSpec((KV_RANK, HB * (QK_NOPE + V_DIM)), const)],
        out_specs=out_specs,
        out_shape=out_shape,
        compiler_params=_params(("parallel",)),
        name="in_proj",
    )(x, g, mod, mod, tables, w_x, g_cq, g_ckv, w_uq, w_ukv)


WIN_BLOCKS_PER_STEP = 4


def _win_attn_kernel(sink_ref, q_ref, k_ref, v_ref, o_ref):
    hk = pl.program_id(1)
    nb = SEQ // BLOCK
    rows = GROUP * BLOCK
    row = lax.broadcasted_iota(jnp.int32, (rows, 1), 0)
    sk = jnp.zeros((rows, 1), F32)
    for g in range(GROUP):
        sk = jnp.where(row // BLOCK == g, sink_ref[hk * GROUP + g], sk)
    q_in_block = lax.broadcasted_iota(jnp.int32, (rows, 3 * BLOCK), 0) % BLOCK
    k_in_win = lax.broadcasted_iota(jnp.int32, (rows, 3 * BLOCK), 1)
    for j in range(WIN_BLOCKS_PER_STEP):
        n = pl.program_id(2) * WIN_BLOCKS_PER_STEP + j
        start = pl.multiple_of(jnp.clip(n - 1, 0, nb - 3) * BLOCK, BLOCK)
        kw = k_ref[pl.ds(start, 3 * BLOCK), :]
        vw = v_ref[pl.ds(start, 3 * BLOCK), :]
        mask = jnp.abs((start + k_in_win) - (n * BLOCK + q_in_block)) <= WINDOW
        q = jnp.concatenate([q_ref[j * BLOCK:(j + 1) * BLOCK, g * HEAD_DIM:(g + 1) * HEAD_DIM]
                             for g in range(GROUP)], axis=0)
        s = _dot_nt(q, kw) * (HEAD_DIM ** -0.5)
        s = jnp.where(mask, s, -1e30)
        m = jnp.maximum(jnp.max(s, axis=-1, keepdims=True), sk)
        p = jnp.exp(s - m)
        denom = jnp.sum(p, axis=-1, keepdims=True) + jnp.exp(sk - m)
        p = (p * (1.0 / denom)).astype(BF16)
        o = _dot(p, vw).astype(BF16)
        for g in range(GROUP):
            o_ref[j * BLOCK:(j + 1) * BLOCK, g * HEAD_DIM:(g + 1) * HEAD_DIM] = o[g * BLOCK:(g + 1) * BLOCK]


def _win_attn(qa, ka, va, sink):
    tq = WIN_BLOCKS_PER_STEP * BLOCK
    nq = SEQ // tq
    gw = GROUP * HEAD_DIM
    return pl.pallas_call(
        _win_attn_kernel,
        grid=(BATCH, HA_KV, nq),
        in_specs=[pl.BlockSpec(memory_space=pltpu.SMEM),
                  pl.BlockSpec((tq, gw), lambda b, h, n: (b * nq + n, h)),
                  pl.BlockSpec((SEQ, HEAD_DIM), lambda b, h, n: (b, h)),
                  pl.BlockSpec((SEQ, HEAD_DIM), lambda b, h, n: (b, h))],
        out_specs=pl.BlockSpec((tq, gw), lambda b, h, n: (b * nq + n, h)),
        out_shape=jax.ShapeDtypeStruct((TOKENS, COLS_QA), BF16),
        compiler_params=_params(("parallel", "parallel", "arbitrary")),
        name="win_attn",
    )(sink, qa, ka, va)


MLA_KEY_CHUNK = 512


def _mla_attn_kernel(qt_ref, k_ref, vt_ref, o_ref):
    c = (QK_NOPE + QK_ROPE) ** -0.5 * float(np.log2(np.e))
    qt = qt_ref[...]
    tq = qt.shape[1]
    m = jnp.full((1, tq), -1e30, F32)
    l = jnp.zeros((1, tq), F32)
    acc = jnp.zeros((V_DIM, tq), F32)
    n_chunks = SEQ // MLA_KEY_CHUNK
    chunk = lambda ci: slice(ci * MLA_KEY_CHUNK, (ci + 1) * MLA_KEY_CHUNK)
    st_next = _dot(k_ref[chunk(0), :], qt)
    for ci in range(n_chunks):
        ks = chunk(ci)
        st = st_next
        if ci + 1 < n_chunks:
            st_next = _dot(k_ref[chunk(ci + 1), :], qt)
        m_new = jnp.maximum(m, jnp.max(st, axis=0, keepdims=True))
        alpha = jnp.exp2((m - m_new) * c)
        p = jnp.exp2((st - m_new) * c)
        l = l * alpha + jnp.sum(p, axis=0, keepdims=True)
        acc = acc * alpha + _dot(vt_ref[:, ks], p.astype(BF16))
        m = m_new
    o_ref[...] = (acc * (1.0 / l)).T.astype(BF16)


def _mla_attn(q_pad_t, k_pad, v_t):
    tq = 1024
    nq = SEQ // tq
    return pl.pallas_call(
        _mla_attn_kernel,
        grid=(BATCH, HB, nq),
        in_specs=[pl.BlockSpec((MLA_PAD, tq), lambda b, h, i: (h, b * nq + i)),
                  pl.BlockSpec((SEQ, MLA_PAD), lambda b, h, i: (b, h)),
                  pl.BlockSpec((V_DIM, SEQ), lambda b, h, i: (h, b))],
        out_specs=pl.BlockSpec((tq, V_DIM), lambda b, h, i: (b * nq + i, h)),
        out_shape=jax.ShapeDtypeStruct((TOKENS, HB * V_DIM), BF16),
        compiler_params=_params(("parallel", "parallel", "arbitrary")),
        name="mla_attn",
    )(q_pad_t, k_pad, v_t)


def _mix_out_kernel(x_ref, g_ref, sh_ref, sc_ref, gt_ref, oa_ref, ob_ref, woa_ref, wob_ref,
                    wga_ref, wgb_ref, bga_ref, bgb_ref, wout_ref, o_ref, h_ref, acc_ref):
    n = pl.program_id(1)

    @pl.when(n == 0)
    def _():
        h_ref[...] = _norm_mod(x_ref[...], g_ref[...], sh_ref[...], sc_ref[...]).astype(BF16)
        acc_ref[...] = jnp.zeros_like(acc_ref)

    h = h_ref[...]
    y_a = _dot(oa_ref[...], woa_ref[...])
    y_b = _dot(ob_ref[...], wob_ref[...])
    g_a = jax.nn.sigmoid(_dot(h, wga_ref[...]) + bga_ref[...])
    g_b = jax.nn.sigmoid(_dot(h, wgb_ref[...]) + bgb_ref[...])
    mix = (g_a * y_a + g_b * y_b).astype(BF16)
    acc_ref[...] += _dot(mix, wout_ref[...])

    @pl.when(n == pl.num_programs(1) - 1)
    def _():
        o_ref[...] = x_ref[...] + gt_ref[...] * acc_ref[...]


def _mix_out(x, g, mod, k_shift, o_a, o_b, w_oa, w_ob, w_gate, b_gate, w_out):
    tm, tc = 512, 512
    nc = D_MODEL // tc
    row = lambda i, n: (i, 0)
    col = lambda i, n: (0, n)
    col_hi = lambda i, n: (0, n + nc)
    return pl.pallas_call(
        _mix_out_kernel,
        grid=(TOKENS // tm, nc),
        in_specs=[pl.BlockSpec((tm, D_MODEL), row),
                  pl.BlockSpec((1, D_MODEL), lambda i, n: (0, 0)),
                  _mod_spec(tm, k_shift), _mod_spec(tm, k_shift + 1), _mod_spec(tm, k_shift + 2),
                  pl.BlockSpec((tm, COLS_QA), row),
                  pl.BlockSpec((tm, HB * V_DIM), row),
                  pl.BlockSpec((COLS_QA, tc), col),
                  pl.BlockSpec((HB * V_DIM, tc), col),
                  pl.BlockSpec((D_MODEL, tc), col),
                  pl.BlockSpec((D_MODEL, tc), col_hi),
                  pl.BlockSpec((1, tc), col),
                  pl.BlockSpec((1, tc), col_hi),
                  pl.BlockSpec((tc, D_MODEL), lambda i, n: (n, 0))],
        out_specs=pl.BlockSpec((tm, D_MODEL), row),
        out_shape=jax.ShapeDtypeStruct((TOKENS, D_MODEL), F32),
        scratch_shapes=[pltpu.VMEM((tm, D_MODEL), BF16), pltpu.VMEM((tm, D_MODEL), F32)],
        compiler_params=_params(("parallel", "arbitrary")),
        name="mix_out",
    )(x, g, mod, mod, mod, o_a, o_b, w_oa, w_ob, w_gate, w_gate, b_gate, b_gate, w_out)


def _final_norm_kernel(x_ref, g_ref, o_ref):
    o_ref[...] = _rms(x_ref[...], g_ref[...])


def _final_norm(x, g):
    tm = 512
    return pl.pallas_call(
        _final_norm_kernel,
        grid=(TOKENS // tm,),
        in_specs=[pl.BlockSpec((tm, D_MODEL), lambda i: (i, 0)),
                  pl.BlockSpec((1, D_MODEL), lambda i: (0, 0))],
        out_specs=pl.BlockSpec((tm, D_MODEL), lambda i: (i, 0)),
        out_shape=jax.ShapeDtypeStruct((TOKENS, D_MODEL), F32),
        compiler_params=_params(("parallel",)),
        name="final_norm",
    )(x, g)


def _pad_heads(w, n_heads, width, pad_to):
    k = w.shape[0]
    w = w.reshape(k, n_heads, width)
    w = jnp.pad(w, ((0, 0), (0, 0), (0, pad_to - width)))
    return w.reshape(k, n_heads * pad_to)


def _layer_weights(l, w_in, w_uq, w_ukv):
    w = w_in[l]
    kr_pad = jnp.pad(w[:, OFF_KR:OFF_GATE], ((0, 0), (0, LANES - QK_ROPE)))
    w_x = jnp.concatenate([w[:, :OFF_KR], kr_pad], axis=1).astype(BF16)
    w_gate = w[:, OFF_GATE:].astype(BF16)
    w_uq_pad = _pad_heads(w_uq[l], HB, QK_NOPE + QK_ROPE, MLA_PAD).astype(BF16)
    ukv = w_ukv[l].reshape(KV_RANK, HB, QK_NOPE + V_DIM)
    w_ukv_perm = jnp.concatenate([ukv[:, :, :QK_NOPE].reshape(KV_RANK, HB * QK_NOPE),
                                  ukv[:, :, QK_NOPE:].reshape(KV_RANK, HB * V_DIM)], axis=1).astype(BF16)
    return w_x, w_gate, w_uq_pad, w_ukv_perm


def kernel(x, c, positions, norm_g, w_ada, b_ada, w_ffn1_gu, w_ffn1_d, w_ffn2_gu, w_ffn2_d, w_in, b_gate, sink,
           g_cq, g_ckv, w_uq, w_ukv, w_oa, w_ob, w_out, g_final):
    tables = _rope_tables(positions)
    mods = _ada_mod(c, w_ada, b_ada)
    xt = x.reshape(TOKENS, D_MODEL)
    for l in range(DEPTH):
        mod = mods[l].reshape(BATCH, 1, N_MOD * D_MODEL)
        g = norm_g[l]
        w_x, w_gate, w_uq_pad, w_ukv_perm = _layer_weights(l, w_in, w_uq, w_ukv)

        act = _ffn_up(xt, g[0:1], mod, 0, w_ffn1_gu[l].astype(BF16))
        xt = _ffn_down(act, w_ffn1_d[l].astype(BF16), xt, mod, 2)

        qa, ka, va, q_pad, k_pad, v_b = _in_proj(xt, g[1:2], mod, 3, tables, w_x, g_cq[l][None, :],
                                                 g_ckv[l][None, :], w_uq_pad, w_ukv_perm)
        o_a = _win_attn(qa, ka, va, sink[l])
        o_b = _mla_attn(q_pad, k_pad, v_b)
        xt = _mix_out(xt, g[1:2], mod, 3, o_a, o_b, w_oa[l].astype(BF16), w_ob[l].astype(BF16), w_gate,
                      b_gate[l][None, :], w_out[l].astype(BF16))

        act = _ffn_up(xt, g[2:3], mod, 6, w_ffn2_gu[l].astype(BF16))
        xt = _ffn_down(act, w_ffn2_d[l].astype(BF16), xt, mod, 8)
    out = _final_norm(xt, g_final[None, :])
    return out.reshape(BATCH, SEQ, D_MODEL)
```

```python
import functools

import jax
import jax.numpy as jnp
import numpy as np
from jax import lax
from jax.experimental import pallas as pl
from jax.experimental.pallas import tpu as pltpu

D_MODEL = 2048
BATCH = 8
SEQ = 4096
DEPTH = 2
HA_Q = 8
HA_KV = 2
GROUP = HA_Q // HA_KV
HEAD_DIM = 128
WINDOW = 128
BLOCK = 128
HB = 8
QK_NOPE = 128
QK_ROPE = 64
V_DIM = 128
Q_RANK = 512
KV_RANK = 512
D_FF = 5632
ROPE_THETA = 10000.0
EPS = 1e-6
N_MOD = 9
TOKENS = BATCH * SEQ

COLS_QA = HA_Q * HEAD_DIM
COLS_KA = HA_KV * HEAD_DIM
COLS_VA = HA_KV * HEAD_DIM
OFF_CQ = COLS_QA + COLS_KA + COLS_VA
OFF_CKV = OFF_CQ + Q_RANK
OFF_KR = OFF_CKV + KV_RANK
OFF_GATE = OFF_KR + QK_ROPE

LANES = 128
MLA_PAD = 256
N_ROPE_TABLES = 5
VMEM_LIMIT = 56 * 1024 * 1024

BF16 = jnp.bfloat16
F32 = jnp.float32


def _params(sem):
    return pltpu.CompilerParams(dimension_semantics=sem, vmem_limit_bytes=VMEM_LIMIT)


def _dot(a, b):
    return jnp.dot(a, b, preferred_element_type=F32)


def _dot_nt(a, b):
    return lax.dot_general(a, b, (((1,), (1,)), ((), ())), preferred_element_type=F32)


def _norm_mod(x, g, shift, scale):
    y = x * lax.rsqrt(jnp.mean(x * x, axis=-1, keepdims=True) + EPS)
    return (y * g) * (1.0 + scale) + shift


def _rope_table_kernel(pos_ref, freq_ref, o_ref):
    ang = pos_ref[...].astype(F32) * freq_ref[...]
    cs = jnp.cos(ang)
    sn = jnp.sin(ang)
    lane = lax.broadcasted_iota(jnp.int32, cs.shape, 1)
    cs64 = pltpu.roll(cs, 64, 1)
    sn64 = pltpu.roll(sn, 64, 1)
    cs96 = pltpu.roll(cs, 96, 1)
    sn96 = pltpu.roll(sn, 96, 1)
    zero = jnp.zeros_like(cs)
    o_ref[:, 0 * LANES:1 * LANES] = jnp.where(lane < 64, cs, cs64)
    o_ref[:, 1 * LANES:2 * LANES] = jnp.where(lane < 64, -sn, sn64)
    o_ref[:, 2 * LANES:3 * LANES] = jnp.where(lane < 32, cs64, jnp.where(lane < 64, cs96, zero))
    o_ref[:, 3 * LANES:4 * LANES] = jnp.where(lane < 32, -sn64, zero)
    o_ref[:, 4 * LANES:5 * LANES] = jnp.where((lane >= 32) & (lane < 64), sn96, zero)


def _rope_tables(positions):
    tm = 1024
    fa = ROPE_THETA ** (-jnp.arange(0, HEAD_DIM, 2, dtype=F32) / HEAD_DIM)
    fb = ROPE_THETA ** (-jnp.arange(0, QK_ROPE, 2, dtype=F32) / QK_ROPE)
    freq = jnp.concatenate([fa, fb, jnp.zeros((32,), F32)])[None, :]
    pos = positions.reshape(TOKENS, 1)
    return pl.pallas_call(
        _rope_table_kernel,
        grid=(TOKENS // tm,),
        in_specs=[pl.BlockSpec((tm, 1), lambda i: (i, 0)),
                  pl.BlockSpec((1, LANES), lambda i: (0, 0))],
        out_specs=pl.BlockSpec((tm, N_ROPE_TABLES * LANES), lambda i: (i, 0)),
        out_shape=jax.ShapeDtypeStruct((TOKENS, N_ROPE_TABLES * LANES), F32),
        compiler_params=_params(("parallel",)),
        name="rope_tables",
    )(pos, freq)


def _ada_kernel(c_ref, w_ref, b_ref, o_ref):
    c = c_ref[...]
    c_act = (c * jax.nn.sigmoid(c)).astype(BF16)
    o_ref[...] = _dot(c_act, w_ref[...].astype(BF16)) + b_ref[...]


def _ada_mod(c, w_ada, b_ada):
    tn = 1024
    n = N_MOD * D_MODEL
    return pl.pallas_call(
        _ada_kernel,
        grid=(DEPTH, n // tn),
        in_specs=[pl.BlockSpec((BATCH, D_MODEL), lambda l, j: (0, 0)),
                  pl.BlockSpec((None, D_MODEL, tn), lambda l, j: (l, 0, j)),
                  pl.BlockSpec((None, 1, tn), lambda l, j: (l, 0, j))],
        out_specs=pl.BlockSpec((None, BATCH, tn), lambda l, j: (l, 0, j)),
        out_shape=jax.ShapeDtypeStruct((DEPTH, BATCH, n), F32),
        compiler_params=_params(("parallel", "parallel")),
        name="ada_mod",
    )(c, w_ada, b_ada.reshape(DEPTH, 1, n))


def _mod_spec(tm, k):
    return pl.BlockSpec((None, 1, D_MODEL), lambda i, *_: (i * tm // SEQ, 0, k))


FFN_ROW_CHUNK = 256


def _ffn_up_kernel(x_ref, g_ref, sh_ref, sc_ref, wg_ref, wu_ref, o_ref, h_ref):
    def swiglu(h):
        g = _dot(h, wg_ref[...])
        u = _dot(h, wu_ref[...])
        return (g * jax.nn.sigmoid(g) * u).astype(BF16)

    @pl.when(pl.program_id(1) == 0)
    def _():
        for r in range(0, x_ref.shape[0], FFN_ROW_CHUNK):
            rows = slice(r, r + FFN_ROW_CHUNK)
            h = _norm_mod(x_ref[rows, :], g_ref[...], sh_ref[...], sc_ref[...]).astype(BF16)
            h_ref[rows, :] = h
            o_ref[rows, :] = swiglu(h)

    @pl.when(pl.program_id(1) != 0)
    def _():
        o_ref[...] = swiglu(h_ref[...])


def _ffn_up(x, g, mod, k_shift, w_gu):
    tm, tn = 1024, 512
    nj = D_FF // tn
    return pl.pallas_call(
        _ffn_up_kernel,
        grid=(TOKENS // tm, nj),
        in_specs=[pl.BlockSpec((tm, D_MODEL), lambda i, j: (i, 0)),
                  pl.BlockSpec((1, D_MODEL), lambda i, j: (0, 0)),
                  _mod_spec(tm, k_shift), _mod_spec(tm, k_shift + 1),
                  pl.BlockSpec((D_MODEL, tn), lambda i, j: (0, j)),
                  pl.BlockSpec((D_MODEL, tn), lambda i, j: (0, j + nj))],
        out_specs=pl.BlockSpec((tm, tn), lambda i, j: (i, j)),
        out_shape=jax.ShapeDtypeStruct((TOKENS, D_FF), BF16),
        scratch_shapes=[pltpu.VMEM((tm, D_MODEL), BF16)],
        compiler_params=_params(("parallel", "arbitrary")),
        name="ffn_up",
    )(x, g, mod, mod, w_gu, w_gu)


def _ffn_down_kernel(a_ref, w_ref, x_ref, gt_ref, o_ref):
    y = _dot(a_ref[...], w_ref[...])
    o_ref[...] = x_ref[...] + (0.5 * gt_ref[...]) * y


def _ffn_down(act, w_d, x, mod, k_gate):
    tm, tn = 1024, 512
    nj = D_MODEL // tn
    return pl.pallas_call(
        _ffn_down_kernel,
        grid=(TOKENS // tm, nj),
        in_specs=[pl.BlockSpec((tm, D_FF), lambda i, j: (i, 0)),
                  pl.BlockSpec((D_FF, tn), lambda i, j: (0, j)),
                  pl.BlockSpec((tm, tn), lambda i, j: (i, j)),
                  pl.BlockSpec((None, 1, tn), lambda i, j: (i * tm // SEQ, 0, k_gate * nj + j))],
        out_specs=pl.BlockSpec((tm, tn), lambda i, j: (i, j)),
        out_shape=jax.ShapeDtypeStruct((TOKENS, D_MODEL), F32),
        compiler_params=_params(("parallel", "arbitrary")),
        name="ffn_down",
    )(act, w_d, x, mod)


def _rms(x, g):
    return (x * lax.rsqrt(jnp.mean(x * x, axis=-1, keepdims=True) + EPS)) * g


def _in_proj_kernel(x_ref, g_ref, sh_ref, sc_ref, tab_ref, wx_ref, gcq_ref, gckv_ref, wuq_ref, wukv_ref,
                    h_ref, qa_ref, ka_ref, va_ref, qp_ref, kp_ref, vb_ref):
    h = _norm_mod(x_ref[...], g_ref[...], sh_ref[...], sc_ref[...]).astype(BF16)
    h_ref[...] = h
    cos_a = tab_ref[:, 0 * LANES:1 * LANES]
    sin_a = tab_ref[:, 1 * LANES:2 * LANES]
    cos_b = tab_ref[:, 2 * LANES:3 * LANES]
    sin_lo = tab_ref[:, 3 * LANES:4 * LANES]
    sin_hi = tab_ref[:, 4 * LANES:5 * LANES]

    def rope_a(t):
        return t * cos_a + pltpu.roll(t, 64, 1) * sin_a

    def rope_b(t):
        return t * cos_b + pltpu.roll(t, 96, 1) * sin_lo + pltpu.roll(t, 32, 1) * sin_hi

    proj = _dot(h, wx_ref[...])

    for hd in range(HA_Q):
        sl = slice(hd * HEAD_DIM, (hd + 1) * HEAD_DIM)
        qa_ref[:, sl] = rope_a(proj[:, sl]).astype(BF16)
    for hd in range(HA_KV):
        sl = slice(hd * HEAD_DIM, (hd + 1) * HEAD_DIM)
        ka_ref[:, sl] = rope_a(proj[:, COLS_QA + hd * HEAD_DIM:COLS_QA + (hd + 1) * HEAD_DIM]).astype(BF16)
    va_ref[...] = proj[:, COLS_QA + COLS_KA:OFF_CQ].astype(BF16)

    cq = _rms(proj[:, OFF_CQ:OFF_CKV], gcq_ref[...]).astype(BF16)
    q_all = _dot(cq, wuq_ref[...])
    for hd in range(HB):
        base = hd * MLA_PAD
        qp_ref[base:base + QK_NOPE, :] = q_all[:, base:base + QK_NOPE].T.astype(BF16)
        qp_ref[base + QK_NOPE:base + MLA_PAD, :] = rope_b(q_all[:, base + QK_NOPE:base + MLA_PAD]).T.astype(BF16)

    ckv = _rms(proj[:, OFF_CKV:OFF_KR], gckv_ref[...]).astype(BF16)
    kv_all = _dot(ckv, wukv_ref[...])
    kr = rope_b(proj[:, OFF_KR:OFF_KR + LANES]).astype(BF16)
    for hd in range(HB):
        base = hd * MLA_PAD
        kp_ref[:, base:base + QK_NOPE] = kv_all[:, hd * QK_NOPE:(hd + 1) * QK_NOPE].astype(BF16)
        kp_ref[:, base + QK_NOPE:base + MLA_PAD] = kr
        vsl = slice(HB * QK_NOPE + hd * V_DIM, HB * QK_NOPE + (hd + 1) * V_DIM)
        vb_ref[hd * V_DIM:(hd + 1) * V_DIM, :] = kv_all[:, vsl].T.astype(BF16)


def _in_proj(x, g, mod, k_shift, tables, w_x, g_cq, g_ckv, w_uq, w_ukv):
    tm = 256
    const = lambda i: (0, 0)
    row = lambda i: (i, 0)
    nx = w_x.shape[1]
    col = lambda i: (0, i)
    outs = [(D_MODEL, True), (COLS_QA, True), (COLS_KA, True), (COLS_VA, True), (HB * MLA_PAD, False),
            (HB * MLA_PAD, True), (HB * V_DIM, False)]
    out_specs = [pl.BlockSpec((tm, n), row) if tok else pl.BlockSpec((n, tm), col) for n, tok in outs]
    out_shape = [jax.ShapeDtypeStruct((TOKENS, n) if tok else (n, TOKENS), BF16) for n, tok in outs]
    return pl.pallas_call(
        _in_proj_kernel,
        grid=(TOKENS // tm,),
        in_specs=[pl.BlockSpec((tm, D_MODEL), row),
                  pl.BlockSpec((1, D_MODEL), const),
                  _mod_spec(tm, k_shift), _mod_spec(tm, k_shift + 1),
                  pl.BlockSpec((tm, N_ROPE_TABLES * LANES), row),
                  pl.BlockSpec((D_MODEL, nx), const),
                  pl.BlockSpec((1, Q_RANK), const),
                  pl.BlockSpec((1, KV_RANK), const),
                  pl.BlockSpec((Q_RANK, HB * MLA_PAD), const),
                  pl.BlockSpec((KV_RANK, HB * (QK_NOPE + V_DIM)), const)],
        out_specs=out_specs,
        out_shape=out_shape,
        compiler_params=_params(("parallel",)),
        name="in_proj",
    )(x, g, mod, mod, tables, w_x, g_cq, g_ckv, w_uq, w_ukv)


WIN_BLOCKS_PER_STEP = 4


def _win_attn_kernel(sink_ref, q_ref, k_ref, v_ref, o_ref):
    hk = pl.program_id(1)
    nb = SEQ // BLOCK
    rows = GROUP * BLOCK
    row = lax.broadcasted_iota(jnp.int32, (rows, 1), 0)
    sk = jnp.zeros((rows, 1), F32)
    for g in range(GROUP):
        sk = jnp.where(row // BLOCK == g, sink_ref[hk * GROUP + g], sk)
    rel = (lax.broadcasted_iota(jnp.int32, (rows, 3 * BLOCK), 1)
           - lax.broadcasted_iota(jnp.int32, (rows, 3 * BLOCK), 0) % BLOCK)
    for j in range(WIN_BLOCKS_PER_STEP):
        n = pl.program_id(2) * WIN_BLOCKS_PER_STEP + j
        start = pl.multiple_of(jnp.clip(n - 1, 0, nb - 3) * BLOCK, BLOCK)
        kw = k_ref[pl.ds(start, 3 * BLOCK), :]
        vw = v_ref[pl.ds(start, 3 * BLOCK), :]
        mask = jnp.abs(rel + (start - n * BLOCK)) <= WINDOW
        q = jnp.concatenate([q_ref[j * BLOCK:(j + 1) * BLOCK, g * HEAD_DIM:(g + 1) * HEAD_DIM]
                             for g in range(GROUP)], axis=0)
        s = _dot_nt(q, kw) * (HEAD_DIM ** -0.5)
        s = jnp.where(mask, s, -1e30)
        m = jnp.maximum(jnp.max(s, axis=-1, keepdims=True), sk)
        p = jnp.exp(s - m)
        denom = jnp.sum(p, axis=-1, keepdims=True) + jnp.exp(sk - m)
        p = (p * (1.0 / denom)).astype(BF16)
        o = _dot(p, vw).astype(BF16)
        for g in range(GROUP):
            o_ref[j * BLOCK:(j + 1) * BLOCK, g * HEAD_DIM:(g + 1) * HEAD_DIM] = o[g * BLOCK:(g + 1) * BLOCK]


def _win_attn(qa, ka, va, sink):
    tq = WIN_BLOCKS_PER_STEP * BLOCK
    nq = SEQ // tq
    gw = GROUP * HEAD_DIM
    return pl.pallas_call(
        _win_attn_kernel,
        grid=(BATCH, HA_KV, nq),
        in_specs=[pl.BlockSpec(memory_space=pltpu.SMEM),
                  pl.BlockSpec((tq, gw), lambda b, h, n: (b * nq + n, h)),
                  pl.BlockSpec((SEQ, HEAD_DIM), lambda b, h, n: (b, h)),
                  pl.BlockSpec((SEQ, HEAD_DIM), lambda b, h, n: (b, h))],
        out_specs=pl.BlockSpec((tq, gw), lambda b, h, n: (b * nq + n, h)),
        out_shape=jax.ShapeDtypeStruct((TOKENS, COLS_QA), BF16),
        compiler_params=_params(("parallel", "parallel", "arbitrary")),
        name="win_attn",
    )(sink, qa, ka, va)


MLA_KEY_CHUNK = 512


def _mla_attn_kernel(qt_ref, k_ref, vt_ref, o_ref):
    c = (QK_NOPE + QK_ROPE) ** -0.5 * float(np.log2(np.e))
    qt = qt_ref[...]
    tq = qt.shape[1]
    m = jnp.full((1, tq), -1e30, F32)
    l = jnp.zeros((1, tq), F32)
    acc = jnp.zeros((V_DIM, tq), F32)
    n_chunks = SEQ // MLA_KEY_CHUNK
    chunk = lambda ci: slice(ci * MLA_KEY_CHUNK, (ci + 1) * MLA_KEY_CHUNK)
    st_next = _dot(k_ref[chunk(0), :], qt)
    for ci in range(n_chunks):
        ks = chunk(ci)
        st = st_next
        if ci + 1 < n_chunks:
            st_next = _dot(k_ref[chunk(ci + 1), :], qt)
        m_new = jnp.maximum(m, jnp.max(st, axis=0, keepdims=True))
        alpha = jnp.exp2((m - m_new) * c)
        p = jnp.exp2((st - m_new) * c)
        l = l * alpha + jnp.sum(p, axis=0, keepdims=True)
        acc = acc * alpha + _dot(vt_ref[:, ks], p.astype(BF16))
        m = m_new
    o_ref[...] = (acc * (1.0 / l)).T.astype(BF16)


def _mla_attn(q_pad_t, k_pad, v_t):
    tq = 1024
    nq = SEQ // tq
    return pl.pallas_call(
        _mla_attn_kernel,
        grid=(BATCH, HB, nq),
        in_specs=[pl.BlockSpec((MLA_PAD, tq), lambda b, h, i: (h, b * nq + i)),
                  pl.BlockSpec((SEQ, MLA_PAD), lambda b, h, i: (b, h)),
                  pl.BlockSpec((V_DIM, SEQ), lambda b, h, i: (h, b))],
        out_specs=pl.BlockSpec((tq, V_DIM), lambda b, h, i: (b * nq + i, h)),
        out_shape=jax.ShapeDtypeStruct((TOKENS, HB * V_DIM), BF16),
        compiler_params=_params(("parallel", "parallel", "arbitrary")),
        name="mla_attn",
    )(q_pad_t, k_pad, v_t)


def _mix_out_kernel(x_ref, h_ref, gt_ref, oa_ref, ob_ref, woa_ref, wob_ref,
                    wga_ref, wgb_ref, bga_ref, bgb_ref, wout_ref, o_ref, acc_ref):
    n = pl.program_id(1)

    @pl.when(n == 0)
    def _():
        acc_ref[...] = jnp.zeros_like(acc_ref)

    h = h_ref[...]
    y_a = _dot(oa_ref[...], woa_ref[...])
    y_b = _dot(ob_ref[...], wob_ref[...])
    g_a = jax.nn.sigmoid(_dot(h, wga_ref[...]) + bga_ref[...])
    g_b = jax.nn.sigmoid(_dot(h, wgb_ref[...]) + bgb_ref[...])
    mix = (g_a * y_a + g_b * y_b).astype(BF16)
    acc_ref[...] += _dot(mix, wout_ref[...])

    @pl.when(n == pl.num_programs(1) - 1)
    def _():
        o_ref[...] = x_ref[...] + gt_ref[...] * acc_ref[...]


def _mix_out(x, h, mod, k_gate, o_a, o_b, w_oa, w_ob, w_gate, b_gate, w_out):
    tm, tc = 512, 512
    nc = D_MODEL // tc
    row = lambda i, n: (i, 0)
    col = lambda i, n: (0, n)
    col_hi = lambda i, n: (0, n + nc)
    return pl.pallas_call(
        _mix_out_kernel,
        grid=(TOKENS // tm, nc),
        in_specs=[pl.BlockSpec((tm, D_MODEL), row),
                  pl.BlockSpec((tm, D_MODEL), row),
                  _mod_spec(tm, k_gate),
                  pl.BlockSpec((tm, COLS_QA), row),
                  pl.BlockSpec((tm, HB * V_DIM), row),
                  pl.BlockSpec((COLS_QA, tc), col),
                  pl.BlockSpec((HB * V_DIM, tc), col),
                  pl.BlockSpec((D_MODEL, tc), col),
                  pl.BlockSpec((D_MODEL, tc), col_hi),
                  pl.BlockSpec((1, tc), col),
                  pl.BlockSpec((1, tc), col_hi),
                  pl.BlockSpec((tc, D_MODEL), lambda i, n: (n, 0))],
        out_specs=pl.BlockSpec((tm, D_MODEL), row),
        out_shape=jax.ShapeDtypeStruct((TOKENS, D_MODEL), F32),
        scratch_shapes=[pltpu.VMEM((tm, D_MODEL), F32)],
        compiler_params=_params(("parallel", "arbitrary")),
        name="mix_out",
    )(x, h, mod, o_a, o_b, w_oa, w_ob, w_gate, w_gate, b_gate, b_gate, w_out)


def _final_norm_kernel(x_ref, g_ref, o_ref):
    o_ref[...] = _rms(x_ref[...], g_ref[...])


def _final_norm(x, g):
    tm = 512
    return pl.pallas_call(
        _final_norm_kernel,
        grid=(TOKENS // tm,),
        in_specs=[pl.BlockSpec((tm, D_MODEL), lambda i: (i, 0)),
                  pl.BlockSpec((1, D_MODEL), lambda i: (0, 0))],
        out_specs=pl.BlockSpec((tm, D_MODEL), lambda i: (i, 0)),
        out_shape=jax.ShapeDtypeStruct((TOKENS, D_MODEL), F32),
        compiler_params=_params(("parallel",)),
        name="final_norm",
    )(x, g)


def _pad_heads(w, n_heads, width, pad_to):
    k = w.shape[0]
    w = w.reshape(k, n_heads, width)
    w = jnp.pad(w, ((0, 0), (0, 0), (0, pad_to - width)))
    return w.reshape(k, n_heads * pad_to)


def _layer_weights(l, w_in, w_uq, w_ukv):
    w = w_in[l]
    kr_pad = jnp.pad(w[:, OFF_KR:OFF_GATE], ((0, 0), (0, LANES - QK_ROPE)))
    w_x = jnp.concatenate([w[:, :OFF_KR], kr_pad], axis=1).astype(BF16)
    w_gate = w[:, OFF_GATE:].astype(BF16)
    w_uq_pad = _pad_heads(w_uq[l], HB, QK_NOPE + QK_ROPE, MLA_PAD).astype(BF16)
    ukv = w_ukv[l].reshape(KV_RANK, HB, QK_NOPE + V_DIM)
    w_ukv_perm = jnp.concatenate([ukv[:, :, :QK_NOPE].reshape(KV_RANK, HB * QK_NOPE),
                                  ukv[:, :, QK_NOPE:].reshape(KV_RANK, HB * V_DIM)], axis=1).astype(BF16)
    return w_x, w_gate, w_uq_pad, w_ukv_perm


def kernel(x, c, positions, norm_g, w_ada, b_ada, w_ffn1_gu, w_ffn1_d, w_ffn2_gu, w_ffn2_d, w_in, b_gate, sink,
           g_cq, g_ckv, w_uq, w_ukv, w_oa, w_ob, w_out, g_final):
    tables = _rope_tables(positions)
    mods = _ada_mod(c, w_ada, b_ada)
    xt = x.reshape(TOKENS, D_MODEL)
    for l in range(DEPTH):
        mod = mods[l].reshape(BATCH, 1, N_MOD * D_MODEL)
        g = norm_g[l]
        w_x, w_gate, w_uq_pad, w_ukv_perm = _layer_weights(l, w_in, w_uq, w_ukv)

        act = _ffn_up(xt, g[0:1], mod, 0, w_ffn1_gu[l].astype(BF16))
        xt = _ffn_down(act, w_ffn1_d[l].astype(BF16), xt, mod, 2)

        h, qa, ka, va, q_pad_t, k_pad, v_t = _in_proj(xt, g[1:2], mod, 3, tables, w_x, g_cq[l][None, :],
                                                      g_ckv[l][None, :], w_uq_pad, w_ukv_perm)
        o_a = _win_attn(qa, ka, va, sink[l])
        o_b = _mla_attn(q_pad_t, k_pad, v_t)
        xt = _mix_out(xt, h, mod, 5, o_a, o_b, w_oa[l].astype(BF16), w_ob[l].astype(BF16), w_gate,
                      b_gate[l][None, :], w_out[l].astype(BF16))

        act = _ffn_up(xt, g[2:3], mod, 6, w_ffn2_gu[l].astype(BF16))
        xt = _ffn_down(act, w_ffn2_d[l].astype(BF16), xt, mod, 8)
    out = _final_norm(xt, g_final[None, :])
    return out.reshape(BATCH, SEQ, D_MODEL)
```

```python
import functools

import jax
import jax.numpy as jnp
import numpy as np
from jax import lax
from jax.experimental import pallas as pl
from jax.experimental.pallas import tpu as pltpu

D_MODEL = 2048
BATCH = 8
SEQ = 4096
DEPTH = 2
HA_Q = 8
HA_KV = 2
GROUP = HA_Q // HA_KV
HEAD_DIM = 128
WINDOW = 128
BLOCK = 128
HB = 8
QK_NOPE = 128
QK_ROPE = 64
V_DIM = 128
Q_RANK = 512
KV_RANK = 512
D_FF = 5632
ROPE_THETA = 10000.0
EPS = 1e-6
N_MOD = 9
TOKENS = BATCH * SEQ

COLS_QA = HA_Q * HEAD_DIM
COLS_KA = HA_KV * HEAD_DIM
COLS_VA = HA_KV * HEAD_DIM
OFF_CQ = COLS_QA + COLS_KA + COLS_VA
OFF_CKV = OFF_CQ + Q_RANK
OFF_KR = OFF_CKV + KV_RANK
OFF_GATE = OFF_KR + QK_ROPE

LANES = 128
MLA_PAD = 256
N_ROPE_TABLES = 5
VMEM_LIMIT = 56 * 1024 * 1024

BF16 = jnp.bfloat16
F32 = jnp.float32


def _params(sem):
    return pltpu.CompilerParams(dimension_semantics=sem, vmem_limit_bytes=VMEM_LIMIT)


def _dot(a, b):
    return jnp.dot(a, b, preferred_element_type=F32)


def _dot_nt(a, b):
    return lax.dot_general(a, b, (((1,), (1,)), ((), ())), preferred_element_type=F32)


def _norm_mod(x, g, shift, scale):
    y = x * lax.rsqrt(jnp.mean(x * x, axis=-1, keepdims=True) + EPS)
    return (y * g) * (1.0 + scale) + shift


def _rope_table_kernel(pos_ref, freq_ref, o_ref):
    ang = pos_ref[...].astype(F32) * freq_ref[...]
    cs = jnp.cos(ang)
    sn = jnp.sin(ang)
    lane = lax.broadcasted_iota(jnp.int32, cs.shape, 1)
    cs64 = pltpu.roll(cs, 64, 1)
    sn64 = pltpu.roll(sn, 64, 1)
    cs96 = pltpu.roll(cs, 96, 1)
    sn96 = pltpu.roll(sn, 96, 1)
    zero = jnp.zeros_like(cs)
    o_ref[:, 0 * LANES:1 * LANES] = jnp.where(lane < 64, cs, cs64)
    o_ref[:, 1 * LANES:2 * LANES] = jnp.where(lane < 64, -sn, sn64)
    o_ref[:, 2 * LANES:3 * LANES] = jnp.where(lane < 32, cs64, jnp.where(lane < 64, cs96, zero))
    o_ref[:, 3 * LANES:4 * LANES] = jnp.where(lane < 32, -sn64, zero)
    o_ref[:, 4 * LANES:5 * LANES] = jnp.where((lane >= 32) & (lane < 64), sn96, zero)


def _rope_tables(positions):
    tm = 1024
    fa = ROPE_THETA ** (-jnp.arange(0, HEAD_DIM, 2, dtype=F32) / HEAD_DIM)
    fb = ROPE_THETA ** (-jnp.arange(0, QK_ROPE, 2, dtype=F32) / QK_ROPE)
    freq = jnp.concatenate([fa, fb, jnp.zeros((32,), F32)])[None, :]
    pos = positions.reshape(TOKENS, 1)
    return pl.pallas_call(
        _rope_table_kernel,
        grid=(TOKENS // tm,),
        in_specs=[pl.BlockSpec((tm, 1), lambda i: (i, 0)),
                  pl.BlockSpec((1, LANES), lambda i: (0, 0))],
        out_specs=pl.BlockSpec((tm, N_ROPE_TABLES * LANES), lambda i: (i, 0)),
        out_shape=jax.ShapeDtypeStruct((TOKENS, N_ROPE_TABLES * LANES), F32),
        compiler_params=_params(("parallel",)),
        name="rope_tables",
    )(pos, freq)


def _ada_kernel(c_ref, w_ref, b_ref, o_ref):
    c = c_ref[...]
    c_act = (c * jax.nn.sigmoid(c)).astype(BF16)
    o_ref[...] = _dot(c_act, w_ref[...].astype(BF16)) + b_ref[...]


def _ada_mod(c, w_ada, b_ada):
    tn = 1024
    n = N_MOD * D_MODEL
    return pl.pallas_call(
        _ada_kernel,
        grid=(DEPTH, n // tn),
        in_specs=[pl.BlockSpec((BATCH, D_MODEL), lambda l, j: (0, 0)),
                  pl.BlockSpec((None, D_MODEL, tn), lambda l, j: (l, 0, j)),
                  pl.BlockSpec((None, 1, tn), lambda l, j: (l, 0, j))],
        out_specs=pl.BlockSpec((None, BATCH, tn), lambda l, j: (l, 0, j)),
        out_shape=jax.ShapeDtypeStruct((DEPTH, BATCH, n), F32),
        compiler_params=_params(("parallel", "parallel")),
        name="ada_mod",
    )(c, w_ada, b_ada.reshape(DEPTH, 1, n))


def _mod_spec(tm, k):
    return pl.BlockSpec((None, 1, D_MODEL), lambda i, *_: (i * tm // SEQ, 0, k))


FFN_ROW_CHUNK = 256


def _ffn_up_kernel(x_ref, g_ref, sh_ref, sc_ref, wg_ref, wu_ref, o_ref, h_ref):
    def swiglu(h):
        g = _dot(h, wg_ref[...])
        u = _dot(h, wu_ref[...])
        return (g * jax.nn.sigmoid(g) * u).astype(BF16)

    @pl.when(pl.program_id(1) == 0)
    def _():
        for r in range(0, x_ref.shape[0], FFN_ROW_CHUNK):
            rows = slice(r, r + FFN_ROW_CHUNK)
            h = _norm_mod(x_ref[rows, :], g_ref[...], sh_ref[...], sc_ref[...]).astype(BF16)
            h_ref[rows, :] = h
            o_ref[rows, :] = swiglu(h)

    @pl.when(pl.program_id(1) != 0)
    def _():
        o_ref[...] = swiglu(h_ref[...])


def _ffn_up(x, g, mod, k_shift, w_gu):
    tm, tn = 1024, 512
    nj = D_FF // tn
    return pl.pallas_call(
        _ffn_up_kernel,
        grid=(TOKENS // tm, nj),
        in_specs=[pl.BlockSpec((tm, D_MODEL), lambda i, j: (i, 0)),
                  pl.BlockSpec((1, D_MODEL), lambda i, j: (0, 0)),
                  _mod_spec(tm, k_shift), _mod_spec(tm, k_shift + 1),
                  pl.BlockSpec((D_MODEL, tn), lambda i, j: (0, j)),
                  pl.BlockSpec((D_MODEL, tn), lambda i, j: (0, j + nj))],
        out_specs=pl.BlockSpec((tm, tn), lambda i, j: (i, j)),
        out_shape=jax.ShapeDtypeStruct((TOKENS, D_FF), BF16),
        scratch_shapes=[pltpu.VMEM((tm, D_MODEL), BF16)],
        compiler_params=_params(("parallel", "arbitrary")),
        name="ffn_up",
    )(x, g, mod, mod, w_gu, w_gu)


FFN_DOWN_COL_CHUNK = 512


def _ffn_down_kernel(a_ref, w_ref, x_ref, gt_ref, g_ref, sh_ref, sc_ref, *out_refs, tail):
    x_out = out_refs[0]
    a = a_ref[...]
    for c in range(0, D_MODEL, FFN_DOWN_COL_CHUNK):
        cols = slice(c, c + FFN_DOWN_COL_CHUNK)
        x_out[:, cols] = x_ref[:, cols] + (0.5 * gt_ref[:, cols]) * _dot(a, w_ref[:, cols])
    if tail == "final_norm":
        x_out[...] = _rms(x_out[...], g_ref[...])
    elif tail == "mixer_norm":
        out_refs[1][...] = _norm_mod(x_out[...], g_ref[...], sh_ref[...], sc_ref[...]).astype(BF16)


def _ffn_down(act, w_d, x, mod, k_gate, g_tail, k_shift_tail, tail):
    tm = 256
    row = lambda i: (i, 0)
    x_spec = pl.BlockSpec((tm, D_MODEL), row)
    out_specs = [x_spec]
    out_shape = [jax.ShapeDtypeStruct((TOKENS, D_MODEL), F32)]
    if tail == "mixer_norm":
        out_specs.append(pl.BlockSpec((tm, D_MODEL), row))
        out_shape.append(jax.ShapeDtypeStruct((TOKENS, D_MODEL), BF16))
    return pl.pallas_call(
        functools.partial(_ffn_down_kernel, tail=tail),
        grid=(TOKENS // tm,),
        in_specs=[pl.BlockSpec((tm, D_FF), row),
                  pl.BlockSpec((D_FF, D_MODEL), lambda i: (0, 0), pipeline_mode=pl.Buffered(1)),
                  x_spec,
                  _mod_spec(tm, k_gate),
                  pl.BlockSpec((1, D_MODEL), lambda i: (0, 0)),
                  _mod_spec(tm, k_shift_tail), _mod_spec(tm, k_shift_tail + 1)],
        out_specs=out_specs,
        out_shape=out_shape,
        compiler_params=_params(("parallel",)),
        name="ffn_down",
    )(act, w_d, x, mod, g_tail, mod, mod)


def _rms(x, g):
    return (x * lax.rsqrt(jnp.mean(x * x, axis=-1, keepdims=True) + EPS)) * g


def _in_proj_kernel(h_ref, tab_ref, wx_ref, gcq_ref, gckv_ref, wuq_ref, wukv_ref,
                    qa_ref, ka_ref, va_ref, qp_ref, kp_ref, vb_ref):
    h = h_ref[...]
    cos_a = tab_ref[:, 0 * LANES:1 * LANES]
    sin_a = tab_ref[:, 1 * LANES:2 * LANES]
    cos_b = tab_ref[:, 2 * LANES:3 * LANES]
    sin_lo = tab_ref[:, 3 * LANES:4 * LANES]
    sin_hi = tab_ref[:, 4 * LANES:5 * LANES]

    def rope_a(t):
        return t * cos_a + pltpu.roll(t, 64, 1) * sin_a

    def rope_b(t):
        return t * cos_b + pltpu.roll(t, 96, 1) * sin_lo + pltpu.roll(t, 32, 1) * sin_hi

    proj = _dot(h, wx_ref[...])

    for hd in range(HA_Q):
        sl = slice(hd * HEAD_DIM, (hd + 1) * HEAD_DIM)
        qa_ref[:, sl] = rope_a(proj[:, sl]).astype(BF16)
    for hd in range(HA_KV):
        sl = slice(hd * HEAD_DIM, (hd + 1) * HEAD_DIM)
        ka_ref[:, sl] = rope_a(proj[:, COLS_QA + hd * HEAD_DIM:COLS_QA + (hd + 1) * HEAD_DIM]).astype(BF16)
    va_ref[...] = proj[:, COLS_QA + COLS_KA:OFF_CQ].astype(BF16)

    cq = _rms(proj[:, OFF_CQ:OFF_CKV], gcq_ref[...]).astype(BF16)
    q_all = _dot(cq, wuq_ref[...])
    for hd in range(HB):
        base = hd * MLA_PAD
        qp_ref[base:base + QK_NOPE, :] = q_all[:, base:base + QK_NOPE].T.astype(BF16)
        qp_ref[base + QK_NOPE:base + MLA_PAD, :] = rope_b(q_all[:, base + QK_NOPE:base + MLA_PAD]).T.astype(BF16)

    ckv = _rms(proj[:, OFF_CKV:OFF_KR], gckv_ref[...]).astype(BF16)
    kv_all = _dot(ckv, wukv_ref[...])
    kr = rope_b(proj[:, OFF_KR:OFF_KR + LANES]).astype(BF16)
    for hd in range(HB):
        base = hd * MLA_PAD
        kp_ref[:, base:base + QK_NOPE] = kv_all[:, hd * QK_NOPE:(hd + 1) * QK_NOPE].astype(BF16)
        kp_ref[:, base + QK_NOPE:base + MLA_PAD] = kr
        vsl = slice(HB * QK_NOPE + hd * V_DIM, HB * QK_NOPE + (hd + 1) * V_DIM)
        vb_ref[hd * V_DIM:(hd + 1) * V_DIM, :] = kv_all[:, vsl].T.astype(BF16)


def _in_proj(h, tables, w_x, g_cq, g_ckv, w_uq, w_ukv):
    tm = 256
    const = lambda i: (0, 0)
    row = lambda i: (i, 0)
    nx = w_x.shape[1]
    col = lambda i: (0, i)
    resident = lambda shape: pl.BlockSpec(shape, const, pipeline_mode=pl.Buffered(1))
    outs = [(COLS_QA, True), (COLS_KA, True), (COLS_VA, True), (HB * MLA_PAD, False),
            (HB * MLA_PAD, True), (HB * V_DIM, False)]
    out_specs = [pl.BlockSpec((tm, n), row) if tok else pl.BlockSpec((n, tm), col) for n, tok in outs]
    out_shape = [jax.ShapeDtypeStruct((TOKENS, n) if tok else (n, TOKENS), BF16) for n, tok in outs]
    return pl.pallas_call(
        _in_proj_kernel,
        grid=(TOKENS // tm,),
        in_specs=[pl.BlockSpec((tm, D_MODEL), row),
                  pl.BlockSpec((tm, N_ROPE_TABLES * LANES), row),
                  resident((D_MODEL, nx)),
                  pl.BlockSpec((1, Q_RANK), const),
                  pl.BlockSpec((1, KV_RANK), const),
                  resident((Q_RANK, HB * MLA_PAD)),
                  resident((KV_RANK, HB * (QK_NOPE + V_DIM)))],
        out_specs=out_specs,
        out_shape=out_shape,
        compiler_params=_params(("parallel",)),
        name="in_proj",
    )(h, tables, w_x, g_cq, g_ckv, w_uq, w_ukv)


WIN_BLOCKS_PER_STEP = 4


def _win_attn_kernel(sink_ref, q_ref, k_ref, v_ref, o_ref):
    hk = pl.program_id(1)
    nb = SEQ // BLOCK
    rows = GROUP * BLOCK
    row = lax.broadcasted_iota(jnp.int32, (rows, 1), 0)
    sk = jnp.zeros((rows, 1), F32)
    for g in range(GROUP):
        sk = jnp.where(row // BLOCK == g, sink_ref[hk * GROUP + g], sk)
    rel = (lax.broadcasted_iota(jnp.int32, (rows, 3 * BLOCK), 1)
           - lax.broadcasted_iota(jnp.int32, (rows, 3 * BLOCK), 0) % BLOCK)
    for j in range(WIN_BLOCKS_PER_STEP):
        n = pl.program_id(2) * WIN_BLOCKS_PER_STEP + j
        start = pl.multiple_of(jnp.clip(n - 1, 0, nb - 3) * BLOCK, BLOCK)
        kw = k_ref[pl.ds(start, 3 * BLOCK), :]
        vw = v_ref[pl.ds(start, 3 * BLOCK), :]
        mask = jnp.abs(rel + (start - n * BLOCK)) <= WINDOW
        q = jnp.concatenate([q_ref[j * BLOCK:(j + 1) * BLOCK, g * HEAD_DIM:(g + 1) * HEAD_DIM]
                             for g in range(GROUP)], axis=0)
        s = _dot_nt(q, kw) * (HEAD_DIM ** -0.5)
        s = jnp.where(mask, s, -1e30)
        m = jnp.maximum(jnp.max(s, axis=-1, keepdims=True), sk)
        p = jnp.exp(s - m)
        denom = jnp.sum(p, axis=-1, keepdims=True) + jnp.exp(sk - m)
        p = (p * (1.0 / denom)).astype(BF16)
        o = _dot(p, vw).astype(BF16)
        for g in range(GROUP):
            o_ref[j * BLOCK:(j + 1) * BLOCK, g * HEAD_DIM:(g + 1) * HEAD_DIM] = o[g * BLOCK:(g + 1) * BLOCK]


def _win_attn(qa, ka, va, sink):
    tq = WIN_BLOCKS_PER_STEP * BLOCK
    nq = SEQ // tq
    gw = GROUP * HEAD_DIM
    return pl.pallas_call(
        _win_attn_kernel,
        grid=(BATCH, HA_KV, nq),
        in_specs=[pl.BlockSpec(memory_space=pltpu.SMEM),
                  pl.BlockSpec((tq, gw), lambda b, h, n: (b * nq + n, h)),
                  pl.BlockSpec((SEQ, HEAD_DIM), lambda b, h, n: (b, h)),
                  pl.BlockSpec((SEQ, HEAD_DIM), lambda b, h, n: (b, h))],
        out_specs=pl.BlockSpec((tq, gw), lambda b, h, n: (b * nq + n, h)),
        out_shape=jax.ShapeDtypeStruct((TOKENS, COLS_QA), BF16),
        compiler_params=_params(("parallel", "parallel", "arbitrary")),
        name="win_attn",
    )(sink, qa, ka, va)


MLA_KEY_CHUNK = 512
MLA_ONES_ROWS = 16


def _mla_attn_kernel(qt_ref, k_ref, vt_ref, o_ref):
    c = (QK_NOPE + QK_ROPE) ** -0.5 * float(np.log2(np.e))
    qt = qt_ref[...]
    tq = qt.shape[1]
    m = jnp.full((1, tq), -1e30, F32)
    acc = jnp.zeros((V_DIM + MLA_ONES_ROWS, tq), F32)
    ones = jnp.ones((MLA_ONES_ROWS, MLA_KEY_CHUNK), BF16)
    n_chunks = SEQ // MLA_KEY_CHUNK
    chunk = lambda ci: slice(ci * MLA_KEY_CHUNK, (ci + 1) * MLA_KEY_CHUNK)
    st_next = _dot(k_ref[chunk(0), :], qt)
    for ci in range(n_chunks):
        ks = chunk(ci)
        st = st_next
        if ci + 1 < n_chunks:
            st_next = _dot(k_ref[chunk(ci + 1), :], qt)
        m_new = jnp.maximum(m, jnp.max(st, axis=0, keepdims=True))
        alpha = jnp.exp2((m - m_new) * c)
        p = jnp.exp2((st - m_new) * c).astype(BF16)
        vt1 = jnp.concatenate([vt_ref[:, ks], ones], axis=0)
        acc = acc * alpha + _dot(vt1, p)
        m = m_new
    o_ref[...] = (acc[:V_DIM] * (1.0 / acc[V_DIM:V_DIM + 1])).T.astype(BF16)


def _mla_attn(q_pad_t, k_pad, v_t):
    tq = 1024
    nq = SEQ // tq
    return pl.pallas_call(
        _mla_attn_kernel,
        grid=(BATCH, HB, nq),
        in_specs=[pl.BlockSpec((MLA_PAD, tq), lambda b, h, i: (h, b * nq + i)),
                  pl.BlockSpec((SEQ, MLA_PAD), lambda b, h, i: (b, h)),
                  pl.BlockSpec((V_DIM, SEQ), lambda b, h, i: (h, b))],
        out_specs=pl.BlockSpec((tq, V_DIM), lambda b, h, i: (b * nq + i, h)),
        out_shape=jax.ShapeDtypeStruct((TOKENS, HB * V_DIM), BF16),
        compiler_params=_params(("parallel", "parallel", "arbitrary")),
        name="mla_attn",
    )(q_pad_t, k_pad, v_t)


def _mix_out_kernel(x_ref, h_ref, gt_ref, oa_ref, ob_ref, woa_ref, wob_ref,
                    wga_ref, wgb_ref, bga_ref, bgb_ref, wout_ref, o_ref, acc_ref):
    n = pl.program_id(1)

    @pl.when(n == 0)
    def _():
        acc_ref[...] = jnp.zeros_like(acc_ref)

    h = h_ref[...]
    y_a = _dot(oa_ref[...], woa_ref[...])
    y_b = _dot(ob_ref[...], wob_ref[...])
    g_a = jax.nn.sigmoid(_dot(h, wga_ref[...]) + bga_ref[...])
    g_b = jax.nn.sigmoid(_dot(h, wgb_ref[...]) + bgb_ref[...])
    mix = (g_a * y_a + g_b * y_b).astype(BF16)
    acc_ref[...] += _dot(mix, wout_ref[...])

    @pl.when(n == pl.num_programs(1) - 1)
    def _():
        o_ref[...] = x_ref[...] + gt_ref[...] * acc_ref[...]


def _mix_out(x, h, mod, k_gate, o_a, o_b, w_oa, w_ob, w_gate, b_gate, w_out):
    tm, tc = 512, 512
    nc = D_MODEL // tc
    row = lambda i, n: (i, 0)
    col = lambda i, n: (0, n)
    col_hi = lambda i, n: (0, n + nc)
    return pl.pallas_call(
        _mix_out_kernel,
        grid=(TOKENS // tm, nc),
        in_specs=[pl.BlockSpec((tm, D_MODEL), row),
                  pl.BlockSpec((tm, D_MODEL), row),
                  _mod_spec(tm, k_gate),
                  pl.BlockSpec((tm, COLS_QA), row),
                  pl.BlockSpec((tm, HB * V_DIM), row),
                  pl.BlockSpec((COLS_QA, tc), col),
                  pl.BlockSpec((HB * V_DIM, tc), col),
                  pl.BlockSpec((D_MODEL, tc), col),
                  pl.BlockSpec((D_MODEL, tc), col_hi),
                  pl.BlockSpec((1, tc), col),
                  pl.BlockSpec((1, tc), col_hi),
                  pl.BlockSpec((tc, D_MODEL), lambda i, n: (n, 0))],
        out_specs=pl.BlockSpec((tm, D_MODEL), row),
        out_shape=jax.ShapeDtypeStruct((TOKENS, D_MODEL), F32),
        scratch_shapes=[pltpu.VMEM((tm, D_MODEL), F32)],
        compiler_params=_params(("parallel", "arbitrary")),
        name="mix_out",
    )(x, h, mod, o_a, o_b, w_oa, w_ob, w_gate, w_gate, b_gate, b_gate, w_out)


def _pad_heads(w, n_heads, width, pad_to):
    k = w.shape[0]
    w = w.reshape(k, n_heads, width)
    w = jnp.pad(w, ((0, 0), (0, 0), (0, pad_to - width)))
    return w.reshape(k, n_heads * pad_to)


def _layer_weights(l, w_in, w_uq, w_ukv):
    w = w_in[l]
    kr_pad = jnp.pad(w[:, OFF_KR:OFF_GATE], ((0, 0), (0, LANES - QK_ROPE)))
    w_x = jnp.concatenate([w[:, :OFF_KR], kr_pad], axis=1).astype(BF16)
    w_gate = w[:, OFF_GATE:].astype(BF16)
    w_uq_pad = _pad_heads(w_uq[l], HB, QK_NOPE + QK_ROPE, MLA_PAD).astype(BF16)
    ukv = w_ukv[l].reshape(KV_RANK, HB, QK_NOPE + V_DIM)
    w_ukv_perm = jnp.concatenate([ukv[:, :, :QK_NOPE].reshape(KV_RANK, HB * QK_NOPE),
                                  ukv[:, :, QK_NOPE:].reshape(KV_RANK, HB * V_DIM)], axis=1).astype(BF16)
    return w_x, w_gate, w_uq_pad, w_ukv_perm


def kernel(x, c, positions, norm_g, w_ada, b_ada, w_ffn1_gu, w_ffn1_d, w_ffn2_gu, w_ffn2_d, w_in, b_gate, sink,
           g_cq, g_ckv, w_uq, w_ukv, w_oa, w_ob, w_out, g_final):
    tables = _rope_tables(positions)
    mods = _ada_mod(c, w_ada, b_ada)
    xt = x.reshape(TOKENS, D_MODEL)
    for l in range(DEPTH):
        mod = mods[l].reshape(BATCH, 1, N_MOD * D_MODEL)
        g = norm_g[l]
        w_x, w_gate, w_uq_pad, w_ukv_perm = _layer_weights(l, w_in, w_uq, w_ukv)

        act = _ffn_up(xt, g[0:1], mod, 0, w_ffn1_gu[l].astype(BF16))
        xt, h = _ffn_down(act, w_ffn1_d[l].astype(BF16), xt, mod, 2, g[1:2], 3, "mixer_norm")

        qa, ka, va, q_pad_t, k_pad, v_t = _in_proj(h, tables, w_x, g_cq[l][None, :], g_ckv[l][None, :],
                                                   w_uq_pad, w_ukv_perm)
        o_a = _win_attn(qa, ka, va, sink[l])
        o_b = _mla_attn(q_pad_t, k_pad, v_t)
        xt = _mix_out(xt, h, mod, 5, o_a, o_b, w_oa[l].astype(BF16), w_ob[l].astype(BF16), w_gate,
                      b_gate[l][None, :], w_out[l].astype(BF16))

        act = _ffn_up(xt, g[2:3], mod, 6, w_ffn2_gu[l].astype(BF16))
        last = l + 1 == DEPTH
        xt, = _ffn_down(act, w_ffn2_d[l].astype(BF16), xt, mod, 8, g_final[None, :], 0,
                        "final_norm" if last else "plain")
    return xt.reshape(BATCH, SEQ, D_MODEL)
```

```python
import functools

import jax
import jax.numpy as jnp
import numpy as np
from jax import lax
from jax.experimental import pallas as pl
from jax.experimental.pallas import tpu as pltpu

D_MODEL = 2048
BATCH = 8
SEQ = 4096
DEPTH = 2
HA_Q = 8
HA_KV = 2
GROUP = HA_Q // HA_KV
HEAD_DIM = 128
WINDOW = 128
BLOCK = 128
HB = 8
QK_NOPE = 128
QK_ROPE = 64
V_DIM = 128
Q_RANK = 512
KV_RANK = 512
D_FF = 5632
ROPE_THETA = 10000.0
EPS = 1e-6
N_MOD = 9
TOKENS = BATCH * SEQ

COLS_QA = HA_Q * HEAD_DIM
COLS_KA = HA_KV * HEAD_DIM
COLS_VA = HA_KV * HEAD_DIM
OFF_CQ = COLS_QA + COLS_KA + COLS_VA
OFF_CKV = OFF_CQ + Q_RANK
OFF_KR = OFF_CKV + KV_RANK
OFF_GATE = OFF_KR + QK_ROPE

LANES = 128
MLA_PAD = 256
N_ROPE_TABLES = 5
VMEM_LIMIT = 56 * 1024 * 1024

BF16 = jnp.bfloat16
F32 = jnp.float32


def _params(sem):
    return pltpu.CompilerParams(dimension_semantics=sem, vmem_limit_bytes=VMEM_LIMIT)


def _dot(a, b):
    return jnp.dot(a, b, preferred_element_type=F32)


def _dot_nt(a, b):
    return lax.dot_general(a, b, (((1,), (1,)), ((), ())), preferred_element_type=F32)


def _norm_mod(x, g, shift, scale):
    y = x * lax.rsqrt(jnp.mean(x * x, axis=-1, keepdims=True) + EPS)
    return (y * g) * (1.0 + scale) + shift


def _rope_table_kernel(pos_ref, freq_ref, o_ref):
    ang = pos_ref[...].astype(F32) * freq_ref[...]
    cs = jnp.cos(ang)
    sn = jnp.sin(ang)
    lane = lax.broadcasted_iota(jnp.int32, cs.shape, 1)
    cs64 = pltpu.roll(cs, 64, 1)
    sn64 = pltpu.roll(sn, 64, 1)
    cs96 = pltpu.roll(cs, 96, 1)
    sn96 = pltpu.roll(sn, 96, 1)
    zero = jnp.zeros_like(cs)
    o_ref[:, 0 * LANES:1 * LANES] = jnp.where(lane < 64, cs, cs64)
    o_ref[:, 1 * LANES:2 * LANES] = jnp.where(lane < 64, -sn, sn64)
    o_ref[:, 2 * LANES:3 * LANES] = jnp.where(lane < 32, cs64, jnp.where(lane < 64, cs96, zero))
    o_ref[:, 3 * LANES:4 * LANES] = jnp.where(lane < 32, -sn64, zero)
    o_ref[:, 4 * LANES:5 * LANES] = jnp.where((lane >= 32) & (lane < 64), sn96, zero)


def _rope_tables(positions):
    tm = 1024
    fa = ROPE_THETA ** (-jnp.arange(0, HEAD_DIM, 2, dtype=F32) / HEAD_DIM)
    fb = ROPE_THETA ** (-jnp.arange(0, QK_ROPE, 2, dtype=F32) / QK_ROPE)
    freq = jnp.concatenate([fa, fb, jnp.zeros((32,), F32)])[None, :]
    pos = positions.reshape(TOKENS, 1)
    return pl.pallas_call(
        _rope_table_kernel,
        grid=(TOKENS // tm,),
        in_specs=[pl.BlockSpec((tm, 1), lambda i: (i, 0)),
                  pl.BlockSpec((1, LANES), lambda i: (0, 0))],
        out_specs=pl.BlockSpec((tm, N_ROPE_TABLES * LANES), lambda i: (i, 0)),
        out_shape=jax.ShapeDtypeStruct((TOKENS, N_ROPE_TABLES * LANES), F32),
        compiler_params=_params(("parallel",)),
        name="rope_tables",
    )(pos, freq)


def _ada_kernel(c_ref, w_ref, b_ref, o_ref):
    c = c_ref[...]
    c_act = (c * jax.nn.sigmoid(c)).astype(BF16)
    o_ref[...] = _dot(c_act, w_ref[...].astype(BF16)) + b_ref[...]


def _ada_mod(c, w_ada, b_ada):
    tn = 1024
    n = N_MOD * D_MODEL
    return pl.pallas_call(
        _ada_kernel,
        grid=(DEPTH, n // tn),
        in_specs=[pl.BlockSpec((BATCH, D_MODEL), lambda l, j: (0, 0)),
                  pl.BlockSpec((None, D_MODEL, tn), lambda l, j: (l, 0, j)),
                  pl.BlockSpec((None, 1, tn), lambda l, j: (l, 0, j))],
        out_specs=pl.BlockSpec((None, BATCH, tn), lambda l, j: (l, 0, j)),
        out_shape=jax.ShapeDtypeStruct((DEPTH, BATCH, n), F32),
        compiler_params=_params(("parallel", "parallel")),
        name="ada_mod",
    )(c, w_ada, b_ada.reshape(DEPTH, 1, n))


def _mod_spec(tm, k):
    return pl.BlockSpec((None, 1, D_MODEL), lambda i, *_: (i * tm // SEQ, 0, k))


FFN_ROW_CHUNK = 256


def _ffn_up_kernel(x_ref, g_ref, sh_ref, sc_ref, wg_ref, wu_ref, o_ref, h_ref):
    def swiglu(h):
        g = _dot(h, wg_ref[...])
        u = _dot(h, wu_ref[...])
        return (g * jax.nn.sigmoid(g) * u).astype(BF16)

    @pl.when(pl.program_id(1) == 0)
    def _():
        for r in range(0, x_ref.shape[0], FFN_ROW_CHUNK):
            rows = slice(r, r + FFN_ROW_CHUNK)
            h = _norm_mod(x_ref[rows, :], g_ref[...], sh_ref[...], sc_ref[...]).astype(BF16)
            h_ref[rows, :] = h
            o_ref[rows, :] = swiglu(h)

    @pl.when(pl.program_id(1) != 0)
    def _():
        o_ref[...] = swiglu(h_ref[...])


def _ffn_up(x, g, mod, k_shift, w_gu):
    tm, tn = 1024, 512
    nj = D_FF // tn
    return pl.pallas_call(
        _ffn_up_kernel,
        grid=(TOKENS // tm, nj),
        in_specs=[pl.BlockSpec((tm, D_MODEL), lambda i, j: (i, 0)),
                  pl.BlockSpec((1, D_MODEL), lambda i, j: (0, 0)),
                  _mod_spec(tm, k_shift), _mod_spec(tm, k_shift + 1),
                  pl.BlockSpec((D_MODEL, tn), lambda i, j: (0, j)),
                  pl.BlockSpec((D_MODEL, tn), lambda i, j: (0, j + nj))],
        out_specs=pl.BlockSpec((tm, tn), lambda i, j: (i, j)),
        out_shape=jax.ShapeDtypeStruct((TOKENS, D_FF), BF16),
        scratch_shapes=[pltpu.VMEM((tm, D_MODEL), BF16)],
        compiler_params=_params(("parallel", "arbitrary")),
        name="ffn_up",
    )(x, g, mod, mod, w_gu, w_gu)


FFN_DOWN_COL_CHUNK = 512


def _ffn_down_kernel(a_ref, w_ref, x_ref, gt_ref, g_ref, sh_ref, sc_ref, *out_refs, tail):
    x_out = out_refs[0]
    a = a_ref[...]
    for c in range(0, D_MODEL, FFN_DOWN_COL_CHUNK):
        cols = slice(c, c + FFN_DOWN_COL_CHUNK)
        x_out[:, cols] = x_ref[:, cols] + (0.5 * gt_ref[:, cols]) * _dot(a, w_ref[:, cols])
    if tail == "final_norm":
        x_out[...] = _rms(x_out[...], g_ref[...])
    elif tail == "mixer_norm":
        out_refs[1][...] = _norm_mod(x_out[...], g_ref[...], sh_ref[...], sc_ref[...]).astype(BF16)


def _ffn_down(act, w_d, x, mod, k_gate, g_tail, k_shift_tail, tail):
    tm = 256
    row = lambda i: (i, 0)
    x_spec = pl.BlockSpec((tm, D_MODEL), row)
    out_specs = [x_spec]
    out_shape = [jax.ShapeDtypeStruct((TOKENS, D_MODEL), F32)]
    if tail == "mixer_norm":
        out_specs.append(pl.BlockSpec((tm, D_MODEL), row))
        out_shape.append(jax.ShapeDtypeStruct((TOKENS, D_MODEL), BF16))
    return pl.pallas_call(
        functools.partial(_ffn_down_kernel, tail=tail),
        grid=(TOKENS // tm,),
        in_specs=[pl.BlockSpec((tm, D_FF), row),
                  pl.BlockSpec((D_FF, D_MODEL), lambda i: (0, 0), pipeline_mode=pl.Buffered(1)),
                  x_spec,
                  _mod_spec(tm, k_gate),
                  pl.BlockSpec((1, D_MODEL), lambda i: (0, 0)),
                  _mod_spec(tm, k_shift_tail), _mod_spec(tm, k_shift_tail + 1)],
        out_specs=out_specs,
        out_shape=out_shape,
        compiler_params=_params(("parallel",)),
        name="ffn_down",
    )(act, w_d, x, mod, g_tail, mod, mod)


def _rms(x, g):
    return (x * lax.rsqrt(jnp.mean(x * x, axis=-1, keepdims=True) + EPS)) * g


def _in_proj_kernel(h_ref, tab_ref, wx_ref, gcq_ref, gckv_ref, wuq_ref, wukv_ref,
                    qa_ref, ka_ref, va_ref, qp_ref, kp_ref, vb_ref):
    h = h_ref[...]
    cos_a = tab_ref[:, 0 * LANES:1 * LANES]
    sin_a = tab_ref[:, 1 * LANES:2 * LANES]
    cos_b = tab_ref[:, 2 * LANES:3 * LANES]
    sin_lo = tab_ref[:, 3 * LANES:4 * LANES]
    sin_hi = tab_ref[:, 4 * LANES:5 * LANES]

    def rope_a(t):
        return t * cos_a + pltpu.roll(t, 64, 1) * sin_a

    def rope_b(t):
        return t * cos_b + pltpu.roll(t, 96, 1) * sin_lo + pltpu.roll(t, 32, 1) * sin_hi

    proj = _dot(h, wx_ref[...])

    for hd in range(HA_Q):
        sl = slice(hd * HEAD_DIM, (hd + 1) * HEAD_DIM)
        qa_ref[:, sl] = rope_a(proj[:, sl]).astype(BF16)
    for hd in range(HA_KV):
        sl = slice(hd * HEAD_DIM, (hd + 1) * HEAD_DIM)
        ka_ref[:, sl] = rope_a(proj[:, COLS_QA + hd * HEAD_DIM:COLS_QA + (hd + 1) * HEAD_DIM]).astype(BF16)
    va_ref[...] = proj[:, COLS_QA + COLS_KA:OFF_CQ].astype(BF16)

    cq = _rms(proj[:, OFF_CQ:OFF_CKV], gcq_ref[...]).astype(BF16)
    q_all = _dot(cq, wuq_ref[...])
    for hd in range(HB):
        base = hd * MLA_PAD
        qp_ref[base:base + QK_NOPE, :] = q_all[:, base:base + QK_NOPE].T.astype(BF16)
        qp_ref[base + QK_NOPE:base + MLA_PAD, :] = rope_b(q_all[:, base + QK_NOPE:base + MLA_PAD]).T.astype(BF16)

    ckv = _rms(proj[:, OFF_CKV:OFF_KR], gckv_ref[...]).astype(BF16)
    kv_all = _dot(ckv, wukv_ref[...])
    kr = rope_b(proj[:, OFF_KR:OFF_KR + LANES]).astype(BF16)
    for hd in range(HB):
        base = hd * MLA_PAD
        kp_ref[:, base:base + QK_NOPE] = kv_all[:, hd * QK_NOPE:(hd + 1) * QK_NOPE].astype(BF16)
        kp_ref[:, base + QK_NOPE:base + MLA_PAD] = kr
        vsl = slice(HB * QK_NOPE + hd * V_DIM, HB * QK_NOPE + (hd + 1) * V_DIM)
        vb_ref[hd * V_DIM:(hd + 1) * V_DIM, :] = kv_all[:, vsl].T.astype(BF16)


def _in_proj(h, tables, w_x, g_cq, g_ckv, w_uq, w_ukv):
    tm = 256
    const = lambda i: (0, 0)
    row = lambda i: (i, 0)
    nx = w_x.shape[1]
    col = lambda i: (0, i)
    resident = lambda shape: pl.BlockSpec(shape, const, pipeline_mode=pl.Buffered(1))
    outs = [(COLS_QA, True), (COLS_KA, True), (COLS_VA, True), (HB * MLA_PAD, False),
            (HB * MLA_PAD, True), (HB * V_DIM, False)]
    out_specs = [pl.BlockSpec((tm, n), row) if tok else pl.BlockSpec((n, tm), col) for n, tok in outs]
    out_shape = [jax.ShapeDtypeStruct((TOKENS, n) if tok else (n, TOKENS), BF16) for n, tok in outs]
    return pl.pallas_call(
        _in_proj_kernel,
        grid=(TOKENS // tm,),
        in_specs=[pl.BlockSpec((tm, D_MODEL), row),
                  pl.BlockSpec((tm, N_ROPE_TABLES * LANES), row),
                  resident((D_MODEL, nx)),
                  pl.BlockSpec((1, Q_RANK), const),
                  pl.BlockSpec((1, KV_RANK), const),
                  resident((Q_RANK, HB * MLA_PAD)),
                  resident((KV_RANK, HB * (QK_NOPE + V_DIM)))],
        out_specs=out_specs,
        out_shape=out_shape,
        compiler_params=_params(("parallel",)),
        name="in_proj",
    )(h, tables, w_x, g_cq, g_ckv, w_uq, w_ukv)


WIN_BLOCKS_PER_STEP = 4


WIN_KEYS = 3 * BLOCK
WIN_ROWS = GROUP * BLOCK
LOG2E = float(np.log2(np.e))


def _win_mask_bias():
    rel = jnp.arange(WIN_KEYS)[None, :] - (jnp.arange(WIN_ROWS) % BLOCK)[:, None]
    off = -BLOCK * jnp.arange(3)[:, None, None]
    return jnp.where(jnp.abs(rel[None] + off) <= WINDOW, 0.0, -1e30).astype(F32)


def _win_attn_kernel(sink_ref, bias_ref, q_ref, k_ref, v_ref, o_ref):
    hk = pl.program_id(1)
    nb = SEQ // BLOCK
    c = HEAD_DIM ** -0.5 * LOG2E
    row = lax.broadcasted_iota(jnp.int32, (WIN_ROWS, 1), 0)
    sk = jnp.zeros((WIN_ROWS, 1), F32)
    for g in range(GROUP):
        sk = jnp.where(row // BLOCK == g, sink_ref[hk * GROUP + g] * LOG2E, sk)
    for j in range(WIN_BLOCKS_PER_STEP):
        n = pl.program_id(2) * WIN_BLOCKS_PER_STEP + j
        first = jnp.clip(n - 1, 0, nb - 3)
        start = pl.multiple_of(first * BLOCK, BLOCK)
        kw = k_ref[pl.ds(start, WIN_KEYS), :]
        vw = v_ref[pl.ds(start, WIN_KEYS), :]
        q = jnp.concatenate([q_ref[j * BLOCK:(j + 1) * BLOCK, g * HEAD_DIM:(g + 1) * HEAD_DIM]
                             for g in range(GROUP)], axis=0)
        t = _dot_nt(q, kw) * c + bias_ref[n - first]
        m = jnp.maximum(jnp.max(t, axis=-1, keepdims=True), sk)
        p = jnp.exp2(t - m)
        denom = jnp.sum(p, axis=-1, keepdims=True) + jnp.exp2(sk - m)
        o = (_dot(p.astype(BF16), vw) * (1.0 / denom)).astype(BF16)
        for g in range(GROUP):
            o_ref[j * BLOCK:(j + 1) * BLOCK, g * HEAD_DIM:(g + 1) * HEAD_DIM] = o[g * BLOCK:(g + 1) * BLOCK]


def _win_attn(qa, ka, va, sink):
    tq = WIN_BLOCKS_PER_STEP * BLOCK
    nq = SEQ // tq
    gw = GROUP * HEAD_DIM
    return pl.pallas_call(
        _win_attn_kernel,
        grid=(BATCH, HA_KV, nq),
        in_specs=[pl.BlockSpec(memory_space=pltpu.SMEM),
                  pl.BlockSpec((3, WIN_ROWS, WIN_KEYS), lambda b, h, n: (0, 0, 0)),
                  pl.BlockSpec((tq, gw), lambda b, h, n: (b * nq + n, h)),
                  pl.BlockSpec((SEQ, HEAD_DIM), lambda b, h, n: (b, h)),
                  pl.BlockSpec((SEQ, HEAD_DIM), lambda b, h, n: (b, h))],
        out_specs=pl.BlockSpec((tq, gw), lambda b, h, n: (b * nq + n, h)),
        out_shape=jax.ShapeDtypeStruct((TOKENS, COLS_QA), BF16),
        compiler_params=_params(("parallel", "parallel", "arbitrary")),
        name="win_attn",
    )(sink, _win_mask_bias(), qa, ka, va)


MLA_KEY_CHUNK = 512
MLA_ONES_ROWS = 16


def _mla_attn_kernel(qt_ref, k_ref, vt_ref, o_ref):
    c = (QK_NOPE + QK_ROPE) ** -0.5 * LOG2E
    qt = qt_ref[...]
    tq = qt.shape[1]
    m = jnp.full((1, tq), -1e30, F32)
    acc = jnp.zeros((V_DIM + MLA_ONES_ROWS, tq), F32)
    ones = jnp.ones((MLA_ONES_ROWS, MLA_KEY_CHUNK), BF16)
    n_chunks = SEQ // MLA_KEY_CHUNK
    chunk = lambda ci: slice(ci * MLA_KEY_CHUNK, (ci + 1) * MLA_KEY_CHUNK)
    st_next = _dot(k_ref[chunk(0), :], qt)
    for ci in range(n_chunks):
        ks = chunk(ci)
        st = st_next
        if ci + 1 < n_chunks:
            st_next = _dot(k_ref[chunk(ci + 1), :], qt)
        m_new = jnp.maximum(m, jnp.max(st, axis=0, keepdims=True))
        alpha = jnp.exp2((m - m_new) * c)
        p = jnp.exp2((st - m_new) * c).astype(BF16)
        vt1 = jnp.concatenate([vt_ref[:, ks], ones], axis=0)
        acc = acc * alpha + _dot(vt1, p)
        m = m_new
    o_ref[...] = (acc[:V_DIM] * (1.0 / acc[V_DIM:V_DIM + 1])).T.astype(BF16)


def _mla_attn(q_pad_t, k_pad, v_t):
    tq = 1024
    nq = SEQ // tq
    return pl.pallas_call(
        _mla_attn_kernel,
        grid=(BATCH, HB, nq),
        in_specs=[pl.BlockSpec((MLA_PAD, tq), lambda b, h, i: (h, b * nq + i)),
                  pl.BlockSpec((SEQ, MLA_PAD), lambda b, h, i: (b, h)),
                  pl.BlockSpec((V_DIM, SEQ), lambda b, h, i: (h, b))],
        out_specs=pl.BlockSpec((tq, V_DIM), lambda b, h, i: (b * nq + i, h)),
        out_shape=jax.ShapeDtypeStruct((TOKENS, HB * V_DIM), BF16),
        compiler_params=_params(("parallel", "parallel", "arbitrary")),
        name="mla_attn",
    )(q_pad_t, k_pad, v_t)


def _mix_out_kernel(x_ref, h_ref, gt_ref, oa_ref, ob_ref, woa_ref, wob_ref,
                    wga_ref, wgb_ref, bga_ref, bgb_ref, wout_ref, o_ref, acc_ref):
    n = pl.program_id(1)
    h = h_ref[...]
    y_a = _dot(oa_ref[...], woa_ref[...])
    y_b = _dot(ob_ref[...], wob_ref[...])
    g_a = jax.nn.sigmoid(_dot(h, wga_ref[...]) + bga_ref[...])
    g_b = jax.nn.sigmoid(_dot(h, wgb_ref[...]) + bgb_ref[...])
    mix = (g_a * y_a + g_b * y_b).astype(BF16)

    @pl.when(n == 0)
    def _():
        acc_ref[...] = _dot(mix, wout_ref[...])

    @pl.when(n != 0)
    def _():
        acc_ref[...] += _dot(mix, wout_ref[...])

    @pl.when(n == pl.num_programs(1) - 1)
    def _():
        o_ref[...] = x_ref[...] + gt_ref[...] * acc_ref[...]


def _mix_out(x, h, mod, k_gate, o_a, o_b, w_oa, w_ob, w_gate, b_gate, w_out):
    tm, tc = 512, 512
    nc = D_MODEL // tc
    row = lambda i, n: (i, 0)
    col = lambda i, n: (0, n)
    col_hi = lambda i, n: (0, n + nc)
    return pl.pallas_call(
        _mix_out_kernel,
        grid=(TOKENS // tm, nc),
        in_specs=[pl.BlockSpec((tm, D_MODEL), row),
                  pl.BlockSpec((tm, D_MODEL), row),
                  _mod_spec(tm, k_gate),
                  pl.BlockSpec((tm, COLS_QA), row),
                  pl.BlockSpec((tm, HB * V_DIM), row),
                  pl.BlockSpec((COLS_QA, tc), col),
                  pl.BlockSpec((HB * V_DIM, tc), col),
                  pl.BlockSpec((D_MODEL, tc), col),
                  pl.BlockSpec((D_MODEL, tc), col_hi),
                  pl.BlockSpec((1, tc), col),
                  pl.BlockSpec((1, tc), col_hi),
                  pl.BlockSpec((tc, D_MODEL), lambda i, n: (n, 0))],
        out_specs=pl.BlockSpec((tm, D_MODEL), row),
        out_shape=jax.ShapeDtypeStruct((TOKENS, D_MODEL), F32),
        scratch_shapes=[pltpu.VMEM((tm, D_MODEL), F32)],
        compiler_params=_params(("parallel", "arbitrary")),
        name="mix_out",
    )(x, h, mod, o_a, o_b, w_oa, w_ob, w_gate, w_gate, b_gate, b_gate, w_out)


def _pad_heads(w, n_heads, width, pad_to):
    k = w.shape[0]
    w = w.reshape(k, n_heads, width)
    w = jnp.pad(w, ((0, 0), (0, 0), (0, pad_to - width)))
    return w.reshape(k, n_heads * pad_to)


def _layer_weights(l, w_in, w_uq, w_ukv):
    w = w_in[l]
    kr_pad = jnp.pad(w[:, OFF_KR:OFF_GATE], ((0, 0), (0, LANES - QK_ROPE)))
    w_x = jnp.concatenate([w[:, :OFF_KR], kr_pad], axis=1).astype(BF16)
    w_gate = w[:, OFF_GATE:].astype(BF16)
    w_uq_pad = _pad_heads(w_uq[l], HB, QK_NOPE + QK_ROPE, MLA_PAD).astype(BF16)
    ukv = w_ukv[l].reshape(KV_RANK, HB, QK_NOPE + V_DIM)
    w_ukv_perm = jnp.concatenate([ukv[:, :, :QK_NOPE].reshape(KV_RANK, HB * QK_NOPE),
                                  ukv[:, :, QK_NOPE:].reshape(KV_RANK, HB * V_DIM)], axis=1).astype(BF16)
    return w_x, w_gate, w_uq_pad, w_ukv_perm


def kernel(x, c, positions, norm_g, w_ada, b_ada, w_ffn1_gu, w_ffn1_d, w_ffn2_gu, w_ffn2_d, w_in, b_gate, sink,
           g_cq, g_ckv, w_uq, w_ukv, w_oa, w_ob, w_out, g_final):
    tables = _rope_tables(positions)
    mods = _ada_mod(c, w_ada, b_ada)
    xt = x.reshape(TOKENS, D_MODEL)
    for l in range(DEPTH):
        mod = mods[l].reshape(BATCH, 1, N_MOD * D_MODEL)
        g = norm_g[l]
        w_x, w_gate, w_uq_pad, w_ukv_perm = _layer_weights(l, w_in, w_uq, w_ukv)

        act = _ffn_up(xt, g[0:1], mod, 0, w_ffn1_gu[l].astype(BF16))
        xt, h = _ffn_down(act, w_ffn1_d[l].astype(BF16), xt, mod, 2, g[1:2], 3, "mixer_norm")

        qa, ka, va, q_pad_t, k_pad, v_t = _in_proj(h, tables, w_x, g_cq[l][None, :], g_ckv[l][None, :],
                                                   w_uq_pad, w_ukv_perm)
        o_a = _win_attn(qa, ka, va, sink[l])
        o_b = _mla_attn(q_pad_t, k_pad, v_t)
        xt = _mix_out(xt, h, mod, 5, o_a, o_b, w_oa[l].astype(BF16), w_ob[l].astype(BF16), w_gate,
                      b_gate[l][None, :], w_out[l].astype(BF16))

        act = _ffn_up(xt, g[2:3], mod, 6, w_ffn2_gu[l].astype(BF16))
        last = l + 1 == DEPTH
        xt, = _ffn_down(act, w_ffn2_d[l].astype(BF16), xt, mod, 8, g_final[None, :], 0,
                        "final_norm" if last else "plain")
    return xt.reshape(BATCH, SEQ, D_MODEL)
```

```python
import functools

import jax
import jax.numpy as jnp
import numpy as np
from jax import lax
from jax.experimental import pallas as pl
from jax.experimental.pallas import tpu as pltpu

D_MODEL = 2048
BATCH = 8
SEQ = 4096
DEPTH = 2
HA_Q = 8
HA_KV = 2
GROUP = HA_Q // HA_KV
HEAD_DIM = 128
WINDOW = 128
BLOCK = 128
HB = 8
QK_NOPE = 128
QK_ROPE = 64
V_DIM = 128
Q_RANK = 512
KV_RANK = 512
D_FF = 5632
ROPE_THETA = 10000.0
EPS = 1e-6
N_MOD = 9
TOKENS = BATCH * SEQ

COLS_QA = HA_Q * HEAD_DIM
COLS_KA = HA_KV * HEAD_DIM
COLS_VA = HA_KV * HEAD_DIM
OFF_CQ = COLS_QA + COLS_KA + COLS_VA
OFF_CKV = OFF_CQ + Q_RANK
OFF_KR = OFF_CKV + KV_RANK
OFF_GATE = OFF_KR + QK_ROPE

LANES = 128
MLA_PAD = 256
N_ROPE_TABLES = 5
VMEM_LIMIT = 56 * 1024 * 1024

BF16 = jnp.bfloat16
F32 = jnp.float32


def _params(sem):
    return pltpu.CompilerParams(dimension_semantics=sem, vmem_limit_bytes=VMEM_LIMIT)


def _dot(a, b):
    return jnp.dot(a, b, preferred_element_type=F32)


def _dot_nt(a, b):
    return lax.dot_general(a, b, (((1,), (1,)), ((), ())), preferred_element_type=F32)


def _norm_mod(x, g, shift, scale):
    y = x * lax.rsqrt(jnp.mean(x * x, axis=-1, keepdims=True) + EPS)
    return (y * g) * (1.0 + scale) + shift


def _rope_table_kernel(pos_ref, freq_ref, o_ref):
    ang = pos_ref[...].astype(F32) * freq_ref[...]
    cs = jnp.cos(ang)
    sn = jnp.sin(ang)
    lane = lax.broadcasted_iota(jnp.int32, cs.shape, 1)
    cs64 = pltpu.roll(cs, 64, 1)
    sn64 = pltpu.roll(sn, 64, 1)
    cs96 = pltpu.roll(cs, 96, 1)
    sn96 = pltpu.roll(sn, 96, 1)
    zero = jnp.zeros_like(cs)
    o_ref[:, 0 * LANES:1 * LANES] = jnp.where(lane < 64, cs, cs64)
    o_ref[:, 1 * LANES:2 * LANES] = jnp.where(lane < 64, -sn, sn64)
    o_ref[:, 2 * LANES:3 * LANES] = jnp.where(lane < 32, cs64, jnp.where(lane < 64, cs96, zero))
    o_ref[:, 3 * LANES:4 * LANES] = jnp.where(lane < 32, -sn64, zero)
    o_ref[:, 4 * LANES:5 * LANES] = jnp.where((lane >= 32) & (lane < 64), sn96, zero)


def _rope_tables(positions):
    tm = 1024
    fa = ROPE_THETA ** (-jnp.arange(0, HEAD_DIM, 2, dtype=F32) / HEAD_DIM)
    fb = ROPE_THETA ** (-jnp.arange(0, QK_ROPE, 2, dtype=F32) / QK_ROPE)
    freq = jnp.concatenate([fa, fb, jnp.zeros((32,), F32)])[None, :]
    pos = positions.reshape(TOKENS, 1)
    return pl.pallas_call(
        _rope_table_kernel,
        grid=(TOKENS // tm,),
        in_specs=[pl.BlockSpec((tm, 1), lambda i: (i, 0)),
                  pl.BlockSpec((1, LANES), lambda i: (0, 0))],
        out_specs=pl.BlockSpec((tm, N_ROPE_TABLES * LANES), lambda i: (i, 0)),
        out_shape=jax.ShapeDtypeStruct((TOKENS, N_ROPE_TABLES * LANES), F32),
        compiler_params=_params(("parallel",)),
        name="rope_tables",
    )(pos, freq)


def _ada_kernel(c_ref, w_ref, b_ref, o_ref):
    c = c_ref[...]
    c_act = (c * jax.nn.sigmoid(c)).astype(BF16)
    o_ref[...] = _dot(c_act, w_ref[...].astype(BF16)) + b_ref[...]


def _ada_mod(c, w_ada, b_ada):
    tn = 1024
    n = N_MOD * D_MODEL
    return pl.pallas_call(
        _ada_kernel,
        grid=(DEPTH, n // tn),
        in_specs=[pl.BlockSpec((BATCH, D_MODEL), lambda l, j: (0, 0)),
                  pl.BlockSpec((None, D_MODEL, tn), lambda l, j: (l, 0, j)),
                  pl.BlockSpec((None, 1, tn), lambda l, j: (l, 0, j))],
        out_specs=pl.BlockSpec((None, BATCH, tn), lambda l, j: (l, 0, j)),
        out_shape=jax.ShapeDtypeStruct((DEPTH, BATCH, n), F32),
        compiler_params=_params(("parallel", "parallel")),
        name="ada_mod",
    )(c, w_ada, b_ada.reshape(DEPTH, 1, n))


def _mod_spec(tm, k):
    return pl.BlockSpec((None, 1, D_MODEL), lambda i, *_: (i * tm // SEQ, 0, k))


FFN_ROW_CHUNK = 256


def _cast_specs(w_stack, layer, n_steps, step_index):
    _, rows, cols = w_stack.shape
    assert rows % n_steps == 0
    slab = rows // n_steps
    in_spec = pl.BlockSpec((None, slab, cols), lambda *ids: (layer, step_index(*ids), 0))
    out_spec = pl.BlockSpec((slab, cols), lambda *ids: (step_index(*ids), 0))
    return in_spec, out_spec, jax.ShapeDtypeStruct((rows, cols), BF16)


def _ffn_up_kernel(x_ref, g_ref, sh_ref, sc_ref, wg_ref, wu_ref, cast_ref, o_ref, cast_out_ref, h_ref):
    cast_out_ref[...] = cast_ref[...].astype(BF16)

    def swiglu(h):
        g = _dot(h, wg_ref[...])
        u = _dot(h, wu_ref[...])
        return (g * jax.nn.sigmoid(g) * u).astype(BF16)

    @pl.when(pl.program_id(1) == 0)
    def _():
        for r in range(0, x_ref.shape[0], FFN_ROW_CHUNK):
            rows = slice(r, r + FFN_ROW_CHUNK)
            h = _norm_mod(x_ref[rows, :], g_ref[...], sh_ref[...], sc_ref[...]).astype(BF16)
            h_ref[rows, :] = h
            o_ref[rows, :] = swiglu(h)

    @pl.when(pl.program_id(1) != 0)
    def _():
        o_ref[...] = swiglu(h_ref[...])


def _ffn_up(x, g, mod, k_shift, w_gu, w_d_stack, layer):
    tm, tn = 1024, 512
    nj = D_FF // tn
    ni = TOKENS // tm
    cast_in, cast_out, cast_shape = _cast_specs(w_d_stack, layer, ni * nj, lambda i, j: i * nj + j)
    return pl.pallas_call(
        _ffn_up_kernel,
        grid=(ni, nj),
        in_specs=[pl.BlockSpec((tm, D_MODEL), lambda i, j: (i, 0)),
                  pl.BlockSpec((1, D_MODEL), lambda i, j: (0, 0)),
                  _mod_spec(tm, k_shift), _mod_spec(tm, k_shift + 1),
                  pl.BlockSpec((D_MODEL, tn), lambda i, j: (0, j)),
                  pl.BlockSpec((D_MODEL, tn), lambda i, j: (0, j + nj)),
                  cast_in],
        out_specs=[pl.BlockSpec((tm, tn), lambda i, j: (i, j)), cast_out],
        out_shape=[jax.ShapeDtypeStruct((TOKENS, D_FF), BF16), cast_shape],
        scratch_shapes=[pltpu.VMEM((tm, D_MODEL), BF16)],
        compiler_params=_params(("parallel", "arbitrary")),
        name="ffn_up",
    )(x, g, mod, mod, w_gu, w_gu, w_d_stack)


FFN_DOWN_COL_CHUNK = 512


def _ffn_down_kernel(a_ref, w_ref, x_ref, gt_ref, g_ref, sh_ref, sc_ref, *refs, tail, cast):
    if cast:
        cast_ref, *out_refs, cast_out_ref = refs
        cast_out_ref[...] = cast_ref[...].astype(BF16)
    else:
        out_refs = refs
    x_out = out_refs[0]
    a = a_ref[...]
    for c in range(0, D_MODEL, FFN_DOWN_COL_CHUNK):
        cols = slice(c, c + FFN_DOWN_COL_CHUNK)
        x_out[:, cols] = x_ref[:, cols] + (0.5 * gt_ref[:, cols]) * _dot(a, w_ref[:, cols])
    if tail == "final_norm":
        x_out[...] = _rms(x_out[...], g_ref[...])
    elif tail == "mixer_norm":
        out_refs[1][...] = _norm_mod(x_out[...], g_ref[...], sh_ref[...], sc_ref[...]).astype(BF16)


def _ffn_down(act, w_d, x, mod, k_gate, g_tail, k_shift_tail, tail, w_gu_stack=None, layer=None):
    tm = 256
    ni = TOKENS // tm
    row = lambda i: (i, 0)
    x_spec = pl.BlockSpec((tm, D_MODEL), row)
    in_specs = [pl.BlockSpec((tm, D_FF), row),
                pl.BlockSpec((D_FF, D_MODEL), lambda i: (0, 0), pipeline_mode=pl.Buffered(1)),
                x_spec,
                _mod_spec(tm, k_gate),
                pl.BlockSpec((1, D_MODEL), lambda i: (0, 0)),
                _mod_spec(tm, k_shift_tail), _mod_spec(tm, k_shift_tail + 1)]
    operands = [act, w_d, x, mod, g_tail, mod, mod]
    out_specs = [x_spec]
    out_shape = [jax.ShapeDtypeStruct((TOKENS, D_MODEL), F32)]
    if tail == "mixer_norm":
        out_specs.append(pl.BlockSpec((tm, D_MODEL), row))
        out_shape.append(jax.ShapeDtypeStruct((TOKENS, D_MODEL), BF16))
    cast = w_gu_stack is not None
    if cast:
        cast_in, cast_out, cast_shape = _cast_specs(w_gu_stack, layer, ni, lambda i: i)
        in_specs.append(cast_in)
        operands.append(w_gu_stack)
        out_specs.append(cast_out)
        out_shape.append(cast_shape)
    return pl.pallas_call(
        functools.partial(_ffn_down_kernel, tail=tail, cast=cast),
        grid=(ni,),
        in_specs=in_specs,
        out_specs=out_specs,
        out_shape=out_shape,
        compiler_params=_params(("parallel",)),
        name="ffn_down",
    )(*operands)


def _rms(x, g):
    return (x * lax.rsqrt(jnp.mean(x * x, axis=-1, keepdims=True) + EPS)) * g


def _in_proj_kernel(h_ref, tab_ref, wx_ref, gcq_ref, gckv_ref, wuq_ref, wukv_ref,
                    qa_ref, ka_ref, va_ref, qp_ref, kp_ref, vb_ref):
    h = h_ref[...]
    cos_a = tab_ref[:, 0 * LANES:1 * LANES]
    sin_a = tab_ref[:, 1 * LANES:2 * LANES]
    cos_b = tab_ref[:, 2 * LANES:3 * LANES]
    sin_lo = tab_ref[:, 3 * LANES:4 * LANES]
    sin_hi = tab_ref[:, 4 * LANES:5 * LANES]

    def rope_a(t):
        return t * cos_a + pltpu.roll(t, 64, 1) * sin_a

    def rope_b(t):
        return t * cos_b + pltpu.roll(t, 96, 1) * sin_lo + pltpu.roll(t, 32, 1) * sin_hi

    proj = _dot(h, wx_ref[...])

    for hd in range(HA_Q):
        sl = slice(hd * HEAD_DIM, (hd + 1) * HEAD_DIM)
        qa_ref[:, sl] = rope_a(proj[:, sl]).astype(BF16)
    for hd in range(HA_KV):
        sl = slice(hd * HEAD_DIM, (hd + 1) * HEAD_DIM)
        ka_ref[:, sl] = rope_a(proj[:, COLS_QA + hd * HEAD_DIM:COLS_QA + (hd + 1) * HEAD_DIM]).astype(BF16)
    va_ref[...] = proj[:, COLS_QA + COLS_KA:OFF_CQ].astype(BF16)

    cq = _rms(proj[:, OFF_CQ:OFF_CKV], gcq_ref[...]).astype(BF16)
    q_all = _dot(cq, wuq_ref[...])
    for hd in range(HB):
        base = hd * MLA_PAD
        qp_ref[base:base + QK_NOPE, :] = q_all[:, base:base + QK_NOPE].T.astype(BF16)
        qp_ref[base + QK_NOPE:base + MLA_PAD, :] = rope_b(q_all[:, base + QK_NOPE:base + MLA_PAD]).T.astype(BF16)

    ckv = _rms(proj[:, OFF_CKV:OFF_KR], gckv_ref[...]).astype(BF16)
    kv_all = _dot(ckv, wukv_ref[...])
    kr = rope_b(proj[:, OFF_KR:OFF_KR + LANES]).astype(BF16)
    for hd in range(HB):
        base = hd * MLA_PAD
        kp_ref[:, base:base + QK_NOPE] = kv_all[:, hd * QK_NOPE:(hd + 1) * QK_NOPE].astype(BF16)
        kp_ref[:, base + QK_NOPE:base + MLA_PAD] = kr
        vsl = slice(HB * QK_NOPE + hd * V_DIM, HB * QK_NOPE + (hd + 1) * V_DIM)
        vb_ref[hd * V_DIM:(hd + 1) * V_DIM, :] = kv_all[:, vsl].T.astype(BF16)


def _in_proj(h, tables, w_x, g_cq, g_ckv, w_uq, w_ukv):
    tm = 256
    const = lambda i: (0, 0)
    row = lambda i: (i, 0)
    nx = w_x.shape[1]
    col = lambda i: (0, i)
    resident = lambda shape: pl.BlockSpec(shape, const, pipeline_mode=pl.Buffered(1))
    outs = [(COLS_QA, True), (COLS_KA, True), (COLS_VA, True), (HB * MLA_PAD, False),
            (HB * MLA_PAD, True), (HB * V_DIM, False)]
    out_specs = [pl.BlockSpec((tm, n), row) if tok else pl.BlockSpec((n, tm), col) for n, tok in outs]
    out_shape = [jax.ShapeDtypeStruct((TOKENS, n) if tok else (n, TOKENS), BF16) for n, tok in outs]
    return pl.pallas_call(
        _in_proj_kernel,
        grid=(TOKENS // tm,),
        in_specs=[pl.BlockSpec((tm, D_MODEL), row),
                  pl.BlockSpec((tm, N_ROPE_TABLES * LANES), row),
                  resident((D_MODEL, nx)),
                  pl.BlockSpec((1, Q_RANK), const),
                  pl.BlockSpec((1, KV_RANK), const),
                  resident((Q_RANK, HB * MLA_PAD)),
                  resident((KV_RANK, HB * (QK_NOPE + V_DIM)))],
        out_specs=out_specs,
        out_shape=out_shape,
        compiler_params=_params(("parallel",)),
        name="in_proj",
    )(h, tables, w_x, g_cq, g_ckv, w_uq, w_ukv)


WIN_BLOCKS_PER_STEP = 4


WIN_KEYS = 3 * BLOCK
WIN_ROWS = GROUP * BLOCK
LOG2E = float(np.log2(np.e))


def _win_mask_bias():
    rel = jnp.arange(WIN_KEYS)[None, :] - (jnp.arange(WIN_ROWS) % BLOCK)[:, None]
    off = -BLOCK * jnp.arange(3)[:, None, None]
    return jnp.where(jnp.abs(rel[None] + off) <= WINDOW, 0.0, -1e30).astype(F32)


def _win_attn_kernel(sink_ref, bias_ref, q_ref, k_ref, v_ref, o_ref):
    hk = pl.program_id(1)
    nb = SEQ // BLOCK
    c = HEAD_DIM ** -0.5 * LOG2E
    row = lax.broadcasted_iota(jnp.int32, (WIN_ROWS, 1), 0)
    sk = jnp.zeros((WIN_ROWS, 1), F32)
    for g in range(GROUP):
        sk = jnp.where(row // BLOCK == g, sink_ref[hk * GROUP + g] * LOG2E, sk)
    for j in range(WIN_BLOCKS_PER_STEP):
        n = pl.program_id(2) * WIN_BLOCKS_PER_STEP + j
        first = jnp.clip(n - 1, 0, nb - 3)
        start = pl.multiple_of(first * BLOCK, BLOCK)
        kw = k_ref[pl.ds(start, WIN_KEYS), :]
        vw = v_ref[pl.ds(start, WIN_KEYS), :]
        q = jnp.concatenate([q_ref[j * BLOCK:(j + 1) * BLOCK, g * HEAD_DIM:(g + 1) * HEAD_DIM]
                             for g in range(GROUP)], axis=0)
        t = _dot_nt(q, kw) * c + bias_ref[n - first]
        m = jnp.maximum(jnp.max(t, axis=-1, keepdims=True), sk)
        p = jnp.exp2(t - m)
        denom = jnp.sum(p, axis=-1, keepdims=True) + jnp.exp2(sk - m)
        o = (_dot(p.astype(BF16), vw) * (1.0 / denom)).astype(BF16)
        for g in range(GROUP):
            o_ref[j * BLOCK:(j + 1) * BLOCK, g * HEAD_DIM:(g + 1) * HEAD_DIM] = o[g * BLOCK:(g + 1) * BLOCK]


def _win_attn(qa, ka, va, sink):
    tq = WIN_BLOCKS_PER_STEP * BLOCK
    nq = SEQ // tq
    gw = GROUP * HEAD_DIM
    return pl.pallas_call(
        _win_attn_kernel,
        grid=(BATCH, HA_KV, nq),
        in_specs=[pl.BlockSpec(memory_space=pltpu.SMEM),
                  pl.BlockSpec((3, WIN_ROWS, WIN_KEYS), lambda b, h, n: (0, 0, 0)),
                  pl.BlockSpec((tq, gw), lambda b, h, n: (b * nq + n, h)),
                  pl.BlockSpec((SEQ, HEAD_DIM), lambda b, h, n: (b, h)),
                  pl.BlockSpec((SEQ, HEAD_DIM), lambda b, h, n: (b, h))],
        out_specs=pl.BlockSpec((tq, gw), lambda b, h, n: (b * nq + n, h)),
        out_shape=jax.ShapeDtypeStruct((TOKENS, COLS_QA), BF16),
        compiler_params=_params(("parallel", "parallel", "arbitrary")),
        name="win_attn",
    )(sink, _win_mask_bias(), qa, ka, va)


MLA_KEY_CHUNK = 512
MLA_ONES_ROWS = 16


def _mla_attn_kernel(qt_ref, k_ref, vt_ref, o_ref):
    c = (QK_NOPE + QK_ROPE) ** -0.5 * LOG2E
    qt = qt_ref[...]
    tq = qt.shape[1]
    m = jnp.full((1, tq), -1e30, F32)
    acc = jnp.zeros((V_DIM + MLA_ONES_ROWS, tq), F32)
    ones = jnp.ones((MLA_ONES_ROWS, MLA_KEY_CHUNK), BF16)
    n_chunks = SEQ // MLA_KEY_CHUNK
    chunk = lambda ci: slice(ci * MLA_KEY_CHUNK, (ci + 1) * MLA_KEY_CHUNK)
    st_next = _dot(k_ref[chunk(0), :], qt)
    for ci in range(n_chunks):
        ks = chunk(ci)
        st = st_next
        if ci + 1 < n_chunks:
            st_next = _dot(k_ref[chunk(ci + 1), :], qt)
        m_new = jnp.maximum(m, jnp.max(st, axis=0, keepdims=True))
        alpha = jnp.exp2((m - m_new) * c)
        p = jnp.exp2((st - m_new) * c).astype(BF16)
        vt1 = jnp.concatenate([vt_ref[:, ks], ones], axis=0)
        acc = acc * alpha + _dot(vt1, p)
        m = m_new
    o_ref[...] = (acc[:V_DIM] * (1.0 / acc[V_DIM:V_DIM + 1])).T.astype(BF16)


def _mla_attn(q_pad_t, k_pad, v_t):
    tq = 1024
    nq = SEQ // tq
    return pl.pallas_call(
        _mla_attn_kernel,
        grid=(BATCH, HB, nq),
        in_specs=[pl.BlockSpec((MLA_PAD, tq), lambda b, h, i: (h, b * nq + i)),
                  pl.BlockSpec((SEQ, MLA_PAD), lambda b, h, i: (b, h)),
                  pl.BlockSpec((V_DIM, SEQ), lambda b, h, i: (h, b))],
        out_specs=pl.BlockSpec((tq, V_DIM), lambda b, h, i: (b * nq + i, h)),
        out_shape=jax.ShapeDtypeStruct((TOKENS, HB * V_DIM), BF16),
        compiler_params=_params(("parallel", "parallel", "arbitrary")),
        name="mla_attn",
    )(q_pad_t, k_pad, v_t)


MIX_COL_CHUNK = 256


def _mix_out_kernel(x_ref, h_ref, gt_ref, oa_ref, ob_ref, woa_ref, wob_ref,
                    wga_ref, wgb_ref, bga_ref, bgb_ref, wout_ref, o_ref, acc_ref):
    n = pl.program_id(1)

    @pl.when(n == 0)
    def _():
        acc_ref[...] = jnp.zeros_like(acc_ref)

    h = h_ref[...]
    o_a = oa_ref[...]
    o_b = ob_ref[...]
    mix = []
    for c in range(0, wout_ref.shape[0], MIX_COL_CHUNK):
        cols = slice(c, c + MIX_COL_CHUNK)
        y_a = _dot(o_a, woa_ref[:, cols])
        y_b = _dot(o_b, wob_ref[:, cols])
        g_a = jax.nn.sigmoid(_dot(h, wga_ref[:, cols]) + bga_ref[:, cols])
        g_b = jax.nn.sigmoid(_dot(h, wgb_ref[:, cols]) + bgb_ref[:, cols])
        mix.append((g_a * y_a + g_b * y_b).astype(BF16))
    acc_ref[...] += _dot(jnp.concatenate(mix, axis=1), wout_ref[...])

    @pl.when(n == pl.num_programs(1) - 1)
    def _():
        o_ref[...] = x_ref[...] + gt_ref[...] * acc_ref[...]


def _mix_out(x, h, mod, k_gate, o_a, o_b, w_oa, w_ob, w_gate, b_gate, w_out):
    tm, tc = 512, 512
    nc = D_MODEL // tc
    row = lambda i, n: (i, 0)
    col = lambda i, n: (0, n)
    col_hi = lambda i, n: (0, n + nc)
    return pl.pallas_call(
        _mix_out_kernel,
        grid=(TOKENS // tm, nc),
        in_specs=[pl.BlockSpec((tm, D_MODEL), row),
                  pl.BlockSpec((tm, D_MODEL), row),
                  _mod_spec(tm, k_gate),
                  pl.BlockSpec((tm, COLS_QA), row),
                  pl.BlockSpec((tm, HB * V_DIM), row),
                  pl.BlockSpec((COLS_QA, tc), col),
                  pl.BlockSpec((HB * V_DIM, tc), col),
                  pl.BlockSpec((D_MODEL, tc), col),
                  pl.BlockSpec((D_MODEL, tc), col_hi),
                  pl.BlockSpec((1, tc), col),
                  pl.BlockSpec((1, tc), col_hi),
                  pl.BlockSpec((tc, D_MODEL), lambda i, n: (n, 0))],
        out_specs=pl.BlockSpec((tm, D_MODEL), row),
        out_shape=jax.ShapeDtypeStruct((TOKENS, D_MODEL), F32),
        scratch_shapes=[pltpu.VMEM((tm, D_MODEL), F32)],
        compiler_params=_params(("parallel", "arbitrary")),
        name="mix_out",
    )(x, h, mod, o_a, o_b, w_oa, w_ob, w_gate, w_gate, b_gate, b_gate, w_out)


def _pad_heads(w, n_heads, width, pad_to):
    k = w.shape[0]
    w = w.reshape(k, n_heads, width)
    w = jnp.pad(w, ((0, 0), (0, 0), (0, pad_to - width)))
    return w.reshape(k, n_heads * pad_to)


def _layer_weights(l, w_in, w_uq, w_ukv):
    w = w_in[l]
    kr_pad = jnp.pad(w[:, OFF_KR:OFF_GATE], ((0, 0), (0, LANES - QK_ROPE)))
    w_x = jnp.concatenate([w[:, :OFF_KR], kr_pad], axis=1).astype(BF16)
    w_gate = w[:, OFF_GATE:].astype(BF16)
    w_uq_pad = _pad_heads(w_uq[l], HB, QK_NOPE + QK_ROPE, MLA_PAD).astype(BF16)
    ukv = w_ukv[l].reshape(KV_RANK, HB, QK_NOPE + V_DIM)
    w_ukv_perm = jnp.concatenate([ukv[:, :, :QK_NOPE].reshape(KV_RANK, HB * QK_NOPE),
                                  ukv[:, :, QK_NOPE:].reshape(KV_RANK, HB * V_DIM)], axis=1).astype(BF16)
    return w_x, w_gate, w_uq_pad, w_ukv_perm


def kernel(x, c, positions, norm_g, w_ada, b_ada, w_ffn1_gu, w_ffn1_d, w_ffn2_gu, w_ffn2_d, w_in, b_gate, sink,
           g_cq, g_ckv, w_uq, w_ukv, w_oa, w_ob, w_out, g_final):
    tables = _rope_tables(positions)
    mods = _ada_mod(c, w_ada, b_ada)
    xt = x.reshape(TOKENS, D_MODEL)
    w_gu = w_ffn1_gu[0].astype(BF16)
    for l in range(DEPTH):
        mod = mods[l].reshape(BATCH, 1, N_MOD * D_MODEL)
        g = norm_g[l]
        w_x, w_gate, w_uq_pad, w_ukv_perm = _layer_weights(l, w_in, w_uq, w_ukv)

        act, w_d = _ffn_up(xt, g[0:1], mod, 0, w_gu, w_ffn1_d, l)
        xt, h, w_gu = _ffn_down(act, w_d, xt, mod, 2, g[1:2], 3, "mixer_norm", w_ffn2_gu, l)

        qa, ka, va, q_pad_t, k_pad, v_t = _in_proj(h, tables, w_x, g_cq[l][None, :], g_ckv[l][None, :],
                                                   w_uq_pad, w_ukv_perm)
        o_a = _win_attn(qa, ka, va, sink[l])
        o_b = _mla_attn(q_pad_t, k_pad, v_t)
        xt = _mix_out(xt, h, mod, 5, o_a, o_b, w_oa[l].astype(BF16), w_ob[l].astype(BF16), w_gate,
                      b_gate[l][None, :], w_out[l].astype(BF16))

        act, w_d = _ffn_up(xt, g[2:3], mod, 6, w_gu, w_ffn2_d, l)
        if l + 1 < DEPTH:
            xt, w_gu = _ffn_down(act, w_d, xt, mod, 8, g_final[None, :], 0, "plain", w_ffn1_gu, l + 1)
        else:
            xt, = _ffn_down(act, w_d, xt, mod, 8, g_final[None, :], 0, "final_norm")
    return xt.reshape(BATCH, SEQ, D_MODEL)
```

```python
import functools

import jax
import jax.numpy as jnp
import numpy as np
from jax import lax
from jax.experimental import pallas as pl
from jax.experimental.pallas import tpu as pltpu

D_MODEL = 2048
BATCH = 8
SEQ = 4096
DEPTH = 2
HA_Q = 8
HA_KV = 2
GROUP = HA_Q // HA_KV
HEAD_DIM = 128
WINDOW = 128
BLOCK = 128
HB = 8
QK_NOPE = 128
QK_ROPE = 64
V_DIM = 128
Q_RANK = 512
KV_RANK = 512
D_FF = 5632
ROPE_THETA = 10000.0
EPS = 1e-6
N_MOD = 9
TOKENS = BATCH * SEQ

COLS_QA = HA_Q * HEAD_DIM
COLS_KA = HA_KV * HEAD_DIM
COLS_VA = HA_KV * HEAD_DIM
OFF_CQ = COLS_QA + COLS_KA + COLS_VA
OFF_CKV = OFF_CQ + Q_RANK
OFF_KR = OFF_CKV + KV_RANK
OFF_GATE = OFF_KR + QK_ROPE

LANES = 128
MLA_PAD = 256
N_ROPE_TABLES = 5
VMEM_LIMIT = 56 * 1024 * 1024

BF16 = jnp.bfloat16
F32 = jnp.float32


def _params(sem):
    return pltpu.CompilerParams(dimension_semantics=sem, vmem_limit_bytes=VMEM_LIMIT)


def _dot(a, b):
    return jnp.dot(a, b, preferred_element_type=F32)


def _dot_nt(a, b):
    return lax.dot_general(a, b, (((1,), (1,)), ((), ())), preferred_element_type=F32)


def _norm_mod(x, g, shift, scale):
    y = x * lax.rsqrt(jnp.mean(x * x, axis=-1, keepdims=True) + EPS)
    return (y * g) * (1.0 + scale) + shift


def _rope_table_kernel(pos_ref, freq_ref, o_ref):
    ang = pos_ref[...].astype(F32) * freq_ref[...]
    cs = jnp.cos(ang)
    sn = jnp.sin(ang)
    lane = lax.broadcasted_iota(jnp.int32, cs.shape, 1)
    cs64 = pltpu.roll(cs, 64, 1)
    sn64 = pltpu.roll(sn, 64, 1)
    cs96 = pltpu.roll(cs, 96, 1)
    sn96 = pltpu.roll(sn, 96, 1)
    zero = jnp.zeros_like(cs)
    o_ref[:, 0 * LANES:1 * LANES] = jnp.where(lane < 64, cs, cs64)
    o_ref[:, 1 * LANES:2 * LANES] = jnp.where(lane < 64, -sn, sn64)
    o_ref[:, 2 * LANES:3 * LANES] = jnp.where(lane < 32, cs64, jnp.where(lane < 64, cs96, zero))
    o_ref[:, 3 * LANES:4 * LANES] = jnp.where(lane < 32, -sn64, zero)
    o_ref[:, 4 * LANES:5 * LANES] = jnp.where((lane >= 32) & (lane < 64), sn96, zero)


def _rope_tables(positions):
    tm = 1024
    fa = ROPE_THETA ** (-jnp.arange(0, HEAD_DIM, 2, dtype=F32) / HEAD_DIM)
    fb = ROPE_THETA ** (-jnp.arange(0, QK_ROPE, 2, dtype=F32) / QK_ROPE)
    freq = jnp.concatenate([fa, fb, jnp.zeros((32,), F32)])[None, :]
    pos = positions.reshape(TOKENS, 1)
    return pl.pallas_call(
        _rope_table_kernel,
        grid=(TOKENS // tm,),
        in_specs=[pl.BlockSpec((tm, 1), lambda i: (i, 0)),
                  pl.BlockSpec((1, LANES), lambda i: (0, 0))],
        out_specs=pl.BlockSpec((tm, N_ROPE_TABLES * LANES), lambda i: (i, 0)),
        out_shape=jax.ShapeDtypeStruct((TOKENS, N_ROPE_TABLES * LANES), F32),
        compiler_params=_params(("parallel",)),
        name="rope_tables",
    )(pos, freq)


def _ada_kernel(c_ref, w_ref, b_ref, o_ref):
    c = c_ref[...]
    c_act = (c * jax.nn.sigmoid(c)).astype(BF16)
    o_ref[...] = _dot(c_act, w_ref[...].astype(BF16)) + b_ref[...]


def _ada_mod(c, w_ada, b_ada):
    tn = 1024
    n = N_MOD * D_MODEL
    return pl.pallas_call(
        _ada_kernel,
        grid=(DEPTH, n // tn),
        in_specs=[pl.BlockSpec((BATCH, D_MODEL), lambda l, j: (0, 0)),
                  pl.BlockSpec((None, D_MODEL, tn), lambda l, j: (l, 0, j)),
                  pl.BlockSpec((None, 1, tn), lambda l, j: (l, 0, j))],
        out_specs=pl.BlockSpec((None, BATCH, tn), lambda l, j: (l, 0, j)),
        out_shape=jax.ShapeDtypeStruct((DEPTH, BATCH, n), F32),
        compiler_params=_params(("parallel", "parallel")),
        name="ada_mod",
    )(c, w_ada, b_ada.reshape(DEPTH, 1, n))


def _mod_spec(tm, k):
    return pl.BlockSpec((None, 1, D_MODEL), lambda i, *_: (i * tm // SEQ, 0, k))


FFN_ROW_CHUNK = 256


def _cast_specs(w_stack, layer, n_steps, step_index):
    _, rows, cols = w_stack.shape
    assert rows % n_steps == 0
    slab = rows // n_steps
    in_spec = pl.BlockSpec((None, slab, cols), lambda *ids: (layer, step_index(*ids), 0))
    out_spec = pl.BlockSpec((slab, cols), lambda *ids: (step_index(*ids), 0))
    return in_spec, out_spec, jax.ShapeDtypeStruct((rows, cols), BF16)


def _ffn_up_kernel(x_ref, g_ref, sh_ref, sc_ref, wg_ref, wu_ref, cast_ref, o_ref, cast_out_ref, h_ref):
    cast_out_ref[...] = cast_ref[...].astype(BF16)

    def swiglu(h):
        g = _dot(h, wg_ref[...])
        u = _dot(h, wu_ref[...])
        return (g * jax.nn.sigmoid(g) * u).astype(BF16)

    @pl.when(pl.program_id(1) == 0)
    def _():
        for r in range(0, x_ref.shape[0], FFN_ROW_CHUNK):
            rows = slice(r, r + FFN_ROW_CHUNK)
            h = _norm_mod(x_ref[rows, :], g_ref[...], sh_ref[...], sc_ref[...]).astype(BF16)
            h_ref[rows, :] = h
            o_ref[rows, :] = swiglu(h)

    @pl.when(pl.program_id(1) != 0)
    def _():
        o_ref[...] = swiglu(h_ref[...])


def _ffn_up(x, g, mod, k_shift, w_gu, w_d_stack, layer):
    tm, tn = 1024, 512
    nj = D_FF // tn
    ni = TOKENS // tm
    cast_in, cast_out, cast_shape = _cast_specs(w_d_stack, layer, ni * nj, lambda i, j: i * nj + j)
    return pl.pallas_call(
        _ffn_up_kernel,
        grid=(ni, nj),
        in_specs=[pl.BlockSpec((tm, D_MODEL), lambda i, j: (i, 0)),
                  pl.BlockSpec((1, D_MODEL), lambda i, j: (0, 0)),
                  _mod_spec(tm, k_shift), _mod_spec(tm, k_shift + 1),
                  pl.BlockSpec((D_MODEL, tn), lambda i, j: (0, j)),
                  pl.BlockSpec((D_MODEL, tn), lambda i, j: (0, j + nj)),
                  cast_in],
        out_specs=[pl.BlockSpec((tm, tn), lambda i, j: (i, j)), cast_out],
        out_shape=[jax.ShapeDtypeStruct((TOKENS, D_FF), BF16), cast_shape],
        scratch_shapes=[pltpu.VMEM((tm, D_MODEL), BF16)],
        compiler_params=_params(("parallel", "arbitrary")),
        name="ffn_up",
    )(x, g, mod, mod, w_gu, w_gu, w_d_stack)


FFN_DOWN_COL_CHUNK = 512


def _ffn_down_kernel(a_ref, w_ref, x_ref, gt_ref, g_ref, sh_ref, sc_ref, *refs, tail, cast):
    if cast:
        cast_ref, *out_refs, cast_out_ref = refs
        cast_out_ref[...] = cast_ref[...].astype(BF16)
    else:
        out_refs = refs
    x_out = out_refs[0]
    a = a_ref[...]
    for c in range(0, D_MODEL, FFN_DOWN_COL_CHUNK):
        cols = slice(c, c + FFN_DOWN_COL_CHUNK)
        x_out[:, cols] = x_ref[:, cols] + (0.5 * gt_ref[:, cols]) * _dot(a, w_ref[:, cols])
    if tail == "final_norm":
        x_out[...] = _rms(x_out[...], g_ref[...])
    elif tail == "mixer_norm":
        out_refs[1][...] = _norm_mod(x_out[...], g_ref[...], sh_ref[...], sc_ref[...]).astype(BF16)


def _ffn_down(act, w_d, x, mod, k_gate, g_tail, k_shift_tail, tail, w_gu_stack=None, layer=None):
    tm = 256
    ni = TOKENS // tm
    row = lambda i: (i, 0)
    x_spec = pl.BlockSpec((tm, D_MODEL), row)
    in_specs = [pl.BlockSpec((tm, D_FF), row),
                pl.BlockSpec((D_FF, D_MODEL), lambda i: (0, 0), pipeline_mode=pl.Buffered(1)),
                x_spec,
                _mod_spec(tm, k_gate),
                pl.BlockSpec((1, D_MODEL), lambda i: (0, 0)),
                _mod_spec(tm, k_shift_tail), _mod_spec(tm, k_shift_tail + 1)]
    operands = [act, w_d, x, mod, g_tail, mod, mod]
    out_specs = [x_spec]
    out_shape = [jax.ShapeDtypeStruct((TOKENS, D_MODEL), F32)]
    if tail == "mixer_norm":
        out_specs.append(pl.BlockSpec((tm, D_MODEL), row))
        out_shape.append(jax.ShapeDtypeStruct((TOKENS, D_MODEL), BF16))
    cast = w_gu_stack is not None
    if cast:
        cast_in, cast_out, cast_shape = _cast_specs(w_gu_stack, layer, ni, lambda i: i)
        in_specs.append(cast_in)
        operands.append(w_gu_stack)
        out_specs.append(cast_out)
        out_shape.append(cast_shape)
    return pl.pallas_call(
        functools.partial(_ffn_down_kernel, tail=tail, cast=cast),
        grid=(ni,),
        in_specs=in_specs,
        out_specs=out_specs,
        out_shape=out_shape,
        compiler_params=_params(("parallel",)),
        name="ffn_down",
    )(*operands)


def _rms(x, g):
    return (x * lax.rsqrt(jnp.mean(x * x, axis=-1, keepdims=True) + EPS)) * g


def _in_proj_kernel(h_ref, tab_ref, wx_ref, gcq_ref, gckv_ref, wuq_ref, wukv_ref,
                    qa_ref, ka_ref, va_ref, qp_ref, kp_ref, vb_ref):
    h = h_ref[...]
    cos_a = tab_ref[:, 0 * LANES:1 * LANES]
    sin_a = tab_ref[:, 1 * LANES:2 * LANES]
    cos_b = tab_ref[:, 2 * LANES:3 * LANES]
    sin_lo = tab_ref[:, 3 * LANES:4 * LANES]
    sin_hi = tab_ref[:, 4 * LANES:5 * LANES]

    def rope_a(t):
        return t * cos_a + pltpu.roll(t, 64, 1) * sin_a

    def rope_b(t):
        return t * cos_b + pltpu.roll(t, 96, 1) * sin_lo + pltpu.roll(t, 32, 1) * sin_hi

    proj = _dot(h, wx_ref[...])

    for hd in range(HA_Q):
        sl = slice(hd * HEAD_DIM, (hd + 1) * HEAD_DIM)
        qa_ref[:, sl] = rope_a(proj[:, sl]).astype(BF16)
    for hd in range(HA_KV):
        sl = slice(hd * HEAD_DIM, (hd + 1) * HEAD_DIM)
        ka_ref[:, sl] = rope_a(proj[:, COLS_QA + hd * HEAD_DIM:COLS_QA + (hd + 1) * HEAD_DIM]).astype(BF16)
    va_ref[...] = proj[:, COLS_QA + COLS_KA:OFF_CQ].astype(BF16)

    cq = _rms(proj[:, OFF_CQ:OFF_CKV], gcq_ref[...]).astype(BF16)
    q_all = _dot(cq, wuq_ref[...])
    for hd in range(HB):
        base = hd * MLA_PAD
        qp_ref[base:base + QK_NOPE, :] = q_all[:, base:base + QK_NOPE].T.astype(BF16)
        qp_ref[base + QK_NOPE:base + MLA_PAD, :] = rope_b(q_all[:, base + QK_NOPE:base + MLA_PAD]).T.astype(BF16)

    ckv = _rms(proj[:, OFF_CKV:OFF_KR], gckv_ref[...]).astype(BF16)
    kv_all = _dot(ckv, wukv_ref[...])
    kr = rope_b(proj[:, OFF_KR:OFF_KR + LANES]).astype(BF16)
    for hd in range(HB):
        base = hd * MLA_PAD
        kp_ref[:, base:base + QK_NOPE] = kv_all[:, hd * QK_NOPE:(hd + 1) * QK_NOPE].astype(BF16)
        kp_ref[:, base + QK_NOPE:base + MLA_PAD] = kr
        vsl = slice(HB * QK_NOPE + hd * V_DIM, HB * QK_NOPE + (hd + 1) * V_DIM)
        vb_ref[hd * V_DIM:(hd + 1) * V_DIM, :] = kv_all[:, vsl].T.astype(BF16)


def _in_proj(h, tables, w_x, g_cq, g_ckv, w_uq, w_ukv):
    tm = 512
    const = lambda i: (0, 0)
    row = lambda i: (i, 0)
    nx = w_x.shape[1]
    col = lambda i: (0, i)
    resident = lambda shape: pl.BlockSpec(shape, const, pipeline_mode=pl.Buffered(1))
    outs = [(COLS_QA, True), (COLS_KA, True), (COLS_VA, True), (HB * MLA_PAD, False),
            (HB * MLA_PAD, True), (HB * V_DIM, False)]
    out_specs = [pl.BlockSpec((tm, n), row) if tok else pl.BlockSpec((n, tm), col) for n, tok in outs]
    out_shape = [jax.ShapeDtypeStruct((TOKENS, n) if tok else (n, TOKENS), BF16) for n, tok in outs]
    return pl.pallas_call(
        _in_proj_kernel,
        grid=(TOKENS // tm,),
        in_specs=[pl.BlockSpec((tm, D_MODEL), row),
                  pl.BlockSpec((tm, N_ROPE_TABLES * LANES), row),
                  resident((D_MODEL, nx)),
                  pl.BlockSpec((1, Q_RANK), const),
                  pl.BlockSpec((1, KV_RANK), const),
                  resident((Q_RANK, HB * MLA_PAD)),
                  resident((KV_RANK, HB * (QK_NOPE + V_DIM)))],
        out_specs=out_specs,
        out_shape=out_shape,
        compiler_params=_params(("parallel",)),
        name="in_proj",
    )(h, tables, w_x, g_cq, g_ckv, w_uq, w_ukv)


WIN_BLOCKS_PER_STEP = 4


WIN_KEYS = 3 * BLOCK
WIN_ROWS = GROUP * BLOCK
LOG2E = float(np.log2(np.e))


def _win_mask_bias():
    rel = jnp.arange(WIN_KEYS)[None, :] - (jnp.arange(WIN_ROWS) % BLOCK)[:, None]
    off = -BLOCK * jnp.arange(3)[:, None, None]
    return jnp.where(jnp.abs(rel[None] + off) <= WINDOW, 0.0, -1e30).astype(F32)


def _win_attn_kernel(sink_ref, bias_ref, q_ref, k_ref, v_ref, o_ref):
    hk = pl.program_id(1)
    nb = SEQ // BLOCK
    c = HEAD_DIM ** -0.5 * LOG2E
    row = lax.broadcasted_iota(jnp.int32, (WIN_ROWS, 1), 0)
    sk = jnp.zeros((WIN_ROWS, 1), F32)
    for g in range(GROUP):
        sk = jnp.where(row // BLOCK == g, sink_ref[hk * GROUP + g] * LOG2E, sk)
    for j in range(WIN_BLOCKS_PER_STEP):
        n = pl.program_id(2) * WIN_BLOCKS_PER_STEP + j
        first = jnp.clip(n - 1, 0, nb - 3)
        start = pl.multiple_of(first * BLOCK, BLOCK)
        kw = k_ref[pl.ds(start, WIN_KEYS), :]
        vw = v_ref[pl.ds(start, WIN_KEYS), :]
        q = jnp.concatenate([q_ref[j * BLOCK:(j + 1) * BLOCK, g * HEAD_DIM:(g + 1) * HEAD_DIM]
                             for g in range(GROUP)], axis=0)
        t = _dot_nt(q, kw) * c + bias_ref[n - first]
        m = jnp.maximum(jnp.max(t, axis=-1, keepdims=True), sk)
        p = jnp.exp2(t - m)
        denom = jnp.sum(p, axis=-1, keepdims=True) + jnp.exp2(sk - m)
        o = (_dot(p.astype(BF16), vw) * (1.0 / denom)).astype(BF16)
        for g in range(GROUP):
            o_ref[j * BLOCK:(j + 1) * BLOCK, g * HEAD_DIM:(g + 1) * HEAD_DIM] = o[g * BLOCK:(g + 1) * BLOCK]


def _win_attn(qa, ka, va, sink):
    tq = WIN_BLOCKS_PER_STEP * BLOCK
    nq = SEQ // tq
    gw = GROUP * HEAD_DIM
    return pl.pallas_call(
        _win_attn_kernel,
        grid=(BATCH, HA_KV, nq),
        in_specs=[pl.BlockSpec(memory_space=pltpu.SMEM),
                  pl.BlockSpec((3, WIN_ROWS, WIN_KEYS), lambda b, h, n: (0, 0, 0)),
                  pl.BlockSpec((tq, gw), lambda b, h, n: (b * nq + n, h)),
                  pl.BlockSpec((SEQ, HEAD_DIM), lambda b, h, n: (b, h)),
                  pl.BlockSpec((SEQ, HEAD_DIM), lambda b, h, n: (b, h))],
        out_specs=pl.BlockSpec((tq, gw), lambda b, h, n: (b * nq + n, h)),
        out_shape=jax.ShapeDtypeStruct((TOKENS, COLS_QA), BF16),
        compiler_params=_params(("parallel", "parallel", "arbitrary")),
        name="win_attn",
    )(sink, _win_mask_bias(), qa, ka, va)


MLA_KEY_CHUNK = 512
MLA_ONES_ROWS = 16


def _mla_attn_kernel(qt_ref, k_ref, vt_ref, o_ref):
    c = (QK_NOPE + QK_ROPE) ** -0.5 * LOG2E
    qt = qt_ref[...]
    tq = qt.shape[1]
    m = jnp.full((1, tq), -1e30, F32)
    acc = jnp.zeros((V_DIM + MLA_ONES_ROWS, tq), F32)
    ones = jnp.ones((MLA_ONES_ROWS, MLA_KEY_CHUNK), BF16)
    n_chunks = SEQ // MLA_KEY_CHUNK
    chunk = lambda ci: slice(ci * MLA_KEY_CHUNK, (ci + 1) * MLA_KEY_CHUNK)
    st_next = _dot(k_ref[chunk(0), :], qt)
    for ci in range(n_chunks):
        ks = chunk(ci)
        st = st_next
        if ci + 1 < n_chunks:
            st_next = _dot(k_ref[chunk(ci + 1), :], qt)
        m_new = jnp.maximum(m, jnp.max(st, axis=0, keepdims=True))
        alpha = jnp.exp2((m - m_new) * c)
        p = jnp.exp2((st - m_new) * c).astype(BF16)
        vt1 = jnp.concatenate([vt_ref[:, ks], ones], axis=0)
        acc = acc * alpha + _dot(vt1, p)
        m = m_new
    o_ref[...] = (acc[:V_DIM] * (1.0 / acc[V_DIM:V_DIM + 1])).T.astype(BF16)


def _mla_attn(q_pad_t, k_pad, v_t):
    tq = 2048
    nq = SEQ // tq
    return pl.pallas_call(
        _mla_attn_kernel,
        grid=(BATCH, HB, nq),
        in_specs=[pl.BlockSpec((MLA_PAD, tq), lambda b, h, i: (h, b * nq + i)),
                  pl.BlockSpec((SEQ, MLA_PAD), lambda b, h, i: (b, h)),
                  pl.BlockSpec((V_DIM, SEQ), lambda b, h, i: (h, b))],
        out_specs=pl.BlockSpec((tq, V_DIM), lambda b, h, i: (b * nq + i, h)),
        out_shape=jax.ShapeDtypeStruct((TOKENS, HB * V_DIM), BF16),
        compiler_params=_params(("parallel", "parallel", "arbitrary")),
        name="mla_attn",
    )(q_pad_t, k_pad, v_t)


MIX_COL_CHUNK = 256


def _mix_out_kernel(x_ref, h_ref, gt_ref, oa_ref, ob_ref, woa_ref, wob_ref,
                    wga_ref, wgb_ref, bga_ref, bgb_ref, wout_ref, o_ref, acc_ref):
    n = pl.program_id(1)

    @pl.when(n == 0)
    def _():
        acc_ref[...] = jnp.zeros_like(acc_ref)

    h = h_ref[...]
    o_a = oa_ref[...]
    o_b = ob_ref[...]
    mix = []
    for c in range(0, wout_ref.shape[0], MIX_COL_CHUNK):
        cols = slice(c, c + MIX_COL_CHUNK)
        y_a = _dot(o_a, woa_ref[:, cols])
        y_b = _dot(o_b, wob_ref[:, cols])
        g_a = jax.nn.sigmoid(_dot(h, wga_ref[:, cols]) + bga_ref[:, cols])
        g_b = jax.nn.sigmoid(_dot(h, wgb_ref[:, cols]) + bgb_ref[:, cols])
        mix.append((g_a * y_a + g_b * y_b).astype(BF16))
    acc_ref[...] += _dot(jnp.concatenate(mix, axis=1), wout_ref[...])

    @pl.when(n == pl.num_programs(1) - 1)
    def _():
        o_ref[...] = x_ref[...] + gt_ref[...] * acc_ref[...]


def _mix_out(x, h, mod, k_gate, o_a, o_b, w_oa, w_ob, w_gate, b_gate, w_out):
    tm, tc = 512, 512
    nc = D_MODEL // tc
    row = lambda i, n: (i, 0)
    col = lambda i, n: (0, n)
    col_hi = lambda i, n: (0, n + nc)
    return pl.pallas_call(
        _mix_out_kernel,
        grid=(TOKENS // tm, nc),
        in_specs=[pl.BlockSpec((tm, D_MODEL), row),
                  pl.BlockSpec((tm, D_MODEL), row),
                  _mod_spec(tm, k_gate),
                  pl.BlockSpec((tm, COLS_QA), row),
                  pl.BlockSpec((tm, HB * V_DIM), row),
                  pl.BlockSpec((COLS_QA, tc), col),
                  pl.BlockSpec((HB * V_DIM, tc), col),
                  pl.BlockSpec((D_MODEL, tc), col),
                  pl.BlockSpec((D_MODEL, tc), col_hi),
                  pl.BlockSpec((1, tc), col),
                  pl.BlockSpec((1, tc), col_hi),
                  pl.BlockSpec((tc, D_MODEL), lambda i, n: (n, 0))],
        out_specs=pl.BlockSpec((tm, D_MODEL), row),
        out_shape=jax.ShapeDtypeStruct((TOKENS, D_MODEL), F32),
        scratch_shapes=[pltpu.VMEM((tm, D_MODEL), F32)],
        compiler_params=_params(("parallel", "arbitrary")),
        name="mix_out",
    )(x, h, mod, o_a, o_b, w_oa, w_ob, w_gate, w_gate, b_gate, b_gate, w_out)


def _pad_heads(w, n_heads, width, pad_to):
    k = w.shape[0]
    w = w.reshape(k, n_heads, width)
    w = jnp.pad(w, ((0, 0), (0, 0), (0, pad_to - width)))
    return w.reshape(k, n_heads * pad_to)


def _layer_weights(l, w_in, w_uq, w_ukv):
    w = w_in[l]
    kr_pad = jnp.pad(w[:, OFF_KR:OFF_GATE], ((0, 0), (0, LANES - QK_ROPE)))
    w_x = jnp.concatenate([w[:, :OFF_KR], kr_pad], axis=1).astype(BF16)
    w_gate = w[:, OFF_GATE:].astype(BF16)
    w_uq_pad = _pad_heads(w_uq[l], HB, QK_NOPE + QK_ROPE, MLA_PAD).astype(BF16)
    ukv = w_ukv[l].reshape(KV_RANK, HB, QK_NOPE + V_DIM)
    w_ukv_perm = jnp.concatenate([ukv[:, :, :QK_NOPE].reshape(KV_RANK, HB * QK_NOPE),
                                  ukv[:, :, QK_NOPE:].reshape(KV_RANK, HB * V_DIM)], axis=1).astype(BF16)
    return w_x, w_gate, w_uq_pad, w_ukv_perm


def kernel(x, c, positions, norm_g, w_ada, b_ada, w_ffn1_gu, w_ffn1_d, w_ffn2_gu, w_ffn2_d, w_in, b_gate, sink,
           g_cq, g_ckv, w_uq, w_ukv, w_oa, w_ob, w_out, g_final):
    tables = _rope_tables(positions)
    mods = _ada_mod(c, w_ada, b_ada)
    xt = x.reshape(TOKENS, D_MODEL)
    w_gu = w_ffn1_gu[0].astype(BF16)
    for l in range(DEPTH):
        mod = mods[l].reshape(BATCH, 1, N_MOD * D_MODEL)
        g = norm_g[l]
        w_x, w_gate, w_uq_pad, w_ukv_perm = _layer_weights(l, w_in, w_uq, w_ukv)

        act, w_d = _ffn_up(xt, g[0:1], mod, 0, w_gu, w_ffn1_d, l)
        xt, h, w_gu = _ffn_down(act, w_d, xt, mod, 2, g[1:2], 3, "mixer_norm", w_ffn2_gu, l)

        qa, ka, va, q_pad_t, k_pad, v_t = _in_proj(h, tables, w_x, g_cq[l][None, :], g_ckv[l][None, :],
                                                   w_uq_pad, w_ukv_perm)
        o_a = _win_attn(qa, ka, va, sink[l])
        o_b = _mla_attn(q_pad_t, k_pad, v_t)
        xt = _mix_out(xt, h, mod, 5, o_a, o_b, w_oa[l].astype(BF16), w_ob[l].astype(BF16), w_gate,
                      b_gate[l][None, :], w_out[l].astype(BF16))

        act, w_d = _ffn_up(xt, g[2:3], mod, 6, w_gu, w_ffn2_d, l)
        if l + 1 < DEPTH:
            xt, w_gu = _ffn_down(act, w_d, xt, mod, 8, g_final[None, :], 0, "plain", w_ffn1_gu, l + 1)
        else:
            xt, = _ffn_down(act, w_d, xt, mod, 8, g_final[None, :], 0, "final_norm")
    return xt.reshape(BATCH, SEQ, D_MODEL)
```

```python
import functools

import jax
import jax.numpy as jnp
import numpy as np
from jax import lax
from jax.experimental import pallas as pl
from jax.experimental.pallas import tpu as pltpu

D_MODEL = 2048
BATCH = 8
SEQ = 4096
DEPTH = 2
HA_Q = 8
HA_KV = 2
GROUP = HA_Q // HA_KV
HEAD_DIM = 128
WINDOW = 128
BLOCK = 128
HB = 8
QK_NOPE = 128
QK_ROPE = 64
V_DIM = 128
Q_RANK = 512
KV_RANK = 512
D_FF = 5632
ROPE_THETA = 10000.0
EPS = 1e-6
N_MOD = 9
TOKENS = BATCH * SEQ

COLS_QA = HA_Q * HEAD_DIM
COLS_KA = HA_KV * HEAD_DIM
COLS_VA = HA_KV * HEAD_DIM
OFF_CQ = COLS_QA + COLS_KA + COLS_VA
OFF_CKV = OFF_CQ + Q_RANK
OFF_KR = OFF_CKV + KV_RANK
OFF_GATE = OFF_KR + QK_ROPE

LANES = 128
MLA_PAD = 256
N_ROPE_TABLES = 5
VMEM_LIMIT = 56 * 1024 * 1024

BF16 = jnp.bfloat16
F32 = jnp.float32


def _params(sem):
    return pltpu.CompilerParams(dimension_semantics=sem, vmem_limit_bytes=VMEM_LIMIT)


def _dot(a, b):
    return jnp.dot(a, b, preferred_element_type=F32)


def _dot_nt(a, b):
    return lax.dot_general(a, b, (((1,), (1,)), ((), ())), preferred_element_type=F32)


def _norm_mod(x, g, shift, scale):
    y = x * lax.rsqrt(jnp.mean(x * x, axis=-1, keepdims=True) + EPS)
    return (y * g) * (1.0 + scale) + shift


def _rope_table_kernel(pos_ref, freq_ref, o_ref):
    ang = pos_ref[...].astype(F32) * freq_ref[...]
    cs = jnp.cos(ang)
    sn = jnp.sin(ang)
    lane = lax.broadcasted_iota(jnp.int32, cs.shape, 1)
    cs64 = pltpu.roll(cs, 64, 1)
    sn64 = pltpu.roll(sn, 64, 1)
    cs96 = pltpu.roll(cs, 96, 1)
    sn96 = pltpu.roll(sn, 96, 1)
    zero = jnp.zeros_like(cs)
    o_ref[:, 0 * LANES:1 * LANES] = jnp.where(lane < 64, cs, cs64)
    o_ref[:, 1 * LANES:2 * LANES] = jnp.where(lane < 64, -sn, sn64)
    o_ref[:, 2 * LANES:3 * LANES] = jnp.where(lane < 32, cs64, jnp.where(lane < 64, cs96, zero))
    o_ref[:, 3 * LANES:4 * LANES] = jnp.where(lane < 32, -sn64, zero)
    o_ref[:, 4 * LANES:5 * LANES] = jnp.where((lane >= 32) & (lane < 64), sn96, zero)


def _rope_tables(positions):
    tm = 1024
    fa = ROPE_THETA ** (-jnp.arange(0, HEAD_DIM, 2, dtype=F32) / HEAD_DIM)
    fb = ROPE_THETA ** (-jnp.arange(0, QK_ROPE, 2, dtype=F32) / QK_ROPE)
    freq = jnp.concatenate([fa, fb, jnp.zeros((32,), F32)])[None, :]
    pos = positions.reshape(TOKENS, 1)
    return pl.pallas_call(
        _rope_table_kernel,
        grid=(TOKENS // tm,),
        in_specs=[pl.BlockSpec((tm, 1), lambda i: (i, 0)),
                  pl.BlockSpec((1, LANES), lambda i: (0, 0))],
        out_specs=pl.BlockSpec((tm, N_ROPE_TABLES * LANES), lambda i: (i, 0)),
        out_shape=jax.ShapeDtypeStruct((TOKENS, N_ROPE_TABLES * LANES), F32),
        compiler_params=_params(("parallel",)),
        name="rope_tables",
    )(pos, freq)


def _ada_kernel(c_ref, w_ref, b_ref, o_ref):
    c = c_ref[...]
    c_act = (c * jax.nn.sigmoid(c)).astype(BF16)
    o_ref[...] = _dot(c_act, w_ref[...].astype(BF16)) + b_ref[...]


def _ada_mod(c, w_ada, b_ada):
    tn = 1024
    n = N_MOD * D_MODEL
    return pl.pallas_call(
        _ada_kernel,
        grid=(DEPTH, n // tn),
        in_specs=[pl.BlockSpec((BATCH, D_MODEL), lambda l, j: (0, 0)),
                  pl.BlockSpec((None, D_MODEL, tn), lambda l, j: (l, 0, j)),
                  pl.BlockSpec((None, 1, tn), lambda l, j: (l, 0, j))],
        out_specs=pl.BlockSpec((None, BATCH, tn), lambda l, j: (l, 0, j)),
        out_shape=jax.ShapeDtypeStruct((DEPTH, BATCH, n), F32),
        compiler_params=_params(("parallel", "parallel")),
        name="ada_mod",
    )(c, w_ada, b_ada.reshape(DEPTH, 1, n))


def _mod_spec(tm, k):
    return pl.BlockSpec((None, 1, D_MODEL), lambda i, *_: (i * tm // SEQ, 0, k))


FFN_ROW_CHUNK = 256


FFN_UP_TN = 1024
FFN_UP_STEPS = -(-D_FF // FFN_UP_TN)
D_FF_PAD = FFN_UP_STEPS * FFN_UP_TN
FFN_UP_LAST = D_FF - (FFN_UP_STEPS - 1) * FFN_UP_TN


def _gu_segments():
    segs = []
    for j in range(FFN_UP_STEPS):
        w = min(FFN_UP_TN, D_FF - j * FFN_UP_TN)
        segs.append((j * FFN_UP_TN, w, 2 * j * FFN_UP_TN))
        segs.append((D_FF + j * FFN_UP_TN, w, 2 * j * FFN_UP_TN + w))
    return segs


def _store_gu_tiles(src, dst_ref):
    dst_ref[:, 2 * D_FF_PAD - 2 * (FFN_UP_TN - FFN_UP_LAST):] = jnp.zeros(
        (dst_ref.shape[0], 2 * (FFN_UP_TN - FFN_UP_LAST)), BF16)
    for s, w, d in _gu_segments():
        dst_ref[:, d:d + w] = src[:, s:s + w].astype(BF16)


def _gu_tiles(w_gu):
    out = jnp.zeros((w_gu.shape[0], 2 * D_FF_PAD), BF16)
    for s, w, d in _gu_segments():
        out = lax.dynamic_update_slice(out, w_gu[:, s:s + w].astype(BF16), (0, d))
    return out


def _ffn_up_kernel(x_ref, g_ref, sh_ref, sc_ref, w_ref, cast_ref, o_ref, cast_out_ref, h_ref):
    cast_out_ref[...] = cast_ref[...].astype(BF16)
    j = pl.program_id(1)

    def swiglu(h, width):
        gu = _dot(h, w_ref[:, :2 * width])
        g = gu[:, :width]
        return (g * jax.nn.sigmoid(g) * gu[:, width:]).astype(BF16)

    @pl.when(j == 0)
    def _():
        for r in range(0, x_ref.shape[0], FFN_ROW_CHUNK):
            rows = slice(r, r + FFN_ROW_CHUNK)
            h = _norm_mod(x_ref[rows, :], g_ref[...], sh_ref[...], sc_ref[...]).astype(BF16)
            h_ref[rows, :] = h
            o_ref[rows, :] = swiglu(h, FFN_UP_TN)

    @pl.when((j != 0) & (j != FFN_UP_STEPS - 1))
    def _():
        o_ref[...] = swiglu(h_ref[...], FFN_UP_TN)

    @pl.when(j == FFN_UP_STEPS - 1)
    def _():
        o_ref[:, :FFN_UP_LAST] = swiglu(h_ref[...], FFN_UP_LAST)
        o_ref[:, FFN_UP_LAST:] = jnp.zeros((o_ref.shape[0], FFN_UP_TN - FFN_UP_LAST), BF16)


def _ffn_up(x, g, mod, k_shift, w_gu_tiles, w_d_stack, layer):
    tm = 1024
    ni = TOKENS // tm
    cast_cols = 4
    slab = (D_FF // ni, D_MODEL // cast_cols)
    cast_idx = lambda i, j: (i, jnp.minimum(j, cast_cols - 1))
    return pl.pallas_call(
        _ffn_up_kernel,
        grid=(ni, FFN_UP_STEPS),
        in_specs=[pl.BlockSpec((tm, D_MODEL), lambda i, j: (i, 0)),
                  pl.BlockSpec((1, D_MODEL), lambda i, j: (0, 0)),
                  _mod_spec(tm, k_shift), _mod_spec(tm, k_shift + 1),
                  pl.BlockSpec((D_MODEL, 2 * FFN_UP_TN), lambda i, j: (0, j)),
                  pl.BlockSpec((None,) + slab, lambda i, j: (layer,) + cast_idx(i, j))],
        out_specs=[pl.BlockSpec((tm, FFN_UP_TN), lambda i, j: (i, j)), pl.BlockSpec(slab, cast_idx)],
        out_shape=[jax.ShapeDtypeStruct((TOKENS, D_FF_PAD), BF16), jax.ShapeDtypeStruct((D_FF, D_MODEL), BF16)],
        scratch_shapes=[pltpu.VMEM((tm, D_MODEL), BF16)],
        compiler_params=_params(("parallel", "arbitrary")),
        name="ffn_up",
    )(x, g, mod, mod, w_gu_tiles, w_d_stack)


FFN_DOWN_COL_CHUNK = 512


def _ffn_down_kernel(a_ref, w_ref, x_ref, gt_ref, g_ref, sh_ref, sc_ref, *refs, tail, cast):
    if cast:
        cast_ref, *out_refs, cast_out_ref = refs
        _store_gu_tiles(cast_ref, cast_out_ref)
    else:
        out_refs = refs
    x_out = out_refs[0]
    a = a_ref[...]
    for c in range(0, D_MODEL, FFN_DOWN_COL_CHUNK):
        cols = slice(c, c + FFN_DOWN_COL_CHUNK)
        x_out[:, cols] = x_ref[:, cols] + (0.5 * gt_ref[:, cols]) * _dot(a, w_ref[:, cols])
    if tail == "final_norm":
        x_out[...] = _rms(x_out[...], g_ref[...])
    elif tail == "mixer_norm":
        out_refs[1][...] = _norm_mod(x_out[...], g_ref[...], sh_ref[...], sc_ref[...]).astype(BF16)


def _ffn_down(act, w_d, x, mod, k_gate, g_tail, k_shift_tail, tail, w_gu_stack=None, layer=None):
    tm = 256
    ni = TOKENS // tm
    row = lambda i: (i, 0)
    x_spec = pl.BlockSpec((tm, D_MODEL), row)
    in_specs = [pl.BlockSpec((tm, D_FF), row),
                pl.BlockSpec((D_FF, D_MODEL), lambda i: (0, 0), pipeline_mode=pl.Buffered(1)),
                x_spec,
                _mod_spec(tm, k_gate),
                pl.BlockSpec((1, D_MODEL), lambda i: (0, 0)),
                _mod_spec(tm, k_shift_tail), _mod_spec(tm, k_shift_tail + 1)]
    operands = [act, w_d, x, mod, g_tail, mod, mod]
    out_specs = [x_spec]
    out_shape = [jax.ShapeDtypeStruct((TOKENS, D_MODEL), F32)]
    if tail == "mixer_norm":
        out_specs.append(pl.BlockSpec((tm, D_MODEL), row))
        out_shape.append(jax.ShapeDtypeStruct((TOKENS, D_MODEL), BF16))
    cast = w_gu_stack is not None
    if cast:
        slab = D_MODEL // ni
        in_specs.append(pl.BlockSpec((None, slab, 2 * D_FF), lambda i: (layer, i, 0)))
        operands.append(w_gu_stack)
        out_specs.append(pl.BlockSpec((slab, 2 * D_FF_PAD), row))
        out_shape.append(jax.ShapeDtypeStruct((D_MODEL, 2 * D_FF_PAD), BF16))
    return pl.pallas_call(
        functools.partial(_ffn_down_kernel, tail=tail, cast=cast),
        grid=(ni,),
        in_specs=in_specs,
        out_specs=out_specs,
        out_shape=out_shape,
        compiler_params=_params(("parallel",)),
        name="ffn_down",
    )(*operands)


def _rms(x, g):
    return (x * lax.rsqrt(jnp.mean(x * x, axis=-1, keepdims=True) + EPS)) * g


def _in_proj_kernel(h_ref, tab_ref, wx_ref, gcq_ref, gckv_ref, wuq_ref, wukv_ref,
                    qa_ref, ka_ref, va_ref, qp_ref, kp_ref, vb_ref):
    h = h_ref[...]
    cos_a = tab_ref[:, 0 * LANES:1 * LANES]
    sin_a = tab_ref[:, 1 * LANES:2 * LANES]
    cos_b = tab_ref[:, 2 * LANES:3 * LANES]
    sin_lo = tab_ref[:, 3 * LANES:4 * LANES]
    sin_hi = tab_ref[:, 4 * LANES:5 * LANES]

    def rope_a(t):
        return t * cos_a + pltpu.roll(t, 64, 1) * sin_a

    def rope_b(t):
        return t * cos_b + pltpu.roll(t, 96, 1) * sin_lo + pltpu.roll(t, 32, 1) * sin_hi

    proj = _dot(h, wx_ref[...])

    for hd in range(HA_Q):
        sl = slice(hd * HEAD_DIM, (hd + 1) * HEAD_DIM)
        qa_ref[:, sl] = rope_a(proj[:, sl]).astype(BF16)
    for hd in range(HA_KV):
        sl = slice(hd * HEAD_DIM, (hd + 1) * HEAD_DIM)
        ka_ref[:, sl] = rope_a(proj[:, COLS_QA + hd * HEAD_DIM:COLS_QA + (hd + 1) * HEAD_DIM]).astype(BF16)
    va_ref[...] = proj[:, COLS_QA + COLS_KA:OFF_CQ].astype(BF16)

    cq = _rms(proj[:, OFF_CQ:OFF_CKV], gcq_ref[...]).astype(BF16)
    q_all = _dot(cq, wuq_ref[...])
    for hd in range(HB):
        base = hd * MLA_PAD
        qp_ref[base:base + QK_NOPE, :] = q_all[:, base:base + QK_NOPE].T.astype(BF16)
        qp_ref[base + QK_NOPE:base + MLA_PAD, :] = rope_b(q_all[:, base + QK_NOPE:base + MLA_PAD]).T.astype(BF16)

    ckv = _rms(proj[:, OFF_CKV:OFF_KR], gckv_ref[...]).astype(BF16)
    kv_all = _dot(ckv, wukv_ref[...])
    kr = rope_b(proj[:, OFF_KR:OFF_KR + LANES]).astype(BF16)
    for hd in range(HB):
        base = hd * MLA_PAD
        kp_ref[:, base:base + QK_NOPE] = kv_all[:, hd * QK_NOPE:(hd + 1) * QK_NOPE].astype(BF16)
        kp_ref[:, base + QK_NOPE:base + MLA_PAD] = kr
        vsl = slice(HB * QK_NOPE + hd * V_DIM, HB * QK_NOPE + (hd + 1) * V_DIM)
        vb_ref[hd * V_DIM:(hd + 1) * V_DIM, :] = kv_all[:, vsl].T.astype(BF16)


def _in_proj(h, tables, w_x, g_cq, g_ckv, w_uq, w_ukv):
    tm = 512
    const = lambda i: (0, 0)
    row = lambda i: (i, 0)
    nx = w_x.shape[1]
    col = lambda i: (0, i)
    resident = lambda shape: pl.BlockSpec(shape, const, pipeline_mode=pl.Buffered(1))
    outs = [(COLS_QA, True), (COLS_KA, True), (COLS_VA, True), (HB * MLA_PAD, False),
            (HB * MLA_PAD, True), (HB * V_DIM, False)]
    out_specs = [pl.BlockSpec((tm, n), row) if tok else pl.BlockSpec((n, tm), col) for n, tok in outs]
    out_shape = [jax.ShapeDtypeStruct((TOKENS, n) if tok else (n, TOKENS), BF16) for n, tok in outs]
    return pl.pallas_call(
        _in_proj_kernel,
        grid=(TOKENS // tm,),
        in_specs=[pl.BlockSpec((tm, D_MODEL), row),
                  pl.BlockSpec((tm, N_ROPE_TABLES * LANES), row),
                  resident((D_MODEL, nx)),
                  pl.BlockSpec((1, Q_RANK), const),
                  pl.BlockSpec((1, KV_RANK), const),
                  resident((Q_RANK, HB * MLA_PAD)),
                  resident((KV_RANK, HB * (QK_NOPE + V_DIM)))],
        out_specs=out_specs,
        out_shape=out_shape,
        compiler_params=_params(("parallel",)),
        name="in_proj",
    )(h, tables, w_x, g_cq, g_ckv, w_uq, w_ukv)


WIN_BLOCKS_PER_STEP = 4


WIN_KEYS = 3 * BLOCK
WIN_ROWS = GROUP * BLOCK
LOG2E = float(np.log2(np.e))


def _win_mask_bias():
    rel = jnp.arange(WIN_KEYS)[None, :] - (jnp.arange(WIN_ROWS) % BLOCK)[:, None]
    off = -BLOCK * jnp.arange(3)[:, None, None]
    return jnp.where(jnp.abs(rel[None] + off) <= WINDOW, 0.0, -1e30).astype(F32)


def _win_attn_kernel(sink_ref, bias_ref, q_ref, k_ref, v_ref, o_ref):
    hk = pl.program_id(1)
    nb = SEQ // BLOCK
    c = HEAD_DIM ** -0.5 * LOG2E
    row = lax.broadcasted_iota(jnp.int32, (WIN_ROWS, 1), 0)
    sk = jnp.zeros((WIN_ROWS, 1), F32)
    for g in range(GROUP):
        sk = jnp.where(row // BLOCK == g, sink_ref[hk * GROUP + g] * LOG2E, sk)
    for j in range(WIN_BLOCKS_PER_STEP):
        n = pl.program_id(2) * WIN_BLOCKS_PER_STEP + j
        first = jnp.clip(n - 1, 0, nb - 3)
        start = pl.multiple_of(first * BLOCK, BLOCK)
        kw = k_ref[pl.ds(start, WIN_KEYS), :]
        vw = v_ref[pl.ds(start, WIN_KEYS), :]
        q = jnp.concatenate([q_ref[j * BLOCK:(j + 1) * BLOCK, g * HEAD_DIM:(g + 1) * HEAD_DIM]
                             for g in range(GROUP)], axis=0)
        t = _dot_nt(q, kw) * c + bias_ref[n - first]
        m = jnp.maximum(jnp.max(t, axis=-1, keepdims=True), sk)
        p = jnp.exp2(t - m)
        denom = jnp.sum(p, axis=-1, keepdims=True) + jnp.exp2(sk - m)
        o = (_dot(p.astype(BF16), vw) * (1.0 / denom)).astype(BF16)
        for g in range(GROUP):
            o_ref[j * BLOCK:(j + 1) * BLOCK, g * HEAD_DIM:(g + 1) * HEAD_DIM] = o[g * BLOCK:(g + 1) * BLOCK]


def _win_attn(qa, ka, va, sink):
    tq = WIN_BLOCKS_PER_STEP * BLOCK
    nq = SEQ // tq
    gw = GROUP * HEAD_DIM
    return pl.pallas_call(
        _win_attn_kernel,
        grid=(BATCH, HA_KV, nq),
        in_specs=[pl.BlockSpec(memory_space=pltpu.SMEM),
                  pl.BlockSpec((3, WIN_ROWS, WIN_KEYS), lambda b, h, n: (0, 0, 0)),
                  pl.BlockSpec((tq, gw), lambda b, h, n: (b * nq + n, h)),
                  pl.BlockSpec((SEQ, HEAD_DIM), lambda b, h, n: (b, h)),
                  pl.BlockSpec((SEQ, HEAD_DIM), lambda b, h, n: (b, h))],
        out_specs=pl.BlockSpec((tq, gw), lambda b, h, n: (b * nq + n, h)),
        out_shape=jax.ShapeDtypeStruct((TOKENS, COLS_QA), BF16),
        compiler_params=_params(("parallel", "parallel", "arbitrary")),
        name="win_attn",
    )(sink, _win_mask_bias(), qa, ka, va)


MLA_KEY_CHUNK = 512
MLA_ONES_ROWS = 16


def _mla_attn_kernel(qt_ref, k_ref, vt_ref, o_ref):
    c = (QK_NOPE + QK_ROPE) ** -0.5 * LOG2E
    qt = qt_ref[...]
    tq = qt.shape[1]
    m = jnp.full((1, tq), -1e30, F32)
    acc = jnp.zeros((V_DIM + MLA_ONES_ROWS, tq), F32)
    ones = jnp.ones((MLA_ONES_ROWS, MLA_KEY_CHUNK), BF16)
    n_chunks = SEQ // MLA_KEY_CHUNK
    chunk = lambda ci: slice(ci * MLA_KEY_CHUNK, (ci + 1) * MLA_KEY_CHUNK)
    st_next = _dot(k_ref[chunk(0), :], qt)
    for ci in range(n_chunks):
        ks = chunk(ci)
        st = st_next
        if ci + 1 < n_chunks:
            st_next = _dot(k_ref[chunk(ci + 1), :], qt)
        m_new = jnp.maximum(m, jnp.max(st, axis=0, keepdims=True))
        alpha = jnp.exp2((m - m_new) * c)
        p = jnp.exp2((st - m_new) * c).astype(BF16)
        vt1 = jnp.concatenate([vt_ref[:, ks], ones], axis=0)
        acc = acc * alpha + _dot(vt1, p)
        m = m_new
    o_ref[...] = (acc[:V_DIM] * (1.0 / acc[V_DIM:V_DIM + 1])).T.astype(BF16)


def _mla_attn(q_pad_t, k_pad, v_t):
    tq = 2048
    nq = SEQ // tq
    return pl.pallas_call(
        _mla_attn_kernel,
        grid=(BATCH, HB, nq),
        in_specs=[pl.BlockSpec((MLA_PAD, tq), lambda b, h, i: (h, b * nq + i)),
                  pl.BlockSpec((SEQ, MLA_PAD), lambda b, h, i: (b, h)),
                  pl.BlockSpec((V_DIM, SEQ), lambda b, h, i: (h, b))],
        out_specs=pl.BlockSpec((tq, V_DIM), lambda b, h, i: (b * nq + i, h)),
        out_shape=jax.ShapeDtypeStruct((TOKENS, HB * V_DIM), BF16),
        compiler_params=_params(("parallel", "parallel", "arbitrary")),
        name="mla_attn",
    )(q_pad_t, k_pad, v_t)


MIX_COL_CHUNK = 256


def _mix_out_kernel(x_ref, h_ref, gt_ref, oa_ref, ob_ref, woa_ref, wob_ref,
                    wga_ref, wgb_ref, bga_ref, bgb_ref, wout_ref, o_ref, acc_ref):
    n = pl.program_id(1)

    @pl.when(n == 0)
    def _():
        acc_ref[...] = jnp.zeros_like(acc_ref)

    h = h_ref[...]
    o_a = oa_ref[...]
    o_b = ob_ref[...]
    mix = []
    for c in range(0, wout_ref.shape[0], MIX_COL_CHUNK):
        cols = slice(c, c + MIX_COL_CHUNK)
        y_a = _dot(o_a, woa_ref[:, cols])
        y_b = _dot(o_b, wob_ref[:, cols])
        g_a = jax.nn.sigmoid(_dot(h, wga_ref[:, cols]) + bga_ref[:, cols])
        g_b = jax.nn.sigmoid(_dot(h, wgb_ref[:, cols]) + bgb_ref[:, cols])
        mix.append((g_a * y_a + g_b * y_b).astype(BF16))
    acc_ref[...] += _dot(jnp.concatenate(mix, axis=1), wout_ref[...])

    @pl.when(n == pl.num_programs(1) - 1)
    def _():
        o_ref[...] = x_ref[...] + gt_ref[...] * acc_ref[...]


def _mix_out(x, h, mod, k_gate, o_a, o_b, w_oa, w_ob, w_gate, b_gate, w_out):
    tm, tc = 512, 512
    nc = D_MODEL // tc
    row = lambda i, n: (i, 0)
    col = lambda i, n: (0, n)
    col_hi = lambda i, n: (0, n + nc)
    return pl.pallas_call(
        _mix_out_kernel,
        grid=(TOKENS // tm, nc),
        in_specs=[pl.BlockSpec((tm, D_MODEL), row),
                  pl.BlockSpec((tm, D_MODEL), row),
                  _mod_spec(tm, k_gate),
                  pl.BlockSpec((tm, COLS_QA), row),
                  pl.BlockSpec((tm, HB * V_DIM), row),
                  pl.BlockSpec((COLS_QA, tc), col),
                  pl.BlockSpec((HB * V_DIM, tc), col),
                  pl.BlockSpec((D_MODEL, tc), col),
                  pl.BlockSpec((D_MODEL, tc), col_hi),
                  pl.BlockSpec((1, tc), col),
                  pl.BlockSpec((1, tc), col_hi),
                  pl.BlockSpec((tc, D_MODEL), lambda i, n: (n, 0))],
        out_specs=pl.BlockSpec((tm, D_MODEL), row),
        out_shape=jax.ShapeDtypeStruct((TOKENS, D_MODEL), F32),
        scratch_shapes=[pltpu.VMEM((tm, D_MODEL), F32)],
        compiler_params=_params(("parallel", "arbitrary")),
        name="mix_out",
    )(x, h, mod, o_a, o_b, w_oa, w_ob, w_gate, w_gate, b_gate, b_gate, w_out)


def _pad_heads(w, n_heads, width, pad_to):
    k = w.shape[0]
    w = w.reshape(k, n_heads, width)
    w = jnp.pad(w, ((0, 0), (0, 0), (0, pad_to - width)))
    return w.reshape(k, n_heads * pad_to)


def _layer_weights(l, w_in, w_uq, w_ukv):
    w = w_in[l]
    kr_pad = jnp.pad(w[:, OFF_KR:OFF_GATE], ((0, 0), (0, LANES - QK_ROPE)))
    w_x = jnp.concatenate([w[:, :OFF_KR], kr_pad], axis=1).astype(BF16)
    w_gate = w[:, OFF_GATE:].astype(BF16)
    w_uq_pad = _pad_heads(w_uq[l], HB, QK_NOPE + QK_ROPE, MLA_PAD).astype(BF16)
    ukv = w_ukv[l].reshape(KV_RANK, HB, QK_NOPE + V_DIM)
    w_ukv_perm = jnp.concatenate([ukv[:, :, :QK_NOPE].reshape(KV_RANK, HB * QK_NOPE),
                                  ukv[:, :, QK_NOPE:].reshape(KV_RANK, HB * V_DIM)], axis=1).astype(BF16)
    return w_x, w_gate, w_uq_pad, w_ukv_perm


def kernel(x, c, positions, norm_g, w_ada, b_ada, w_ffn1_gu, w_ffn1_d, w_ffn2_gu, w_ffn2_d, w_in, b_gate, sink,
           g_cq, g_ckv, w_uq, w_ukv, w_oa, w_ob, w_out, g_final):
    tables = _rope_tables(positions)
    mods = _ada_mod(c, w_ada, b_ada)
    xt = x.reshape(TOKENS, D_MODEL)
    w_gu = _gu_tiles(w_ffn1_gu[0])
    for l in range(DEPTH):
        mod = mods[l].reshape(BATCH, 1, N_MOD * D_MODEL)
        g = norm_g[l]
        w_x, w_gate, w_uq_pad, w_ukv_perm = _layer_weights(l, w_in, w_uq, w_ukv)

        act, w_d = _ffn_up(xt, g[0:1], mod, 0, w_gu, w_ffn1_d, l)
        xt, h, w_gu = _ffn_down(act, w_d, xt, mod, 2, g[1:2], 3, "mixer_norm", w_ffn2_gu, l)

        qa, ka, va, q_pad_t, k_pad, v_t = _in_proj(h, tables, w_x, g_cq[l][None, :], g_ckv[l][None, :],
                                                   w_uq_pad, w_ukv_perm)
        o_a = _win_attn(qa, ka, va, sink[l])
        o_b = _mla_attn(q_pad_t, k_pad, v_t)
        xt = _mix_out(xt, h, mod, 5, o_a, o_b, w_oa[l].astype(BF16), w_ob[l].astype(BF16), w_gate,
                      b_gate[l][None, :], w_out[l].astype(BF16))

        act, w_d = _ffn_up(xt, g[2:3], mod, 6, w_gu, w_ffn2_d, l)
        if l + 1 < DEPTH:
            xt, w_gu = _ffn_down(act, w_d, xt, mod, 8, g_final[None, :], 0, "plain", w_ffn1_gu, l + 1)
        else:
            xt, = _ffn_down(act, w_d, xt, mod, 8, g_final[None, :], 0, "final_norm")
    return xt.reshape(BATCH, SEQ, D_MODEL)
```

```python
import functools

import jax
import jax.numpy as jnp
import numpy as np
from jax import lax
from jax.experimental import pallas as pl
from jax.experimental.pallas import tpu as pltpu

D_MODEL = 2048
BATCH = 8
SEQ = 4096
DEPTH = 2
HA_Q = 8
HA_KV = 2
GROUP = HA_Q // HA_KV
HEAD_DIM = 128
WINDOW = 128
BLOCK = 128
HB = 8
QK_NOPE = 128
QK_ROPE = 64
V_DIM = 128
Q_RANK = 512
KV_RANK = 512
D_FF = 5632
ROPE_THETA = 10000.0
EPS = 1e-6
N_MOD = 9
TOKENS = BATCH * SEQ

COLS_QA = HA_Q * HEAD_DIM
COLS_KA = HA_KV * HEAD_DIM
COLS_VA = HA_KV * HEAD_DIM
OFF_CQ = COLS_QA + COLS_KA + COLS_VA
OFF_CKV = OFF_CQ + Q_RANK
OFF_KR = OFF_CKV + KV_RANK
OFF_GATE = OFF_KR + QK_ROPE

LANES = 128
MLA_PAD = 256
N_ROPE_TABLES = 5
VMEM_LIMIT = 56 * 1024 * 1024
LOG2E = float(np.log2(np.e))

BF16 = jnp.bfloat16
F32 = jnp.float32


def _params(sem):
    return pltpu.CompilerParams(dimension_semantics=sem, vmem_limit_bytes=VMEM_LIMIT)


def _dot(a, b):
    return jnp.dot(a, b, preferred_element_type=F32)


def _dot_nt(a, b):
    return lax.dot_general(a, b, (((1,), (1,)), ((), ())), preferred_element_type=F32)


def _rms(x, g):
    return (x * lax.rsqrt(jnp.mean(x * x, axis=-1, keepdims=True) + EPS)) * g


def _norm_mod(x, g, shift, scale):
    return _rms(x, g) * (1.0 + scale) + shift


def _rope_table_kernel(pos_ref, freq_ref, o_ref):
    ang = pos_ref[...].astype(F32) * freq_ref[...]
    cs = jnp.cos(ang)
    sn = jnp.sin(ang)
    lane = lax.broadcasted_iota(jnp.int32, cs.shape, 1)
    cs64 = pltpu.roll(cs, 64, 1)
    sn64 = pltpu.roll(sn, 64, 1)
    cs96 = pltpu.roll(cs, 96, 1)
    sn96 = pltpu.roll(sn, 96, 1)
    zero = jnp.zeros_like(cs)
    o_ref[:, 0 * LANES:1 * LANES] = jnp.where(lane < 64, cs, cs64)
    o_ref[:, 1 * LANES:2 * LANES] = jnp.where(lane < 64, -sn, sn64)
    o_ref[:, 2 * LANES:3 * LANES] = jnp.where(lane < 32, cs64, jnp.where(lane < 64, cs96, zero))
    o_ref[:, 3 * LANES:4 * LANES] = jnp.where(lane < 32, -sn64, zero)
    o_ref[:, 4 * LANES:5 * LANES] = jnp.where((lane >= 32) & (lane < 64), sn96, zero)


def _rope_tables(positions):
    tm = 1024
    fa = ROPE_THETA ** (-jnp.arange(0, HEAD_DIM, 2, dtype=F32) / HEAD_DIM)
    fb = ROPE_THETA ** (-jnp.arange(0, QK_ROPE, 2, dtype=F32) / QK_ROPE)
    freq = jnp.concatenate([fa, fb, jnp.zeros((32,), F32)])[None, :]
    pos = positions.reshape(TOKENS, 1)
    return pl.pallas_call(
        _rope_table_kernel,
        grid=(TOKENS // tm,),
        in_specs=[pl.BlockSpec((tm, 1), lambda i: (i, 0)),
                  pl.BlockSpec((1, LANES), lambda i: (0, 0))],
        out_specs=pl.BlockSpec((tm, N_ROPE_TABLES * LANES), lambda i: (i, 0)),
        out_shape=jax.ShapeDtypeStruct((TOKENS, N_ROPE_TABLES * LANES), F32),
        compiler_params=_params(("parallel",)),
        name="rope_tables",
    )(pos, freq)


def _ada_kernel(c_ref, w_ref, b_ref, o_ref):
    c = c_ref[...]
    c_act = (c * jax.nn.sigmoid(c)).astype(BF16)
    o_ref[...] = _dot(c_act, w_ref[...].astype(BF16)) + b_ref[...]


def _ada_mod(c, w_ada, b_ada):
    tn = 1024
    n = N_MOD * D_MODEL
    return pl.pallas_call(
        _ada_kernel,
        grid=(DEPTH, n // tn),
        in_specs=[pl.BlockSpec((BATCH, D_MODEL), lambda l, j: (0, 0)),
                  pl.BlockSpec((None, D_MODEL, tn), lambda l, j: (l, 0, j)),
                  pl.BlockSpec((None, 1, tn), lambda l, j: (l, 0, j))],
        out_specs=pl.BlockSpec((None, BATCH, tn), lambda l, j: (l, 0, j)),
        out_shape=jax.ShapeDtypeStruct((DEPTH, BATCH, n), F32),
        compiler_params=_params(("parallel", "parallel")),
        name="ada_mod",
    )(c, w_ada, b_ada.reshape(DEPTH, 1, n))


def _mod_spec(tm, k):
    return pl.BlockSpec((None, 1, D_MODEL), lambda i, *_: (i * tm // SEQ, 0, k))


FFN_ROW_CHUNK = 256


def _cast_specs(w_stack, layer, n_steps, step_index):
    _, rows, cols = w_stack.shape
    assert rows % n_steps == 0
    slab = rows // n_steps
    in_spec = pl.BlockSpec((None, slab, cols), lambda *ids: (layer, step_index(*ids), 0))
    out_spec = pl.BlockSpec((slab, cols), lambda *ids: (step_index(*ids), 0))
    return in_spec, out_spec, jax.ShapeDtypeStruct((rows, cols), BF16)


def _ffn_up_kernel(x_ref, g_ref, sh_ref, sc_ref, wg_ref, wu_ref, cast_ref, o_ref, cast_out_ref, h_ref):
    cast_out_ref[...] = cast_ref[...].astype(BF16)

    def swiglu(h):
        g = _dot(h, wg_ref[...])
        u = _dot(h, wu_ref[...])
        return (g * jax.nn.sigmoid(g) * u).astype(BF16)

    @pl.when(pl.program_id(1) == 0)
    def _():
        for r in range(0, x_ref.shape[0], FFN_ROW_CHUNK):
            rows = slice(r, r + FFN_ROW_CHUNK)
            h = _norm_mod(x_ref[rows, :], g_ref[...], sh_ref[...], sc_ref[...]).astype(BF16)
            h_ref[rows, :] = h
            o_ref[rows, :] = swiglu(h)

    @pl.when(pl.program_id(1) != 0)
    def _():
        o_ref[...] = swiglu(h_ref[...])


def _ffn_up(x, g, mod, k_shift, w_gu, w_d_stack, layer):
    tm, tn = 1024, 512
    nj = D_FF // tn
    ni = TOKENS // tm
    cast_in, cast_out, cast_shape = _cast_specs(w_d_stack, layer, ni * nj, lambda i, j: i * nj + j)
    return pl.pallas_call(
        _ffn_up_kernel,
        grid=(ni, nj),
        in_specs=[pl.BlockSpec((tm, D_MODEL), lambda i, j: (i, 0)),
                  pl.BlockSpec((1, D_MODEL), lambda i, j: (0, 0)),
                  _mod_spec(tm, k_shift), _mod_spec(tm, k_shift + 1),
                  pl.BlockSpec((D_MODEL, tn), lambda i, j: (0, j)),
                  pl.BlockSpec((D_MODEL, tn), lambda i, j: (0, j + nj)),
                  cast_in],
        out_specs=[pl.BlockSpec((tm, tn), lambda i, j: (i, j)), cast_out],
        out_shape=[jax.ShapeDtypeStruct((TOKENS, D_FF), BF16), cast_shape],
        scratch_shapes=[pltpu.VMEM((tm, D_MODEL), BF16)],
        compiler_params=_params(("parallel", "arbitrary")),
        name="ffn_up",
    )(x, g, mod, mod, w_gu, w_gu, w_d_stack)


FFN_DOWN_COL_CHUNK = 512


def _ffn_down_kernel(a_ref, w_ref, x_ref, gt_ref, g_ref, sh_ref, sc_ref, *refs, tail, cast):
    if cast:
        cast_ref, *out_refs, cast_out_ref = refs
        cast_out_ref[...] = cast_ref[...].astype(BF16)
    else:
        out_refs = refs
    x_out = out_refs[0]
    a = a_ref[...]
    for c in range(0, D_MODEL, FFN_DOWN_COL_CHUNK):
        cols = slice(c, c + FFN_DOWN_COL_CHUNK)
        x_out[:, cols] = x_ref[:, cols] + (0.5 * gt_ref[:, cols]) * _dot(a, w_ref[:, cols])
    if tail == "final_norm":
        x_out[...] = _rms(x_out[...], g_ref[...])
    elif tail == "mixer_norm":
        out_refs[1][...] = _norm_mod(x_out[...], g_ref[...], sh_ref[...], sc_ref[...]).astype(BF16)


def _ffn_down(act, w_d, x, mod, k_gate, g_tail, k_shift_tail, tail, w_gu_stack=None, layer=None):
    tm = 256
    ni = TOKENS // tm
    row = lambda i: (i, 0)
    x_spec = pl.BlockSpec((tm, D_MODEL), row)
    in_specs = [pl.BlockSpec((tm, D_FF), row),
                pl.BlockSpec((D_FF, D_MODEL), lambda i: (0, 0), pipeline_mode=pl.Buffered(1)),
                x_spec,
                _mod_spec(tm, k_gate),
                pl.BlockSpec((1, D_MODEL), lambda i: (0, 0)),
                _mod_spec(tm, k_shift_tail), _mod_spec(tm, k_shift_tail + 1)]
    operands = [act, w_d, x, mod, g_tail, mod, mod]
    out_specs = [x_spec]
    out_shape = [jax.ShapeDtypeStruct((TOKENS, D_MODEL), F32)]
    if tail == "mixer_norm":
        out_specs.append(pl.BlockSpec((tm, D_MODEL), row))
        out_shape.append(jax.ShapeDtypeStruct((TOKENS, D_MODEL), BF16))
    cast = w_gu_stack is not None
    if cast:
        cast_in, cast_out, cast_shape = _cast_specs(w_gu_stack, layer, ni, lambda i: i)
        in_specs.append(cast_in)
        operands.append(w_gu_stack)
        out_specs.append(cast_out)
        out_shape.append(cast_shape)
    return pl.pallas_call(
        functools.partial(_ffn_down_kernel, tail=tail, cast=cast),
        grid=(ni,),
        in_specs=in_specs,
        out_specs=out_specs,
        out_shape=out_shape,
        compiler_params=_params(("parallel",)),
        name="ffn_down",
    )(*operands)


def _in_proj_kernel(h_ref, tab_ref, wx_ref, gcq_ref, gckv_ref, wuq_ref, wukv_ref,
                    qa_ref, ka_ref, va_ref, qp_ref, kp_ref, vb_ref):
    h = h_ref[...]
    cos_a = tab_ref[:, 0 * LANES:1 * LANES]
    sin_a = tab_ref[:, 1 * LANES:2 * LANES]
    cos_b = tab_ref[:, 2 * LANES:3 * LANES]
    sin_lo = tab_ref[:, 3 * LANES:4 * LANES]
    sin_hi = tab_ref[:, 4 * LANES:5 * LANES]

    def rope_a(t):
        return t * cos_a + pltpu.roll(t, 64, 1) * sin_a

    def rope_b(t):
        return t * cos_b + pltpu.roll(t, 96, 1) * sin_lo + pltpu.roll(t, 32, 1) * sin_hi

    proj = _dot(h, wx_ref[...])

    for hd in range(HA_Q):
        sl = slice(hd * HEAD_DIM, (hd + 1) * HEAD_DIM)
        qa_ref[:, sl] = rope_a(proj[:, sl]).astype(BF16)
    for hd in range(HA_KV):
        sl = slice(hd * HEAD_DIM, (hd + 1) * HEAD_DIM)
        ka_ref[:, sl] = rope_a(proj[:, COLS_QA + hd * HEAD_DIM:COLS_QA + (hd + 1) * HEAD_DIM]).astype(BF16)
    va_ref[...] = proj[:, COLS_QA + COLS_KA:OFF_CQ].astype(BF16)

    cq = _rms(proj[:, OFF_CQ:OFF_CKV], gcq_ref[...]).astype(BF16)
    q_all = _dot(cq, wuq_ref[...])
    for hd in range(HB):
        base = hd * MLA_PAD
        qp_ref[base:base + QK_NOPE, :] = q_all[:, base:base + QK_NOPE].T.astype(BF16)
        qp_ref[base + QK_NOPE:base + MLA_PAD, :] = rope_b(q_all[:, base + QK_NOPE:base + MLA_PAD]).T.astype(BF16)

    ckv = _rms(proj[:, OFF_CKV:OFF_KR], gckv_ref[...]).astype(BF16)
    kv_all = _dot(ckv, wukv_ref[...])
    kr = rope_b(proj[:, OFF_KR:OFF_KR + LANES]).astype(BF16)
    for hd in range(HB):
        base = hd * MLA_PAD
        kp_ref[:, base:base + QK_NOPE] = kv_all[:, hd * QK_NOPE:(hd + 1) * QK_NOPE].astype(BF16)
        kp_ref[:, base + QK_NOPE:base + MLA_PAD] = kr
        vsl = slice(HB * QK_NOPE + hd * V_DIM, HB * QK_NOPE + (hd + 1) * V_DIM)
        vb_ref[hd * V_DIM:(hd + 1) * V_DIM, :] = kv_all[:, vsl].T.astype(BF16)


def _in_proj(h, tables, w_x, g_cq, g_ckv, w_uq, w_ukv):
    tm = 512
    const = lambda i: (0, 0)
    row = lambda i: (i, 0)
    nx = w_x.shape[1]
    col = lambda i: (0, i)
    resident = lambda shape: pl.BlockSpec(shape, const, pipeline_mode=pl.Buffered(1))
    outs = [(COLS_QA, True), (COLS_KA, True), (COLS_VA, True), (HB * MLA_PAD, False),
            (HB * MLA_PAD, True), (HB * V_DIM, False)]
    out_specs = [pl.BlockSpec((tm, n), row) if tok else pl.BlockSpec((n, tm), col) for n, tok in outs]
    out_shape = [jax.ShapeDtypeStruct((TOKENS, n) if tok else (n, TOKENS), BF16) for n, tok in outs]
    return pl.pallas_call(
        _in_proj_kernel,
        grid=(TOKENS // tm,),
        in_specs=[pl.BlockSpec((tm, D_MODEL), row),
                  pl.BlockSpec((tm, N_ROPE_TABLES * LANES), row),
                  resident((D_MODEL, nx)),
                  pl.BlockSpec((1, Q_RANK), const),
                  pl.BlockSpec((1, KV_RANK), const),
                  resident((Q_RANK, HB * MLA_PAD)),
                  resident((KV_RANK, HB * (QK_NOPE + V_DIM)))],
        out_specs=out_specs,
        out_shape=out_shape,
        compiler_params=_params(("parallel",)),
        name="in_proj",
    )(h, tables, w_x, g_cq, g_ckv, w_uq, w_ukv)


WIN_BLOCKS_PER_STEP = 4
WIN_KEYS = 3 * BLOCK
WIN_ROWS = GROUP * BLOCK


def _win_mask_bias():
    rel = jnp.arange(WIN_KEYS)[None, :] - (jnp.arange(WIN_ROWS) % BLOCK)[:, None]
    off = -BLOCK * jnp.arange(3)[:, None, None]
    return jnp.where(jnp.abs(rel[None] + off) <= WINDOW, 0.0, -1e30).astype(F32)


def _win_attn_kernel(sink_ref, bias_ref, q_ref, k_ref, v_ref, o_ref):
    hk = pl.program_id(1)
    nb = SEQ // BLOCK
    c = HEAD_DIM ** -0.5 * LOG2E
    row = lax.broadcasted_iota(jnp.int32, (WIN_ROWS, 1), 0)
    sk = jnp.zeros((WIN_ROWS, 1), F32)
    for g in range(GROUP):
        sk = jnp.where(row // BLOCK == g, sink_ref[hk * GROUP + g] * LOG2E, sk)
    for j in range(WIN_BLOCKS_PER_STEP):
        n = pl.program_id(2) * WIN_BLOCKS_PER_STEP + j
        first = jnp.clip(n - 1, 0, nb - 3)
        start = pl.multiple_of(first * BLOCK, BLOCK)
        kw = k_ref[pl.ds(start, WIN_KEYS), :]
        vw = v_ref[pl.ds(start, WIN_KEYS), :]
        q = jnp.concatenate([q_ref[j * BLOCK:(j + 1) * BLOCK, g * HEAD_DIM:(g + 1) * HEAD_DIM]
                             for g in range(GROUP)], axis=0)
        t = _dot_nt(q, kw) * c + bias_ref[n - first]
        m = jnp.maximum(jnp.max(t, axis=-1, keepdims=True), sk)
        p = jnp.exp2(t - m)
        denom = jnp.sum(p, axis=-1, keepdims=True) + jnp.exp2(sk - m)
        o = (_dot(p.astype(BF16), vw) * (1.0 / denom)).astype(BF16)
        for g in range(GROUP):
            o_ref[j * BLOCK:(j + 1) * BLOCK, g * HEAD_DIM:(g + 1) * HEAD_DIM] = o[g * BLOCK:(g + 1) * BLOCK]


def _win_attn(qa, ka, va, sink):
    tq = WIN_BLOCKS_PER_STEP * BLOCK
    nq = SEQ // tq
    gw = GROUP * HEAD_DIM
    return pl.pallas_call(
        _win_attn_kernel,
        grid=(BATCH, HA_KV, nq),
        in_specs=[pl.BlockSpec(memory_space=pltpu.SMEM),
                  pl.BlockSpec((3, WIN_ROWS, WIN_KEYS), lambda b, h, n: (0, 0, 0)),
                  pl.BlockSpec((tq, gw), lambda b, h, n: (b * nq + n, h)),
                  pl.BlockSpec((SEQ, HEAD_DIM), lambda b, h, n: (b, h)),
                  pl.BlockSpec((SEQ, HEAD_DIM), lambda b, h, n: (b, h))],
        out_specs=pl.BlockSpec((tq, gw), lambda b, h, n: (b * nq + n, h)),
        out_shape=jax.ShapeDtypeStruct((TOKENS, COLS_QA), BF16),
        compiler_params=_params(("parallel", "parallel", "arbitrary")),
        name="win_attn",
    )(sink, _win_mask_bias(), qa, ka, va)


MLA_KEY_CHUNK = 512
MLA_ONES_ROWS = 16


def _mla_attn_kernel(qt_ref, k_ref, vt_ref, o_ref):
    c = (QK_NOPE + QK_ROPE) ** -0.5 * LOG2E
    qt = qt_ref[...]
    tq = qt.shape[1]
    m = jnp.full((1, tq), -1e30, F32)
    acc = jnp.zeros((V_DIM + MLA_ONES_ROWS, tq), F32)
    ones = jnp.ones((MLA_ONES_ROWS, MLA_KEY_CHUNK), BF16)
    n_chunks = SEQ // MLA_KEY_CHUNK
    chunk = lambda ci: slice(ci * MLA_KEY_CHUNK, (ci + 1) * MLA_KEY_CHUNK)
    st_next = _dot(k_ref[chunk(0), :], qt)
    for ci in range(n_chunks):
        ks = chunk(ci)
        st = st_next
        if ci + 1 < n_chunks:
            st_next = _dot(k_ref[chunk(ci + 1), :], qt)
        m_new = jnp.maximum(m, jnp.max(st, axis=0, keepdims=True))
        alpha = jnp.exp2((m - m_new) * c)
        p = jnp.exp2((st - m_new) * c).astype(BF16)
        vt1 = jnp.concatenate([vt_ref[:, ks], ones], axis=0)
        acc = acc * alpha + _dot(vt1, p)
        m = m_new
    o_ref[...] = (acc[:V_DIM] * (1.0 / acc[V_DIM:V_DIM + 1])).T.astype(BF16)


def _mla_attn(q_pad_t, k_pad, v_t):
    tq = 2048
    nq = SEQ // tq
    return pl.pallas_call(
        _mla_attn_kernel,
        grid=(BATCH, HB, nq),
        in_specs=[pl.BlockSpec((MLA_PAD, tq), lambda b, h, i: (h, b * nq + i)),
                  pl.BlockSpec((SEQ, MLA_PAD), lambda b, h, i: (b, h)),
                  pl.BlockSpec((V_DIM, SEQ), lambda b, h, i: (h, b))],
        out_specs=pl.BlockSpec((tq, V_DIM), lambda b, h, i: (b * nq + i, h)),
        out_shape=jax.ShapeDtypeStruct((TOKENS, HB * V_DIM), BF16),
        compiler_params=_params(("parallel", "parallel", "arbitrary")),
        name="mla_attn",
    )(q_pad_t, k_pad, v_t)


MIX_COL_CHUNK = 256


def _mix_out_kernel(x_ref, h_ref, gt_ref, oa_ref, ob_ref, woa_ref, wob_ref,
                    wga_ref, wgb_ref, bga_ref, bgb_ref, wout_ref, o_ref, acc_ref, mix_ref):
    n = pl.program_id(1)
    last = pl.num_programs(1) - 1

    def merge():
        h = h_ref[...]
        o_a = oa_ref[...]
        o_b = ob_ref[...]
        mix = []
        for c in range(0, wout_ref.shape[0], MIX_COL_CHUNK):
            cols = slice(c, c + MIX_COL_CHUNK)
            y_a = _dot(o_a, woa_ref[:, cols])
            y_b = _dot(o_b, wob_ref[:, cols])
            g_a = jax.nn.sigmoid(_dot(h, wga_ref[:, cols]) + bga_ref[:, cols])
            g_b = jax.nn.sigmoid(_dot(h, wgb_ref[:, cols]) + bgb_ref[:, cols])
            mix.append((g_a * y_a + g_b * y_b).astype(BF16))
        return jnp.concatenate(mix, axis=1)

    @pl.when(n == 0)
    def _():
        acc_ref[...] = jnp.zeros_like(acc_ref)
        mix_ref[...] = merge()

    @pl.when((n != 0) & (n != last))
    def _():
        prev = mix_ref[...]
        acc_ref[...] += _dot(prev, wout_ref[...])
        mix_ref[...] = merge()

    @pl.when(n == last)
    def _():
        y = acc_ref[...] + _dot(mix_ref[...], wout_ref[...])
        o_ref[...] = x_ref[...] + gt_ref[...] * y


def _mix_out(x, h, mod, k_gate, o_a, o_b, w_oa, w_ob, w_gate, b_gate, w_out):
    tm, tc = 512, 512
    nc = D_MODEL // tc
    row = lambda i, n: (i, 0)
    col = lambda i, n: (0, jnp.minimum(n, nc - 1))
    col_hi = lambda i, n: (0, jnp.minimum(n, nc - 1) + nc)
    return pl.pallas_call(
        _mix_out_kernel,
        grid=(TOKENS // tm, nc + 1),
        in_specs=[pl.BlockSpec((tm, D_MODEL), row),
                  pl.BlockSpec((tm, D_MODEL), row),
                  _mod_spec(tm, k_gate),
                  pl.BlockSpec((tm, COLS_QA), row),
                  pl.BlockSpec((tm, HB * V_DIM), row),
                  pl.BlockSpec((COLS_QA, tc), col),
                  pl.BlockSpec((HB * V_DIM, tc), col),
                  pl.BlockSpec((D_MODEL, tc), col),
                  pl.BlockSpec((D_MODEL, tc), col_hi),
                  pl.BlockSpec((1, tc), col),
                  pl.BlockSpec((1, tc), col_hi),
                  pl.BlockSpec((tc, D_MODEL), lambda i, n: (jnp.maximum(n - 1, 0), 0))],
        out_specs=pl.BlockSpec((tm, D_MODEL), row),
        out_shape=jax.ShapeDtypeStruct((TOKENS, D_MODEL), F32),
        scratch_shapes=[pltpu.VMEM((tm, D_MODEL), F32), pltpu.VMEM((tm, tc), BF16)],
        compiler_params=_params(("parallel", "arbitrary")),
        name="mix_out",
    )(x, h, mod, o_a, o_b, w_oa, w_ob, w_gate, w_gate, b_gate, b_gate, w_out)


def _pad_heads(w, n_heads, width, pad_to):
    k = w.shape[0]
    w = w.reshape(k, n_heads, width)
    w = jnp.pad(w, ((0, 0), (0, 0), (0, pad_to - width)))
    return w.reshape(k, n_heads * pad_to)


def _layer_weights(l, w_in, w_uq, w_ukv):
    w = w_in[l]
    kr_pad = jnp.pad(w[:, OFF_KR:OFF_GATE], ((0, 0), (0, LANES - QK_ROPE)))
    w_x = jnp.concatenate([w[:, :OFF_KR], kr_pad], axis=1).astype(BF16)
    w_gate = w[:, OFF_GATE:].astype(BF16)
    w_uq_pad = _pad_heads(w_uq[l], HB, QK_NOPE + QK_ROPE, MLA_PAD).astype(BF16)
    ukv = w_ukv[l].reshape(KV_RANK, HB, QK_NOPE + V_DIM)
    w_ukv_perm = jnp.concatenate([ukv[:, :, :QK_NOPE].reshape(KV_RANK, HB * QK_NOPE),
                                  ukv[:, :, QK_NOPE:].reshape(KV_RANK, HB * V_DIM)], axis=1).astype(BF16)
    return w_x, w_gate, w_uq_pad, w_ukv_perm


def kernel(x, c, positions, norm_g, w_ada, b_ada, w_ffn1_gu, w_ffn1_d, w_ffn2_gu, w_ffn2_d, w_in, b_gate, sink,
           g_cq, g_ckv, w_uq, w_ukv, w_oa, w_ob, w_out, g_final):
    tables = _rope_tables(positions)
    mods = _ada_mod(c, w_ada, b_ada)
    xt = x.reshape(TOKENS, D_MODEL)
    w_gu = w_ffn1_gu[0].astype(BF16)
    for l in range(DEPTH):
        mod = mods[l].reshape(BATCH, 1, N_MOD * D_MODEL)
        g = norm_g[l]
        w_x, w_gate, w_uq_pad, w_ukv_perm = _layer_weights(l, w_in, w_uq, w_ukv)

        act, w_d = _ffn_up(xt, g[0:1], mod, 0, w_gu, w_ffn1_d, l)
        xt, h, w_gu = _ffn_down(act, w_d, xt, mod, 2, g[1:2], 3, "mixer_norm", w_ffn2_gu, l)

        qa, ka, va, q_pad_t, k_pad, v_t = _in_proj(h, tables, w_x, g_cq[l][None, :], g_ckv[l][None, :],
                                                   w_uq_pad, w_ukv_perm)
        o_a = _win_attn(qa, ka, va, sink[l])
        o_b = _mla_attn(q_pad_t, k_pad, v_t)
        xt = _mix_out(xt, h, mod, 5, o_a, o_b, w_oa[l].astype(BF16), w_ob[l].astype(BF16), w_gate,
                      b_gate[l][None, :], w_out[l].astype(BF16))

        act, w_d = _ffn_up(xt, g[2:3], mod, 6, w_gu, w_ffn2_d, l)
        if l + 1 < DEPTH:
            xt, w_gu = _ffn_down(act, w_d, xt, mod, 8, g_final[None, :], 0, "plain", w_ffn1_gu, l + 1)
        else:
            xt, = _ffn_down(act, w_d, xt, mod, 8, g_final[None, :], 0, "final_norm")
    return xt.reshape(BATCH, SEQ, D_MODEL)
```

```python
import functools

import jax
import jax.numpy as jnp
import numpy as np
from jax import lax
from jax.experimental import pallas as pl
from jax.experimental.pallas import tpu as pltpu

D_MODEL = 2048
BATCH = 8
SEQ = 4096
DEPTH = 2
HA_Q = 8
HA_KV = 2
GROUP = HA_Q // HA_KV
HEAD_DIM = 128
WINDOW = 128
BLOCK = 128
HB = 8
QK_NOPE = 128
QK_ROPE = 64
V_DIM = 128
Q_RANK = 512
KV_RANK = 512
D_FF = 5632
ROPE_THETA = 10000.0
EPS = 1e-6
N_MOD = 9
TOKENS = BATCH * SEQ

COLS_QA = HA_Q * HEAD_DIM
COLS_KA = HA_KV * HEAD_DIM
COLS_VA = HA_KV * HEAD_DIM
OFF_CQ = COLS_QA + COLS_KA + COLS_VA
OFF_CKV = OFF_CQ + Q_RANK
OFF_KR = OFF_CKV + KV_RANK
OFF_GATE = OFF_KR + QK_ROPE

LANES = 128
MLA_PAD = 256
N_ROPE_TABLES = 5
VMEM_LIMIT = 56 * 1024 * 1024
LOG2E = float(np.log2(np.e))

BF16 = jnp.bfloat16
F32 = jnp.float32


def _params(sem):
    return pltpu.CompilerParams(dimension_semantics=sem, vmem_limit_bytes=VMEM_LIMIT)


def _dot(a, b):
    return jnp.dot(a, b, preferred_element_type=F32)


def _dot_nt(a, b):
    return lax.dot_general(a, b, (((1,), (1,)), ((), ())), preferred_element_type=F32)


def _rms(x, g):
    return (x * lax.rsqrt(jnp.mean(x * x, axis=-1, keepdims=True) + EPS)) * g


def _norm_mod(x, g, shift, scale):
    return _rms(x, g) * (1.0 + scale) + shift


def _rope_table_kernel(pos_ref, freq_ref, o_ref):
    ang = pos_ref[...].astype(F32) * freq_ref[...]
    cs = jnp.cos(ang)
    sn = jnp.sin(ang)
    lane = lax.broadcasted_iota(jnp.int32, cs.shape, 1)
    cs64 = pltpu.roll(cs, 64, 1)
    sn64 = pltpu.roll(sn, 64, 1)
    cs96 = pltpu.roll(cs, 96, 1)
    sn96 = pltpu.roll(sn, 96, 1)
    zero = jnp.zeros_like(cs)
    o_ref[:, 0 * LANES:1 * LANES] = jnp.where(lane < 64, cs, cs64)
    o_ref[:, 1 * LANES:2 * LANES] = jnp.where(lane < 64, -sn, sn64)
    o_ref[:, 2 * LANES:3 * LANES] = jnp.where(lane < 32, cs64, jnp.where(lane < 64, cs96, zero))
    o_ref[:, 3 * LANES:4 * LANES] = jnp.where(lane < 32, -sn64, zero)
    o_ref[:, 4 * LANES:5 * LANES] = jnp.where((lane >= 32) & (lane < 64), sn96, zero)


def _rope_tables(positions):
    tm = 1024
    fa = ROPE_THETA ** (-jnp.arange(0, HEAD_DIM, 2, dtype=F32) / HEAD_DIM)
    fb = ROPE_THETA ** (-jnp.arange(0, QK_ROPE, 2, dtype=F32) / QK_ROPE)
    freq = jnp.concatenate([fa, fb, jnp.zeros((32,), F32)])[None, :]
    pos = positions.reshape(TOKENS, 1)
    return pl.pallas_call(
        _rope_table_kernel,
        grid=(TOKENS // tm,),
        in_specs=[pl.BlockSpec((tm, 1), lambda i: (i, 0)),
                  pl.BlockSpec((1, LANES), lambda i: (0, 0))],
        out_specs=pl.BlockSpec((tm, N_ROPE_TABLES * LANES), lambda i: (i, 0)),
        out_shape=jax.ShapeDtypeStruct((TOKENS, N_ROPE_TABLES * LANES), F32),
        compiler_params=_params(("parallel",)),
        name="rope_tables",
    )(pos, freq)


def _ada_kernel(c_ref, w_ref, b_ref, o_ref):
    c = c_ref[...]
    c_act = (c * jax.nn.sigmoid(c)).astype(BF16)
    o_ref[...] = _dot(c_act, w_ref[...].astype(BF16)) + b_ref[...]


def _ada_mod(c, w_ada, b_ada):
    tn = 1024
    n = N_MOD * D_MODEL
    return pl.pallas_call(
        _ada_kernel,
        grid=(DEPTH, n // tn),
        in_specs=[pl.BlockSpec((BATCH, D_MODEL), lambda l, j: (0, 0)),
                  pl.BlockSpec((None, D_MODEL, tn), lambda l, j: (l, 0, j)),
                  pl.BlockSpec((None, 1, tn), lambda l, j: (l, 0, j))],
        out_specs=pl.BlockSpec((None, BATCH, tn), lambda l, j: (l, 0, j)),
        out_shape=jax.ShapeDtypeStruct((DEPTH, BATCH, n), F32),
        compiler_params=_params(("parallel", "parallel")),
        name="ada_mod",
    )(c, w_ada, b_ada.reshape(DEPTH, 1, n))


def _mod_spec(tm, k):
    return pl.BlockSpec((None, 1, D_MODEL), lambda i, *_: (i * tm // SEQ, 0, k))


FFN_ROW_CHUNK = 256


def _cast_specs(w_stack, layer, n_steps, step_index):
    _, rows, cols = w_stack.shape
    assert rows % n_steps == 0
    slab = rows // n_steps
    in_spec = pl.BlockSpec((None, slab, cols), lambda *ids: (layer, step_index(*ids), 0))
    out_spec = pl.BlockSpec((slab, cols), lambda *ids: (step_index(*ids), 0))
    return in_spec, out_spec, jax.ShapeDtypeStruct((rows, cols), BF16)


def _ffn_up_kernel(x_ref, g_ref, sh_ref, sc_ref, wg_ref, wu_ref, cast_ref, o_ref, cast_out_ref, h_ref):
    cast_out_ref[...] = cast_ref[...].astype(BF16)

    def swiglu(h):
        g = _dot(h, wg_ref[...])
        u = _dot(h, wu_ref[...])
        return (g * jax.nn.sigmoid(g) * u).astype(BF16)

    @pl.when(pl.program_id(1) == 0)
    def _():
        for r in range(0, x_ref.shape[0], FFN_ROW_CHUNK):
            rows = slice(r, r + FFN_ROW_CHUNK)
            h = _norm_mod(x_ref[rows, :], g_ref[...], sh_ref[...], sc_ref[...]).astype(BF16)
            h_ref[rows, :] = h
            o_ref[rows, :] = swiglu(h)

    @pl.when(pl.program_id(1) != 0)
    def _():
        o_ref[...] = swiglu(h_ref[...])


FFN_UP_TN = 512
GU_TILES = 2 * D_FF // FFN_UP_TN


def _col_tiles(w, tn):
    k, n = w.shape
    return w.reshape(k, n // tn, tn).transpose(1, 0, 2)


def _ffn_up(x, g, mod, k_shift, w_gu_tiles, w_d_stack, layer):
    tm, tn = 1024, FFN_UP_TN
    nj = D_FF // tn
    ni = TOKENS // tm
    cast_in, cast_out, cast_shape = _cast_specs(w_d_stack, layer, ni * nj, lambda i, j: i * nj + j)
    return pl.pallas_call(
        _ffn_up_kernel,
        grid=(ni, nj),
        in_specs=[pl.BlockSpec((tm, D_MODEL), lambda i, j: (i, 0)),
                  pl.BlockSpec((1, D_MODEL), lambda i, j: (0, 0)),
                  _mod_spec(tm, k_shift), _mod_spec(tm, k_shift + 1),
                  pl.BlockSpec((None, D_MODEL, tn), lambda i, j: (j, 0, 0)),
                  pl.BlockSpec((None, D_MODEL, tn), lambda i, j: (j + nj, 0, 0)),
                  cast_in],
        out_specs=[pl.BlockSpec((tm, tn), lambda i, j: (i, j)), cast_out],
        out_shape=[jax.ShapeDtypeStruct((TOKENS, D_FF), BF16), cast_shape],
        scratch_shapes=[pltpu.VMEM((tm, D_MODEL), BF16)],
        compiler_params=_params(("parallel", "arbitrary")),
        name="ffn_up",
    )(x, g, mod, mod, w_gu_tiles, w_gu_tiles, w_d_stack)


FFN_DOWN_COL_CHUNK = 512


def _ffn_down_kernel(a_ref, w_ref, x_ref, gt_ref, g_ref, sh_ref, sc_ref, *refs, tail, cast):
    if cast:
        cast_ref, *out_refs, cast_out_ref = refs
        for t in range(GU_TILES):
            cast_out_ref[t] = cast_ref[:, t * FFN_UP_TN:(t + 1) * FFN_UP_TN].astype(BF16)
    else:
        out_refs = refs
    x_out = out_refs[0]
    a = a_ref[...]
    for c in range(0, D_MODEL, FFN_DOWN_COL_CHUNK):
        cols = slice(c, c + FFN_DOWN_COL_CHUNK)
        x_out[:, cols] = x_ref[:, cols] + (0.5 * gt_ref[:, cols]) * _dot(a, w_ref[:, cols])
    if tail == "final_norm":
        x_out[...] = _rms(x_out[...], g_ref[...])
    elif tail == "mixer_norm":
        out_refs[1][...] = _norm_mod(x_out[...], g_ref[...], sh_ref[...], sc_ref[...]).astype(BF16)


def _ffn_down(act, w_d, x, mod, k_gate, g_tail, k_shift_tail, tail, w_gu_stack=None, layer=None):
    tm = 256
    ni = TOKENS // tm
    row = lambda i: (i, 0)
    x_spec = pl.BlockSpec((tm, D_MODEL), row)
    in_specs = [pl.BlockSpec((tm, D_FF), row),
                pl.BlockSpec((D_FF, D_MODEL), lambda i: (0, 0), pipeline_mode=pl.Buffered(1)),
                x_spec,
                _mod_spec(tm, k_gate),
                pl.BlockSpec((1, D_MODEL), lambda i: (0, 0)),
                _mod_spec(tm, k_shift_tail), _mod_spec(tm, k_shift_tail + 1)]
    operands = [act, w_d, x, mod, g_tail, mod, mod]
    out_specs = [x_spec]
    out_shape = [jax.ShapeDtypeStruct((TOKENS, D_MODEL), F32)]
    if tail == "mixer_norm":
        out_specs.append(pl.BlockSpec((tm, D_MODEL), row))
        out_shape.append(jax.ShapeDtypeStruct((TOKENS, D_MODEL), BF16))
    cast = w_gu_stack is not None
    if cast:
        slab = D_MODEL // ni
        in_specs.append(pl.BlockSpec((None, slab, 2 * D_FF), lambda i: (layer, i, 0)))
        operands.append(w_gu_stack)
        out_specs.append(pl.BlockSpec((GU_TILES, slab, FFN_UP_TN), lambda i: (0, i, 0)))
        out_shape.append(jax.ShapeDtypeStruct((GU_TILES, D_MODEL, FFN_UP_TN), BF16))
    return pl.pallas_call(
        functools.partial(_ffn_down_kernel, tail=tail, cast=cast),
        grid=(ni,),
        in_specs=in_specs,
        out_specs=out_specs,
        out_shape=out_shape,
        compiler_params=_params(("parallel",)),
        name="ffn_down",
    )(*operands)


def _in_proj_kernel(h_ref, tab_ref, wx_ref, gcq_ref, gckv_ref, wuq_ref, wukv_ref,
                    qa_ref, ka_ref, va_ref, qp_ref, kp_ref, vb_ref):
    h = h_ref[...]
    cos_a = tab_ref[:, 0 * LANES:1 * LANES]
    sin_a = tab_ref[:, 1 * LANES:2 * LANES]
    cos_b = tab_ref[:, 2 * LANES:3 * LANES]
    sin_lo = tab_ref[:, 3 * LANES:4 * LANES]
    sin_hi = tab_ref[:, 4 * LANES:5 * LANES]

    def rope_a(t):
        return t * cos_a + pltpu.roll(t, 64, 1) * sin_a

    def rope_b(t):
        return t * cos_b + pltpu.roll(t, 96, 1) * sin_lo + pltpu.roll(t, 32, 1) * sin_hi

    proj = _dot(h, wx_ref[...])

    for hd in range(HA_Q):
        sl = slice(hd * HEAD_DIM, (hd + 1) * HEAD_DIM)
        qa_ref[:, sl] = rope_a(proj[:, sl]).astype(BF16)
    for hd in range(HA_KV):
        sl = slice(hd * HEAD_DIM, (hd + 1) * HEAD_DIM)
        ka_ref[:, sl] = rope_a(proj[:, COLS_QA + hd * HEAD_DIM:COLS_QA + (hd + 1) * HEAD_DIM]).astype(BF16)
    va_ref[...] = proj[:, COLS_QA + COLS_KA:OFF_CQ].astype(BF16)

    cq = _rms(proj[:, OFF_CQ:OFF_CKV], gcq_ref[...]).astype(BF16)
    q_all = _dot(cq, wuq_ref[...])
    for hd in range(HB):
        base = hd * MLA_PAD
        qp_ref[base:base + QK_NOPE, :] = q_all[:, base:base + QK_NOPE].T.astype(BF16)
        qp_ref[base + QK_NOPE:base + MLA_PAD, :] = rope_b(q_all[:, base + QK_NOPE:base + MLA_PAD]).T.astype(BF16)

    ckv = _rms(proj[:, OFF_CKV:OFF_KR], gckv_ref[...]).astype(BF16)
    kv_all = _dot(ckv, wukv_ref[...])
    kr = rope_b(proj[:, OFF_KR:OFF_KR + LANES]).astype(BF16)
    for hd in range(HB):
        base = hd * MLA_PAD
        kp_ref[:, base:base + QK_NOPE] = kv_all[:, hd * QK_NOPE:(hd + 1) * QK_NOPE].astype(BF16)
        kp_ref[:, base + QK_NOPE:base + MLA_PAD] = kr
        vsl = slice(HB * QK_NOPE + hd * V_DIM, HB * QK_NOPE + (hd + 1) * V_DIM)
        vb_ref[hd * V_DIM:(hd + 1) * V_DIM, :] = kv_all[:, vsl].T.astype(BF16)


def _in_proj(h, tables, w_x, g_cq, g_ckv, w_uq, w_ukv):
    tm = 512
    const = lambda i: (0, 0)
    row = lambda i: (i, 0)
    nx = w_x.shape[1]
    col = lambda i: (0, i)
    resident = lambda shape: pl.BlockSpec(shape, const, pipeline_mode=pl.Buffered(1))
    outs = [(COLS_QA, True), (COLS_KA, True), (COLS_VA, True), (HB * MLA_PAD, False),
            (HB * MLA_PAD, True), (HB * V_DIM, False)]
    out_specs = [pl.BlockSpec((tm, n), row) if tok else pl.BlockSpec((n, tm), col) for n, tok in outs]
    out_shape = [jax.ShapeDtypeStruct((TOKENS, n) if tok else (n, TOKENS), BF16) for n, tok in outs]
    return pl.pallas_call(
        _in_proj_kernel,
        grid=(TOKENS // tm,),
        in_specs=[pl.BlockSpec((tm, D_MODEL), row),
                  pl.BlockSpec((tm, N_ROPE_TABLES * LANES), row),
                  resident((D_MODEL, nx)),
                  pl.BlockSpec((1, Q_RANK), const),
                  pl.BlockSpec((1, KV_RANK), const),
                  resident((Q_RANK, HB * MLA_PAD)),
                  resident((KV_RANK, HB * (QK_NOPE + V_DIM)))],
        out_specs=out_specs,
        out_shape=out_shape,
        compiler_params=_params(("parallel",)),
        name="in_proj",
    )(h, tables, w_x, g_cq, g_ckv, w_uq, w_ukv)


WIN_BLOCKS_PER_STEP = 4
WIN_KEYS = 3 * BLOCK
WIN_ROWS = GROUP * BLOCK


def _win_mask_bias():
    rel = jnp.arange(WIN_KEYS)[None, :] - (jnp.arange(WIN_ROWS) % BLOCK)[:, None]
    off = -BLOCK * jnp.arange(3)[:, None, None]
    return jnp.where(jnp.abs(rel[None] + off) <= WINDOW, 0.0, -1e30).astype(F32)


def _win_attn_kernel(sink_ref, bias_ref, q_ref, k_ref, v_ref, o_ref):
    hk = pl.program_id(1)
    nb = SEQ // BLOCK
    c = HEAD_DIM ** -0.5 * LOG2E
    row = lax.broadcasted_iota(jnp.int32, (WIN_ROWS, 1), 0)
    sk = jnp.zeros((WIN_ROWS, 1), F32)
    for g in range(GROUP):
        sk = jnp.where(row // BLOCK == g, sink_ref[hk * GROUP + g] * LOG2E, sk)
    for j in range(WIN_BLOCKS_PER_STEP):
        n = pl.program_id(2) * WIN_BLOCKS_PER_STEP + j
        first = jnp.clip(n - 1, 0, nb - 3)
        start = pl.multiple_of(first * BLOCK, BLOCK)
        kw = k_ref[pl.ds(start, WIN_KEYS), :]
        vw = v_ref[pl.ds(start, WIN_KEYS), :]
        q = jnp.concatenate([q_ref[j * BLOCK:(j + 1) * BLOCK, g * HEAD_DIM:(g + 1) * HEAD_DIM]
                             for g in range(GROUP)], axis=0)
        t = _dot_nt(q, kw) * c + bias_ref[n - first]
        m = jnp.maximum(jnp.max(t, axis=-1, keepdims=True), sk)
        p = jnp.exp2(t - m)
        denom = jnp.sum(p, axis=-1, keepdims=True) + jnp.exp2(sk - m)
        o = (_dot(p.astype(BF16), vw) * (1.0 / denom)).astype(BF16)
        for g in range(GROUP):
            o_ref[j * BLOCK:(j + 1) * BLOCK, g * HEAD_DIM:(g + 1) * HEAD_DIM] = o[g * BLOCK:(g + 1) * BLOCK]


def _win_attn(qa, ka, va, sink):
    tq = WIN_BLOCKS_PER_STEP * BLOCK
    nq = SEQ // tq
    gw = GROUP * HEAD_DIM
    return pl.pallas_call(
        _win_attn_kernel,
        grid=(BATCH, HA_KV, nq),
        in_specs=[pl.BlockSpec(memory_space=pltpu.SMEM),
                  pl.BlockSpec((3, WIN_ROWS, WIN_KEYS), lambda b, h, n: (0, 0, 0)),
                  pl.BlockSpec((tq, gw), lambda b, h, n: (b * nq + n, h)),
                  pl.BlockSpec((SEQ, HEAD_DIM), lambda b, h, n: (b, h)),
                  pl.BlockSpec((SEQ, HEAD_DIM), lambda b, h, n: (b, h))],
        out_specs=pl.BlockSpec((tq, gw), lambda b, h, n: (b * nq + n, h)),
        out_shape=jax.ShapeDtypeStruct((TOKENS, COLS_QA), BF16),
        compiler_params=_params(("parallel", "parallel", "arbitrary")),
        name="win_attn",
    )(sink, _win_mask_bias(), qa, ka, va)


MLA_KEY_CHUNK = 512
MLA_ONES_ROWS = 16


def _mla_attn_kernel(qt_ref, k_ref, vt_ref, o_ref):
    c = (QK_NOPE + QK_ROPE) ** -0.5 * LOG2E
    qt = qt_ref[...]
    tq = qt.shape[1]
    m = jnp.full((1, tq), -1e30, F32)
    acc = jnp.zeros((V_DIM + MLA_ONES_ROWS, tq), F32)
    ones = jnp.ones((MLA_ONES_ROWS, MLA_KEY_CHUNK), BF16)
    n_chunks = SEQ // MLA_KEY_CHUNK
    chunk = lambda ci: slice(ci * MLA_KEY_CHUNK, (ci + 1) * MLA_KEY_CHUNK)
    st_next = _dot(k_ref[chunk(0), :], qt)
    for ci in range(n_chunks):
        ks = chunk(ci)
        st = st_next
        if ci + 1 < n_chunks:
            st_next = _dot(k_ref[chunk(ci + 1), :], qt)
        m_new = jnp.maximum(m, jnp.max(st, axis=0, keepdims=True))
        alpha = jnp.exp2((m - m_new) * c)
        p = jnp.exp2((st - m_new) * c).astype(BF16)
        vt1 = jnp.concatenate([vt_ref[:, ks], ones], axis=0)
        acc = acc * alpha + _dot(vt1, p)
        m = m_new
    o_ref[...] = (acc[:V_DIM] * (1.0 / acc[V_DIM:V_DIM + 1])).T.astype(BF16)


def _mla_attn(q_pad_t, k_pad, v_t):
    tq = 2048
    nq = SEQ // tq
    return pl.pallas_call(
        _mla_attn_kernel,
        grid=(BATCH, HB, nq),
        in_specs=[pl.BlockSpec((MLA_PAD, tq), lambda b, h, i: (h, b * nq + i)),
                  pl.BlockSpec((SEQ, MLA_PAD), lambda b, h, i: (b, h)),
                  pl.BlockSpec((V_DIM, SEQ), lambda b, h, i: (h, b))],
        out_specs=pl.BlockSpec((tq, V_DIM), lambda b, h, i: (b * nq + i, h)),
        out_shape=jax.ShapeDtypeStruct((TOKENS, HB * V_DIM), BF16),
        compiler_params=_params(("parallel", "parallel", "arbitrary")),
        name="mla_attn",
    )(q_pad_t, k_pad, v_t)


MIX_COL_CHUNK = 256


def _mix_out_kernel(x_ref, h_ref, gt_ref, oa_ref, ob_ref, woa_ref, wob_ref,
                    wga_ref, wgb_ref, bga_ref, bgb_ref, wout_ref, o_ref, acc_ref, mix_ref):
    n = pl.program_id(1)
    last = pl.num_programs(1) - 1

    def merge():
        h = h_ref[...]
        o_a = oa_ref[...]
        o_b = ob_ref[...]
        mix = []
        for c in range(0, wout_ref.shape[0], MIX_COL_CHUNK):
            cols = slice(c, c + MIX_COL_CHUNK)
            y_a = _dot(o_a, woa_ref[:, cols])
            y_b = _dot(o_b, wob_ref[:, cols])
            g_a = jax.nn.sigmoid(_dot(h, wga_ref[:, cols]) + bga_ref[:, cols])
            g_b = jax.nn.sigmoid(_dot(h, wgb_ref[:, cols]) + bgb_ref[:, cols])
            mix.append((g_a * y_a + g_b * y_b).astype(BF16))
        return jnp.concatenate(mix, axis=1)

    @pl.when(n == 0)
    def _():
        acc_ref[...] = jnp.zeros_like(acc_ref)
        mix_ref[...] = merge()

    @pl.when((n != 0) & (n != last))
    def _():
        prev = mix_ref[...]
        acc_ref[...] += _dot(prev, wout_ref[...])
        mix_ref[...] = merge()

    @pl.when(n == last)
    def _():
        y = acc_ref[...] + _dot(mix_ref[...], wout_ref[...])
        o_ref[...] = x_ref[...] + gt_ref[...] * y


MIX_TC = 512


def _mix_out(x, h, mod, k_gate, o_a, o_b, w_oa, w_ob, w_gate, b_gate, w_out):
    tm, tc = 512, MIX_TC
    nc = D_MODEL // tc
    row = lambda i, n: (i, 0)
    col = lambda i, n: (0, jnp.minimum(n, nc - 1))
    col_hi = lambda i, n: (0, jnp.minimum(n, nc - 1) + nc)
    tile = lambda i, n: (jnp.minimum(n, nc - 1), 0, 0)
    tile_hi = lambda i, n: (jnp.minimum(n, nc - 1) + nc, 0, 0)
    return pl.pallas_call(
        _mix_out_kernel,
        grid=(TOKENS // tm, nc + 1),
        in_specs=[pl.BlockSpec((tm, D_MODEL), row),
                  pl.BlockSpec((tm, D_MODEL), row),
                  _mod_spec(tm, k_gate),
                  pl.BlockSpec((tm, COLS_QA), row),
                  pl.BlockSpec((tm, HB * V_DIM), row),
                  pl.BlockSpec((None, COLS_QA, tc), tile),
                  pl.BlockSpec((None, HB * V_DIM, tc), tile),
                  pl.BlockSpec((None, D_MODEL, tc), tile),
                  pl.BlockSpec((None, D_MODEL, tc), tile_hi),
                  pl.BlockSpec((1, tc), col),
                  pl.BlockSpec((1, tc), col_hi),
                  pl.BlockSpec((tc, D_MODEL), lambda i, n: (jnp.maximum(n - 1, 0), 0))],
        out_specs=pl.BlockSpec((tm, D_MODEL), row),
        out_shape=jax.ShapeDtypeStruct((TOKENS, D_MODEL), F32),
        scratch_shapes=[pltpu.VMEM((tm, D_MODEL), F32), pltpu.VMEM((tm, tc), BF16)],
        compiler_params=_params(("parallel", "arbitrary")),
        name="mix_out",
    )(x, h, mod, o_a, o_b, w_oa, w_ob, w_gate, w_gate, b_gate, b_gate, w_out)


def _pad_heads(w, n_heads, width, pad_to):
    k = w.shape[0]
    w = w.reshape(k, n_heads, width)
    w = jnp.pad(w, ((0, 0), (0, 0), (0, pad_to - width)))
    return w.reshape(k, n_heads * pad_to)


def _layer_weights(l, w_in, w_uq, w_ukv):
    w = w_in[l]
    kr_pad = jnp.pad(w[:, OFF_KR:OFF_GATE], ((0, 0), (0, LANES - QK_ROPE)))
    w_x = jnp.concatenate([w[:, :OFF_KR], kr_pad], axis=1).astype(BF16)
    w_gate = _col_tiles(w[:, OFF_GATE:].astype(BF16), MIX_TC)
    w_uq_pad = _pad_heads(w_uq[l], HB, QK_NOPE + QK_ROPE, MLA_PAD).astype(BF16)
    ukv = w_ukv[l].reshape(KV_RANK, HB, QK_NOPE + V_DIM)
    w_ukv_perm = jnp.concatenate([ukv[:, :, :QK_NOPE].reshape(KV_RANK, HB * QK_NOPE),
                                  ukv[:, :, QK_NOPE:].reshape(KV_RANK, HB * V_DIM)], axis=1).astype(BF16)
    return w_x, w_gate, w_uq_pad, w_ukv_perm


def kernel(x, c, positions, norm_g, w_ada, b_ada, w_ffn1_gu, w_ffn1_d, w_ffn2_gu, w_ffn2_d, w_in, b_gate, sink,
           g_cq, g_ckv, w_uq, w_ukv, w_oa, w_ob, w_out, g_final):
    tables = _rope_tables(positions)
    mods = _ada_mod(c, w_ada, b_ada)
    xt = x.reshape(TOKENS, D_MODEL)
    w_gu = _col_tiles(w_ffn1_gu[0].astype(BF16), FFN_UP_TN)
    for l in range(DEPTH):
        mod = mods[l].reshape(BATCH, 1, N_MOD * D_MODEL)
        g = norm_g[l]
        w_x, w_gate, w_uq_pad, w_ukv_perm = _layer_weights(l, w_in, w_uq, w_ukv)

        act, w_d = _ffn_up(xt, g[0:1], mod, 0, w_gu, w_ffn1_d, l)
        xt, h, w_gu = _ffn_down(act, w_d, xt, mod, 2, g[1:2], 3, "mixer_norm", w_ffn2_gu, l)

        qa, ka, va, q_pad_t, k_pad, v_t = _in_proj(h, tables, w_x, g_cq[l][None, :], g_ckv[l][None, :],
                                                   w_uq_pad, w_ukv_perm)
        o_a = _win_attn(qa, ka, va, sink[l])
        o_b = _mla_attn(q_pad_t, k_pad, v_t)
        xt = _mix_out(xt, h, mod, 5, o_a, o_b, _col_tiles(w_oa[l].astype(BF16), MIX_TC),
                      _col_tiles(w_ob[l].astype(BF16), MIX_TC), w_gate, b_gate[l][None, :], w_out[l].astype(BF16))

        act, w_d = _ffn_up(xt, g[2:3], mod, 6, w_gu, w_ffn2_d, l)
        if l + 1 < DEPTH:
            xt, w_gu = _ffn_down(act, w_d, xt, mod, 8, g_final[None, :], 0, "plain", w_ffn1_gu, l + 1)
        else:
            xt, = _ffn_down(act, w_d, xt, mod, 8, g_final[None, :], 0, "final_norm")
    return xt.reshape(BATCH, SEQ, D_MODEL)
```

```python
import functools

import jax
import jax.numpy as jnp
import numpy as np
from jax import lax
from jax.experimental import pallas as pl
from jax.experimental.pallas import tpu as pltpu

D_MODEL = 2048
BATCH = 8
SEQ = 4096
DEPTH = 2
HA_Q = 8
HA_KV = 2
GROUP = HA_Q // HA_KV
HEAD_DIM = 128
WINDOW = 128
BLOCK = 128
HB = 8
QK_NOPE = 128
QK_ROPE = 64
V_DIM = 128
Q_RANK = 512
KV_RANK = 512
D_FF = 5632
ROPE_THETA = 10000.0
EPS = 1e-6
N_MOD = 9
TOKENS = BATCH * SEQ

COLS_QA = HA_Q * HEAD_DIM
COLS_KA = HA_KV * HEAD_DIM
COLS_VA = HA_KV * HEAD_DIM
OFF_CQ = COLS_QA + COLS_KA + COLS_VA
OFF_CKV = OFF_CQ + Q_RANK
OFF_KR = OFF_CKV + KV_RANK
OFF_GATE = OFF_KR + QK_ROPE

LANES = 128
MLA_PAD = 256
N_ROPE_TABLES = 5
VMEM_LIMIT = 56 * 1024 * 1024
LOG2E = float(np.log2(np.e))

BF16 = jnp.bfloat16
F32 = jnp.float32


def _params(sem):
    return pltpu.CompilerParams(dimension_semantics=sem, vmem_limit_bytes=VMEM_LIMIT)


def _dot(a, b):
    return jnp.dot(a, b, preferred_element_type=F32)


def _dot_nt(a, b):
    return lax.dot_general(a, b, (((1,), (1,)), ((), ())), preferred_element_type=F32)


def _rms(x, g):
    return (x * lax.rsqrt(jnp.mean(x * x, axis=-1, keepdims=True) + EPS)) * g


def _norm_mod(x, g, shift, scale):
    return _rms(x, g) * (1.0 + scale) + shift


def _rope_table_kernel(pos_ref, freq_ref, o_ref):
    ang = pos_ref[...].astype(F32) * freq_ref[...]
    cs = jnp.cos(ang)
    sn = jnp.sin(ang)
    lane = lax.broadcasted_iota(jnp.int32, cs.shape, 1)
    cs64 = pltpu.roll(cs, 64, 1)
    sn64 = pltpu.roll(sn, 64, 1)
    cs96 = pltpu.roll(cs, 96, 1)
    sn96 = pltpu.roll(sn, 96, 1)
    zero = jnp.zeros_like(cs)
    o_ref[:, 0 * LANES:1 * LANES] = jnp.where(lane < 64, cs, cs64)
    o_ref[:, 1 * LANES:2 * LANES] = jnp.where(lane < 64, -sn, sn64)
    o_ref[:, 2 * LANES:3 * LANES] = jnp.where(lane < 32, cs64, jnp.where(lane < 64, cs96, zero))
    o_ref[:, 3 * LANES:4 * LANES] = jnp.where(lane < 32, -sn64, zero)
    o_ref[:, 4 * LANES:5 * LANES] = jnp.where((lane >= 32) & (lane < 64), sn96, zero)


def _rope_tables(positions):
    tm = 1024
    fa = ROPE_THETA ** (-jnp.arange(0, HEAD_DIM, 2, dtype=F32) / HEAD_DIM)
    fb = ROPE_THETA ** (-jnp.arange(0, QK_ROPE, 2, dtype=F32) / QK_ROPE)
    freq = jnp.concatenate([fa, fb, jnp.zeros((32,), F32)])[None, :]
    pos = positions.reshape(TOKENS, 1)
    return pl.pallas_call(
        _rope_table_kernel,
        grid=(TOKENS // tm,),
        in_specs=[pl.BlockSpec((tm, 1), lambda i: (i, 0)),
                  pl.BlockSpec((1, LANES), lambda i: (0, 0))],
        out_specs=pl.BlockSpec((tm, N_ROPE_TABLES * LANES), lambda i: (i, 0)),
        out_shape=jax.ShapeDtypeStruct((TOKENS, N_ROPE_TABLES * LANES), F32),
        compiler_params=_params(("parallel",)),
        name="rope_tables",
    )(pos, freq)


def _ada_kernel(c_ref, w_ref, b_ref, o_ref):
    c = c_ref[...]
    c_act = (c * jax.nn.sigmoid(c)).astype(BF16)
    o_ref[...] = _dot(c_act, w_ref[...].astype(BF16)) + b_ref[...]


def _ada_mod(c, w_ada, b_ada):
    tn = 1024
    n = N_MOD * D_MODEL
    return pl.pallas_call(
        _ada_kernel,
        grid=(DEPTH, n // tn),
        in_specs=[pl.BlockSpec((BATCH, D_MODEL), lambda l, j: (0, 0)),
                  pl.BlockSpec((None, D_MODEL, tn), lambda l, j: (l, 0, j)),
                  pl.BlockSpec((None, 1, tn), lambda l, j: (l, 0, j))],
        out_specs=pl.BlockSpec((None, BATCH, tn), lambda l, j: (l, 0, j)),
        out_shape=jax.ShapeDtypeStruct((DEPTH, BATCH, n), F32),
        compiler_params=_params(("parallel", "parallel")),
        name="ada_mod",
    )(c, w_ada, b_ada.reshape(DEPTH, 1, n))


def _mod_spec(tm, k):
    return pl.BlockSpec((None, 1, D_MODEL), lambda i, *_: (i * tm // SEQ, 0, k))


FFN_ROW_CHUNK = 256


def _cast_specs(w_stack, layer, n_steps, step_index):
    _, rows, cols = w_stack.shape
    assert rows % n_steps == 0
    slab = rows // n_steps
    in_spec = pl.BlockSpec((None, slab, cols), lambda *ids: (layer, step_index(*ids), 0))
    out_spec = pl.BlockSpec((slab, cols), lambda *ids: (step_index(*ids), 0))
    return in_spec, out_spec, jax.ShapeDtypeStruct((rows, cols), BF16)


def _ffn_up_kernel(x_ref, g_ref, sh_ref, sc_ref, wg_ref, wu_ref, cast_ref, o_ref, cast_out_ref, h_ref):
    cast_out_ref[...] = cast_ref[...].astype(BF16)

    def swiglu(h):
        g = _dot(h, wg_ref[...])
        u = _dot(h, wu_ref[...])
        return (g * jax.nn.sigmoid(g) * u).astype(BF16)

    @pl.when(pl.program_id(1) == 0)
    def _():
        for r in range(0, x_ref.shape[0], FFN_ROW_CHUNK):
            rows = slice(r, r + FFN_ROW_CHUNK)
            h = _norm_mod(x_ref[rows, :], g_ref[...], sh_ref[...], sc_ref[...]).astype(BF16)
            h_ref[rows, :] = h
            o_ref[rows, :] = swiglu(h)

    @pl.when(pl.program_id(1) != 0)
    def _():
        o_ref[...] = swiglu(h_ref[...])


def _ffn_up(x, g, mod, k_shift, w_gu, w_d_stack, layer):
    tm, tn = 1024, 512
    nj = D_FF // tn
    ni = TOKENS // tm
    cast_in, cast_out, cast_shape = _cast_specs(w_d_stack, layer, ni * nj, lambda i, j: i * nj + j)
    return pl.pallas_call(
        _ffn_up_kernel,
        grid=(ni, nj),
        in_specs=[pl.BlockSpec((tm, D_MODEL), lambda i, j: (i, 0)),
                  pl.BlockSpec((1, D_MODEL), lambda i, j: (0, 0)),
                  _mod_spec(tm, k_shift), _mod_spec(tm, k_shift + 1),
                  pl.BlockSpec((D_MODEL, tn), lambda i, j: (0, j)),
                  pl.BlockSpec((D_MODEL, tn), lambda i, j: (0, j + nj)),
                  cast_in],
        out_specs=[pl.BlockSpec((tm, tn), lambda i, j: (i, j)), cast_out],
        out_shape=[jax.ShapeDtypeStruct((TOKENS, D_FF), BF16), cast_shape],
        scratch_shapes=[pltpu.VMEM((tm, D_MODEL), BF16)],
        compiler_params=_params(("parallel", "arbitrary")),
        name="ffn_up",
    )(x, g, mod, mod, w_gu, w_gu, w_d_stack)


FFN_DOWN_COL_CHUNK = 512


def _ffn_down_kernel(a_ref, w_ref, x_ref, gt_ref, g_ref, sh_ref, sc_ref, *refs, tail, cast):
    if cast:
        cast_ref, *out_refs, cast_out_ref = refs
        cast_out_ref[...] = cast_ref[...].astype(BF16)
    else:
        out_refs = refs
    x_out = out_refs[0]
    a = a_ref[...]
    for c in range(0, D_MODEL, FFN_DOWN_COL_CHUNK):
        cols = slice(c, c + FFN_DOWN_COL_CHUNK)
        x_out[:, cols] = x_ref[:, cols] + (0.5 * gt_ref[:, cols]) * _dot(a, w_ref[:, cols])
    if tail == "final_norm":
        x_out[...] = _rms(x_out[...], g_ref[...])
    elif tail == "mixer_norm":
        out_refs[1][...] = _norm_mod(x_out[...], g_ref[...], sh_ref[...], sc_ref[...]).astype(BF16)


def _ffn_down(act, w_d, x, mod, k_gate, g_tail, k_shift_tail, tail, w_gu_stack=None, layer=None):
    tm = 256
    ni = TOKENS // tm
    row = lambda i: (i, 0)
    x_spec = pl.BlockSpec((tm, D_MODEL), row)
    in_specs = [pl.BlockSpec((tm, D_FF), row),
                pl.BlockSpec((D_FF, D_MODEL), lambda i: (0, 0), pipeline_mode=pl.Buffered(1)),
                x_spec,
                _mod_spec(tm, k_gate),
                pl.BlockSpec((1, D_MODEL), lambda i: (0, 0)),
                _mod_spec(tm, k_shift_tail), _mod_spec(tm, k_shift_tail + 1)]
    operands = [act, w_d, x, mod, g_tail, mod, mod]
    out_specs = [x_spec]
    out_shape = [jax.ShapeDtypeStruct((TOKENS, D_MODEL), F32)]
    if tail == "mixer_norm":
        out_specs.append(pl.BlockSpec((tm, D_MODEL), row))
        out_shape.append(jax.ShapeDtypeStruct((TOKENS, D_MODEL), BF16))
    cast = w_gu_stack is not None
    if cast:
        cast_in, cast_out, cast_shape = _cast_specs(w_gu_stack, layer, ni, lambda i: i)
        in_specs.append(cast_in)
        operands.append(w_gu_stack)
        out_specs.append(cast_out)
        out_shape.append(cast_shape)
    return pl.pallas_call(
        functools.partial(_ffn_down_kernel, tail=tail, cast=cast),
        grid=(ni,),
        in_specs=in_specs,
        out_specs=out_specs,
        out_shape=out_shape,
        compiler_params=_params(("parallel",)),
        name="ffn_down",
    )(*operands)


def _in_proj_kernel(h_ref, tab_ref, wx_ref, gcq_ref, gckv_ref, wuq_ref, wukv_ref,
                    qa_ref, ka_ref, va_ref, qp_ref, kp_ref, vb_ref):
    h = h_ref[...]
    cos_a = tab_ref[:, 0 * LANES:1 * LANES]
    sin_a = tab_ref[:, 1 * LANES:2 * LANES]
    cos_b = tab_ref[:, 2 * LANES:3 * LANES]
    sin_lo = tab_ref[:, 3 * LANES:4 * LANES]
    sin_hi = tab_ref[:, 4 * LANES:5 * LANES]

    def rope_a(t):
        return t * cos_a + pltpu.roll(t, 64, 1) * sin_a

    def rope_b(t):
        return t * cos_b + pltpu.roll(t, 96, 1) * sin_lo + pltpu.roll(t, 32, 1) * sin_hi

    proj = _dot(h, wx_ref[...])

    for hd in range(HA_Q):
        sl = slice(hd * HEAD_DIM, (hd + 1) * HEAD_DIM)
        qa_ref[:, sl] = rope_a(proj[:, sl]).astype(BF16)
    for hd in range(HA_KV):
        sl = slice(hd * HEAD_DIM, (hd + 1) * HEAD_DIM)
        ka_ref[:, sl] = rope_a(proj[:, COLS_QA + hd * HEAD_DIM:COLS_QA + (hd + 1) * HEAD_DIM]).astype(BF16)
    va_ref[...] = proj[:, COLS_QA + COLS_KA:OFF_CQ].astype(BF16)

    cq = _rms(proj[:, OFF_CQ:OFF_CKV], gcq_ref[...]).astype(BF16)
    q_all = _dot(cq, wuq_ref[...])
    for hd in range(HB):
        base = hd * MLA_PAD
        qp_ref[base:base + QK_NOPE, :] = q_all[:, base:base + QK_NOPE].T.astype(BF16)
        qp_ref[base + QK_NOPE:base + MLA_PAD, :] = rope_b(q_all[:, base + QK_NOPE:base + MLA_PAD]).T.astype(BF16)

    ckv = _rms(proj[:, OFF_CKV:OFF_KR], gckv_ref[...]).astype(BF16)
    kv_all = _dot(ckv, wukv_ref[...])
    kr = rope_b(proj[:, OFF_KR:OFF_KR + LANES]).astype(BF16)
    for hd in range(HB):
        base = hd * MLA_PAD
        kp_ref[:, base:base + QK_NOPE] = kv_all[:, hd * QK_NOPE:(hd + 1) * QK_NOPE].astype(BF16)
        kp_ref[:, base + QK_NOPE:base + MLA_PAD] = kr
        vsl = slice(HB * QK_NOPE + hd * V_DIM, HB * QK_NOPE + (hd + 1) * V_DIM)
        vb_ref[hd * V_DIM:(hd + 1) * V_DIM, :] = kv_all[:, vsl].T.astype(BF16)


def _in_proj(h, tables, w_x, g_cq, g_ckv, w_uq, w_ukv):
    tm = 512
    const = lambda i: (0, 0)
    row = lambda i: (i, 0)
    nx = w_x.shape[1]
    col = lambda i: (0, i)
    resident = lambda shape: pl.BlockSpec(shape, const, pipeline_mode=pl.Buffered(1))
    outs = [(COLS_QA, True), (COLS_KA, True), (COLS_VA, True), (HB * MLA_PAD, False),
            (HB * MLA_PAD, True), (HB * V_DIM, False)]
    out_specs = [pl.BlockSpec((tm, n), row) if tok else pl.BlockSpec((n, tm), col) for n, tok in outs]
    out_shape = [jax.ShapeDtypeStruct((TOKENS, n) if tok else (n, TOKENS), BF16) for n, tok in outs]
    return pl.pallas_call(
        _in_proj_kernel,
        grid=(TOKENS // tm,),
        in_specs=[pl.BlockSpec((tm, D_MODEL), row),
                  pl.BlockSpec((tm, N_ROPE_TABLES * LANES), row),
                  resident((D_MODEL, nx)),
                  pl.BlockSpec((1, Q_RANK), const),
                  pl.BlockSpec((1, KV_RANK), const),
                  resident((Q_RANK, HB * MLA_PAD)),
                  resident((KV_RANK, HB * (QK_NOPE + V_DIM)))],
        out_specs=out_specs,
        out_shape=out_shape,
        compiler_params=_params(("parallel",)),
        name="in_proj",
    )(h, tables, w_x, g_cq, g_ckv, w_uq, w_ukv)


WIN_BLOCKS_PER_STEP = 4
WIN_KEYS = 3 * BLOCK
WIN_ROWS = GROUP * BLOCK


def _win_mask_bias():
    rel = jnp.arange(WIN_KEYS)[None, :] - (jnp.arange(WIN_ROWS) % BLOCK)[:, None]
    off = -BLOCK * jnp.arange(3)[:, None, None]
    return jnp.where(jnp.abs(rel[None] + off) <= WINDOW, 0.0, -1e30).astype(F32)


def _win_attn_kernel(sink_ref, bias_ref, q_ref, k_ref, v_ref, o_ref):
    hk = pl.program_id(1)
    nb = SEQ // BLOCK
    c = HEAD_DIM ** -0.5 * LOG2E
    row = lax.broadcasted_iota(jnp.int32, (WIN_ROWS, 1), 0)
    sk = jnp.zeros((WIN_ROWS, 1), F32)
    for g in range(GROUP):
        sk = jnp.where(row // BLOCK == g, sink_ref[hk * GROUP + g] * LOG2E, sk)
    for j in range(WIN_BLOCKS_PER_STEP):
        n = pl.program_id(2) * WIN_BLOCKS_PER_STEP + j
        first = jnp.clip(n - 1, 0, nb - 3)
        start = pl.multiple_of(first * BLOCK, BLOCK)
        kw = k_ref[pl.ds(start, WIN_KEYS), :]
        vw = v_ref[pl.ds(start, WIN_KEYS), :]
        q = jnp.concatenate([q_ref[j * BLOCK:(j + 1) * BLOCK, g * HEAD_DIM:(g + 1) * HEAD_DIM]
                             for g in range(GROUP)], axis=0)
        t = _dot_nt(q, kw) * c + bias_ref[n - first]
        m = jnp.maximum(jnp.max(t, axis=-1, keepdims=True), sk)
        p = jnp.exp2(t - m)
        denom = jnp.sum(p, axis=-1, keepdims=True) + jnp.exp2(sk - m)
        o = (_dot(p.astype(BF16), vw) * (1.0 / denom)).astype(BF16)
        for g in range(GROUP):
            o_ref[j * BLOCK:(j + 1) * BLOCK, g * HEAD_DIM:(g + 1) * HEAD_DIM] = o[g * BLOCK:(g + 1) * BLOCK]


def _win_attn(qa, ka, va, sink):
    tq = WIN_BLOCKS_PER_STEP * BLOCK
    nq = SEQ // tq
    gw = GROUP * HEAD_DIM
    return pl.pallas_call(
        _win_attn_kernel,
        grid=(BATCH, HA_KV, nq),
        in_specs=[pl.BlockSpec(memory_space=pltpu.SMEM),
                  pl.BlockSpec((3, WIN_ROWS, WIN_KEYS), lambda b, h, n: (0, 0, 0)),
                  pl.BlockSpec((tq, gw), lambda b, h, n: (b * nq + n, h)),
                  pl.BlockSpec((SEQ, HEAD_DIM), lambda b, h, n: (b, h)),
                  pl.BlockSpec((SEQ, HEAD_DIM), lambda b, h, n: (b, h))],
        out_specs=pl.BlockSpec((tq, gw), lambda b, h, n: (b * nq + n, h)),
        out_shape=jax.ShapeDtypeStruct((TOKENS, COLS_QA), BF16),
        compiler_params=_params(("parallel", "parallel", "arbitrary")),
        name="win_attn",
    )(sink, _win_mask_bias(), qa, ka, va)


MLA_KEY_CHUNK = 512
MLA_ONES_ROWS = 16


def _mla_attn_kernel(qt_ref, k_ref, vt_ref, o_ref):
    c = (QK_NOPE + QK_ROPE) ** -0.5 * LOG2E
    qt = qt_ref[...]
    tq = qt.shape[1]
    m = jnp.full((1, tq), -1e30, F32)
    acc = jnp.zeros((V_DIM + MLA_ONES_ROWS, tq), F32)
    ones = jnp.ones((MLA_ONES_ROWS, MLA_KEY_CHUNK), BF16)
    n_chunks = SEQ // MLA_KEY_CHUNK
    chunk = lambda ci: slice(ci * MLA_KEY_CHUNK, (ci + 1) * MLA_KEY_CHUNK)
    st_next = _dot(k_ref[chunk(0), :], qt)
    for ci in range(n_chunks):
        ks = chunk(ci)
        st = st_next
        if ci + 1 < n_chunks:
            st_next = _dot(k_ref[chunk(ci + 1), :], qt)
        m_new = jnp.maximum(m, jnp.max(st, axis=0, keepdims=True))
        alpha = jnp.exp2((m - m_new) * c)
        p = jnp.exp2((st - m_new) * c).astype(BF16)
        vt1 = jnp.concatenate([vt_ref[:, ks], ones], axis=0)
        acc = acc * alpha + _dot(vt1, p)
        m = m_new
    o_ref[...] = (acc[:V_DIM] * (1.0 / acc[V_DIM:V_DIM + 1])).T.astype(BF16)


def _mla_attn(q_pad_t, k_pad, v_t):
    tq = 2048
    nq = SEQ // tq
    return pl.pallas_call(
        _mla_attn_kernel,
        grid=(BATCH, HB, nq),
        in_specs=[pl.BlockSpec((MLA_PAD, tq), lambda b, h, i: (h, b * nq + i)),
                  pl.BlockSpec((SEQ, MLA_PAD), lambda b, h, i: (b, h)),
                  pl.BlockSpec((V_DIM, SEQ), lambda b, h, i: (h, b))],
        out_specs=pl.BlockSpec((tq, V_DIM), lambda b, h, i: (b * nq + i, h)),
        out_shape=jax.ShapeDtypeStruct((TOKENS, HB * V_DIM), BF16),
        compiler_params=_params(("parallel", "parallel", "arbitrary")),
        name="mla_attn",
    )(q_pad_t, k_pad, v_t)


MIX_COL_CHUNK = 512


def _mix_out_kernel(x_ref, h_ref, gt_ref, oa_ref, ob_ref, woa_ref, wob_ref, wg_ref, bg_ref, wout_ref, o_ref):
    h = h_ref[...]
    o_a = oa_ref[...]
    o_b = ob_ref[...]

    def merge(c):
        a_cols = slice(c, c + MIX_COL_CHUNK)
        b_cols = slice(D_MODEL + c, D_MODEL + c + MIX_COL_CHUNK)
        g_a = jax.nn.sigmoid(_dot(h, wg_ref[:, a_cols]) + bg_ref[:, a_cols])
        g_b = jax.nn.sigmoid(_dot(h, wg_ref[:, b_cols]) + bg_ref[:, b_cols])
        return (g_a * _dot(o_a, woa_ref[:, a_cols]) + g_b * _dot(o_b, wob_ref[:, a_cols])).astype(BF16)

    acc = None
    mix = merge(0)
    for c in range(0, D_MODEL, MIX_COL_CHUNK):
        nxt = merge(c + MIX_COL_CHUNK) if c + MIX_COL_CHUNK < D_MODEL else None
        part = _dot(mix, wout_ref[c:c + MIX_COL_CHUNK, :])
        acc = part if acc is None else acc + part
        mix = nxt
    o_ref[...] = x_ref[...] + gt_ref[...] * acc


def _mix_out(x, h, mod, k_gate, o_a, o_b, w_oa, w_ob, w_gate, b_gate, w_out):
    tm = 256
    row = lambda i: (i, 0)
    resident = lambda a: pl.BlockSpec(a.shape, lambda i: (0, 0), pipeline_mode=pl.Buffered(1))
    return pl.pallas_call(
        _mix_out_kernel,
        grid=(TOKENS // tm,),
        in_specs=[pl.BlockSpec((tm, D_MODEL), row),
                  pl.BlockSpec((tm, D_MODEL), row),
                  _mod_spec(tm, k_gate),
                  pl.BlockSpec((tm, COLS_QA), row),
                  pl.BlockSpec((tm, HB * V_DIM), row),
                  resident(w_oa), resident(w_ob), resident(w_gate), resident(b_gate), resident(w_out)],
        out_specs=pl.BlockSpec((tm, D_MODEL), row),
        out_shape=jax.ShapeDtypeStruct((TOKENS, D_MODEL), F32),
        compiler_params=_params(("parallel",)),
        name="mix_out",
    )(x, h, mod, o_a, o_b, w_oa, w_ob, w_gate, b_gate, w_out)


def _pad_heads(w, n_heads, width, pad_to):
    k = w.shape[0]
    w = w.reshape(k, n_heads, width)
    w = jnp.pad(w, ((0, 0), (0, 0), (0, pad_to - width)))
    return w.reshape(k, n_heads * pad_to)


def _layer_weights(l, w_in, w_uq, w_ukv):
    w = w_in[l]
    kr_pad = jnp.pad(w[:, OFF_KR:OFF_GATE], ((0, 0), (0, LANES - QK_ROPE)))
    w_x = jnp.concatenate([w[:, :OFF_KR], kr_pad], axis=1).astype(BF16)
    w_gate = w[:, OFF_GATE:].astype(BF16)
    w_uq_pad = _pad_heads(w_uq[l], HB, QK_NOPE + QK_ROPE, MLA_PAD).astype(BF16)
    ukv = w_ukv[l].reshape(KV_RANK, HB, QK_NOPE + V_DIM)
    w_ukv_perm = jnp.concatenate([ukv[:, :, :QK_NOPE].reshape(KV_RANK, HB * QK_NOPE),
                                  ukv[:, :, QK_NOPE:].reshape(KV_RANK, HB * V_DIM)], axis=1).astype(BF16)
    return w_x, w_gate, w_uq_pad, w_ukv_perm


def kernel(x, c, positions, norm_g, w_ada, b_ada, w_ffn1_gu, w_ffn1_d, w_ffn2_gu, w_ffn2_d, w_in, b_gate, sink,
           g_cq, g_ckv, w_uq, w_ukv, w_oa, w_ob, w_out, g_final):
    tables = _rope_tables(positions)
    mods = _ada_mod(c, w_ada, b_ada)
    xt = x.reshape(TOKENS, D_MODEL)
    w_gu = w_ffn1_gu[0].astype(BF16)
    for l in range(DEPTH):
        mod = mods[l].reshape(BATCH, 1, N_MOD * D_MODEL)
        g = norm_g[l]
        w_x, w_gate, w_uq_pad, w_ukv_perm = _layer_weights(l, w_in, w_uq, w_ukv)

        act, w_d = _ffn_up(xt, g[0:1], mod, 0, w_gu, w_ffn1_d, l)
        xt, h, w_gu = _ffn_down(act, w_d, xt, mod, 2, g[1:2], 3, "mixer_norm", w_ffn2_gu, l)

        qa, ka, va, q_pad_t, k_pad, v_t = _in_proj(h, tables, w_x, g_cq[l][None, :], g_ckv[l][None, :],
                                                   w_uq_pad, w_ukv_perm)
        o_a = _win_attn(qa, ka, va, sink[l])
        o_b = _mla_attn(q_pad_t, k_pad, v_t)
        xt = _mix_out(xt, h, mod, 5, o_a, o_b, w_oa[l].astype(BF16), w_ob[l].astype(BF16), w_gate,
                      b_gate[l][None, :], w_out[l].astype(BF16))

        act, w_d = _ffn_up(xt, g[2:3], mod, 6, w_gu, w_ffn2_d, l)
        if l + 1 < DEPTH:
            xt, w_gu = _ffn_down(act, w_d, xt, mod, 8, g_final[None, :], 0, "plain", w_ffn1_gu, l + 1)
        else:
            xt, = _ffn_down(act, w_d, xt, mod, 8, g_final[None, :], 0, "final_norm")
    return xt.reshape(BATCH, SEQ, D_MODEL)
```

```python
import functools

import jax
import jax.numpy as jnp
import numpy as np
from jax import lax
from jax.experimental import pallas as pl
from jax.experimental.pallas import tpu as pltpu

D_MODEL = 2048
BATCH = 8
SEQ = 4096
DEPTH = 2
HA_Q = 8
HA_KV = 2
GROUP = HA_Q // HA_KV
HEAD_DIM = 128
WINDOW = 128
BLOCK = 128
HB = 8
QK_NOPE = 128
QK_ROPE = 64
V_DIM = 128
Q_RANK = 512
KV_RANK = 512
D_FF = 5632
ROPE_THETA = 10000.0
EPS = 1e-6
N_MOD = 9
TOKENS = BATCH * SEQ

COLS_QA = HA_Q * HEAD_DIM
COLS_KA = HA_KV * HEAD_DIM
COLS_VA = HA_KV * HEAD_DIM
OFF_CQ = COLS_QA + COLS_KA + COLS_VA
OFF_CKV = OFF_CQ + Q_RANK
OFF_KR = OFF_CKV + KV_RANK
OFF_GATE = OFF_KR + QK_ROPE

LANES = 128
MLA_PAD = 256
N_ROPE_TABLES = 5
VMEM_LIMIT = 56 * 1024 * 1024
LOG2E = float(np.log2(np.e))

BF16 = jnp.bfloat16
F32 = jnp.float32


def _params(sem):
    return pltpu.CompilerParams(dimension_semantics=sem, vmem_limit_bytes=VMEM_LIMIT)


def _dot(a, b):
    return jnp.dot(a, b, preferred_element_type=F32)


def _dot_nt(a, b):
    return lax.dot_general(a, b, (((1,), (1,)), ((), ())), preferred_element_type=F32)


def _rms(x, g):
    return (x * lax.rsqrt(jnp.mean(x * x, axis=-1, keepdims=True) + EPS)) * g


def _norm_mod(x, g, shift, scale):
    return _rms(x, g) * (1.0 + scale) + shift


def _cast_specs(w_stack, layer, n_steps, step_index):
    _, rows, cols = w_stack.shape
    assert rows % n_steps == 0
    slab = rows // n_steps
    in_spec = pl.BlockSpec((None, slab, cols), lambda *ids: (layer, step_index(*ids), 0))
    out_spec = pl.BlockSpec((slab, cols), lambda *ids: (step_index(*ids), 0))
    return in_spec, out_spec, jax.ShapeDtypeStruct((rows, cols), BF16)


def _rope_table_kernel(pos_ref, freq_ref, cast_ref, o_ref, cast_out_ref):
    cast_out_ref[...] = cast_ref[...].astype(BF16)
    ang = pos_ref[...].astype(F32) * freq_ref[...]
    cs = jnp.cos(ang)
    sn = jnp.sin(ang)
    lane = lax.broadcasted_iota(jnp.int32, cs.shape, 1)
    cs64 = pltpu.roll(cs, 64, 1)
    sn64 = pltpu.roll(sn, 64, 1)
    cs96 = pltpu.roll(cs, 96, 1)
    sn96 = pltpu.roll(sn, 96, 1)
    zero = jnp.zeros_like(cs)
    o_ref[:, 0 * LANES:1 * LANES] = jnp.where(lane < 64, cs, cs64)
    o_ref[:, 1 * LANES:2 * LANES] = jnp.where(lane < 64, -sn, sn64)
    o_ref[:, 2 * LANES:3 * LANES] = jnp.where(lane < 32, cs64, jnp.where(lane < 64, cs96, zero))
    o_ref[:, 3 * LANES:4 * LANES] = jnp.where(lane < 32, -sn64, zero)
    o_ref[:, 4 * LANES:5 * LANES] = jnp.where((lane >= 32) & (lane < 64), sn96, zero)


def _rope_tables(positions, w_gu_stack):
    tm = 1024
    ni = TOKENS // tm
    fa = ROPE_THETA ** (-jnp.arange(0, HEAD_DIM, 2, dtype=F32) / HEAD_DIM)
    fb = ROPE_THETA ** (-jnp.arange(0, QK_ROPE, 2, dtype=F32) / QK_ROPE)
    freq = jnp.concatenate([fa, fb, jnp.zeros((32,), F32)])[None, :]
    pos = positions.reshape(TOKENS, 1)
    cast_in, cast_out, cast_shape = _cast_specs(w_gu_stack, 0, ni, lambda i: i)
    return pl.pallas_call(
        _rope_table_kernel,
        grid=(ni,),
        in_specs=[pl.BlockSpec((tm, 1), lambda i: (i, 0)),
                  pl.BlockSpec((1, LANES), lambda i: (0, 0)),
                  cast_in],
        out_specs=[pl.BlockSpec((tm, N_ROPE_TABLES * LANES), lambda i: (i, 0)), cast_out],
        out_shape=[jax.ShapeDtypeStruct((TOKENS, N_ROPE_TABLES * LANES), F32), cast_shape],
        compiler_params=_params(("parallel",)),
        name="rope_tables",
    )(pos, freq, w_gu_stack)


def _ada_kernel(c_ref, w_ref, b_ref, o_ref):
    c = c_ref[...]
    c_act = (c * jax.nn.sigmoid(c)).astype(BF16)
    o_ref[...] = _dot(c_act, w_ref[...].astype(BF16)) + b_ref[...]


def _ada_mod(c, w_ada, b_ada):
    tn = 1024
    n = N_MOD * D_MODEL
    return pl.pallas_call(
        _ada_kernel,
        grid=(DEPTH, n // tn),
        in_specs=[pl.BlockSpec((BATCH, D_MODEL), lambda l, j: (0, 0)),
                  pl.BlockSpec((None, D_MODEL, tn), lambda l, j: (l, 0, j)),
                  pl.BlockSpec((None, 1, tn), lambda l, j: (l, 0, j))],
        out_specs=pl.BlockSpec((None, BATCH, tn), lambda l, j: (l, 0, j)),
        out_shape=jax.ShapeDtypeStruct((DEPTH, BATCH, n), F32),
        compiler_params=_params(("parallel", "parallel")),
        name="ada_mod",
    )(c, w_ada, b_ada.reshape(DEPTH, 1, n))


def _mod_spec(tm, k):
    return pl.BlockSpec((None, 1, D_MODEL), lambda i, *_: (i * tm // SEQ, 0, k))


FFN_ROW_CHUNK = 256


def _ffn_up_kernel(x_ref, g_ref, sh_ref, sc_ref, wg_ref, wu_ref, cast_ref, o_ref, cast_out_ref, h_ref):
    cast_out_ref[...] = cast_ref[...].astype(BF16)

    def swiglu(h):
        g = _dot(h, wg_ref[...])
        u = _dot(h, wu_ref[...])
        return (g * jax.nn.sigmoid(g) * u).astype(BF16)

    @pl.when(pl.program_id(1) == 0)
    def _():
        for r in range(0, x_ref.shape[0], FFN_ROW_CHUNK):
            rows = slice(r, r + FFN_ROW_CHUNK)
            h = _norm_mod(x_ref[rows, :], g_ref[...], sh_ref[...], sc_ref[...]).astype(BF16)
            h_ref[rows, :] = h
            o_ref[rows, :] = swiglu(h)

    @pl.when(pl.program_id(1) != 0)
    def _():
        o_ref[...] = swiglu(h_ref[...])


def _ffn_up(x, g, mod, k_shift, w_gu, w_d_stack, layer):
    tm, tn = 1024, 512
    nj = D_FF // tn
    ni = TOKENS // tm
    cast_in, cast_out, cast_shape = _cast_specs(w_d_stack, layer, ni * nj, lambda i, j: i * nj + j)
    return pl.pallas_call(
        _ffn_up_kernel,
        grid=(ni, nj),
        in_specs=[pl.BlockSpec((tm, D_MODEL), lambda i, j: (i, 0)),
                  pl.BlockSpec((1, D_MODEL), lambda i, j: (0, 0)),
                  _mod_spec(tm, k_shift), _mod_spec(tm, k_shift + 1),
                  pl.BlockSpec((D_MODEL, tn), lambda i, j: (0, j)),
                  pl.BlockSpec((D_MODEL, tn), lambda i, j: (0, j + nj)),
                  cast_in],
        out_specs=[pl.BlockSpec((tm, tn), lambda i, j: (i, j)), cast_out],
        out_shape=[jax.ShapeDtypeStruct((TOKENS, D_FF), BF16), cast_shape],
        scratch_shapes=[pltpu.VMEM((tm, D_MODEL), BF16)],
        compiler_params=_params(("parallel", "arbitrary")),
        name="ffn_up",
    )(x, g, mod, mod, w_gu, w_gu, w_d_stack)


FFN_DOWN_COL_CHUNK = 512


def _ffn_down_kernel(a_ref, w_ref, x_ref, gt_ref, g_ref, sh_ref, sc_ref, *refs, tail, cast):
    if cast:
        cast_ref, *out_refs, cast_out_ref = refs
        cast_out_ref[...] = cast_ref[...].astype(BF16)
    else:
        out_refs = refs
    x_out = out_refs[0]
    a = a_ref[...]
    for c in range(0, D_MODEL, FFN_DOWN_COL_CHUNK):
        cols = slice(c, c + FFN_DOWN_COL_CHUNK)
        x_out[:, cols] = x_ref[:, cols] + (0.5 * gt_ref[:, cols]) * _dot(a, w_ref[:, cols])
    if tail == "final_norm":
        x_out[...] = _rms(x_out[...], g_ref[...])
    elif tail == "mixer_norm":
        out_refs[1][...] = _norm_mod(x_out[...], g_ref[...], sh_ref[...], sc_ref[...]).astype(BF16)


def _ffn_down(act, w_d, x, mod, k_gate, g_tail, k_shift_tail, tail, w_gu_stack=None, layer=None):
    tm = 256
    ni = TOKENS // tm
    row = lambda i: (i, 0)
    x_spec = pl.BlockSpec((tm, D_MODEL), row)
    in_specs = [pl.BlockSpec((tm, D_FF), row),
                pl.BlockSpec((D_FF, D_MODEL), lambda i: (0, 0), pipeline_mode=pl.Buffered(1)),
                x_spec,
                _mod_spec(tm, k_gate),
                pl.BlockSpec((1, D_MODEL), lambda i: (0, 0)),
                _mod_spec(tm, k_shift_tail), _mod_spec(tm, k_shift_tail + 1)]
    operands = [act, w_d, x, mod, g_tail, mod, mod]
    out_specs = [x_spec]
    out_shape = [jax.ShapeDtypeStruct((TOKENS, D_MODEL), F32)]
    if tail == "mixer_norm":
        out_specs.append(pl.BlockSpec((tm, D_MODEL), row))
        out_shape.append(jax.ShapeDtypeStruct((TOKENS, D_MODEL), BF16))
    cast = w_gu_stack is not None
    if cast:
        cast_in, cast_out, cast_shape = _cast_specs(w_gu_stack, layer, ni, lambda i: i)
        in_specs.append(cast_in)
        operands.append(w_gu_stack)
        out_specs.append(cast_out)
        out_shape.append(cast_shape)
    return pl.pallas_call(
        functools.partial(_ffn_down_kernel, tail=tail, cast=cast),
        grid=(ni,),
        in_specs=in_specs,
        out_specs=out_specs,
        out_shape=out_shape,
        compiler_params=_params(("parallel",)),
        name="ffn_down",
    )(*operands)


def _in_proj_kernel(h_ref, tab_ref, wx_ref, gcq_ref, gckv_ref, wuq_ref, wukv_ref,
                    qa_ref, ka_ref, va_ref, qp_ref, kp_ref, vb_ref):
    h = h_ref[...]
    cos_a = tab_ref[:, 0 * LANES:1 * LANES]
    sin_a = tab_ref[:, 1 * LANES:2 * LANES]
    cos_b = tab_ref[:, 2 * LANES:3 * LANES]
    sin_lo = tab_ref[:, 3 * LANES:4 * LANES]
    sin_hi = tab_ref[:, 4 * LANES:5 * LANES]

    def rope_a(t):
        return t * cos_a + pltpu.roll(t, 64, 1) * sin_a

    def rope_b(t):
        return t * cos_b + pltpu.roll(t, 96, 1) * sin_lo + pltpu.roll(t, 32, 1) * sin_hi

    proj = _dot(h, wx_ref[...])

    for hd in range(HA_Q):
        sl = slice(hd * HEAD_DIM, (hd + 1) * HEAD_DIM)
        qa_ref[:, sl] = rope_a(proj[:, sl]).astype(BF16)
    for hd in range(HA_KV):
        sl = slice(hd * HEAD_DIM, (hd + 1) * HEAD_DIM)
        ka_ref[:, sl] = rope_a(proj[:, COLS_QA + hd * HEAD_DIM:COLS_QA + (hd + 1) * HEAD_DIM]).astype(BF16)
    va_ref[...] = proj[:, COLS_QA + COLS_KA:OFF_CQ].astype(BF16)

    cq = _rms(proj[:, OFF_CQ:OFF_CKV], gcq_ref[...]).astype(BF16)
    q_all = _dot(cq, wuq_ref[...])
    for hd in range(HB):
        base = hd * MLA_PAD
        qp_ref[base:base + QK_NOPE, :] = q_all[:, base:base + QK_NOPE].T.astype(BF16)
        qp_ref[base + QK_NOPE:base + MLA_PAD, :] = rope_b(q_all[:, base + QK_NOPE:base + MLA_PAD]).T.astype(BF16)

    ckv = _rms(proj[:, OFF_CKV:OFF_KR], gckv_ref[...]).astype(BF16)
    kv_all = _dot(ckv, wukv_ref[...])
    kr = rope_b(proj[:, OFF_KR:OFF_KR + LANES]).astype(BF16)
    for hd in range(HB):
        base = hd * MLA_PAD
        kp_ref[:, base:base + QK_NOPE] = kv_all[:, hd * QK_NOPE:(hd + 1) * QK_NOPE].astype(BF16)
        kp_ref[:, base + QK_NOPE:base + MLA_PAD] = kr
        vsl = slice(HB * QK_NOPE + hd * V_DIM, HB * QK_NOPE + (hd + 1) * V_DIM)
        vb_ref[hd * V_DIM:(hd + 1) * V_DIM, :] = kv_all[:, vsl].T.astype(BF16)


def _in_proj(h, tables, w_x, g_cq, g_ckv, w_uq, w_ukv):
    tm = 512
    const = lambda i: (0, 0)
    row = lambda i: (i, 0)
    nx = w_x.shape[1]
    col = lambda i: (0, i)
    resident = lambda shape: pl.BlockSpec(shape, const, pipeline_mode=pl.Buffered(1))
    outs = [(COLS_QA, True), (COLS_KA, True), (COLS_VA, True), (HB * MLA_PAD, False),
            (HB * MLA_PAD, True), (HB * V_DIM, False)]
    out_specs = [pl.BlockSpec((tm, n), row) if tok else pl.BlockSpec((n, tm), col) for n, tok in outs]
    out_shape = [jax.ShapeDtypeStruct((TOKENS, n) if tok else (n, TOKENS), BF16) for n, tok in outs]
    return pl.pallas_call(
        _in_proj_kernel,
        grid=(TOKENS // tm,),
        in_specs=[pl.BlockSpec((tm, D_MODEL), row),
                  pl.BlockSpec((tm, N_ROPE_TABLES * LANES), row),
                  resident((D_MODEL, nx)),
                  pl.BlockSpec((1, Q_RANK), const),
                  pl.BlockSpec((1, KV_RANK), const),
                  resident((Q_RANK, HB * MLA_PAD)),
                  resident((KV_RANK, HB * (QK_NOPE + V_DIM)))],
        out_specs=out_specs,
        out_shape=out_shape,
        compiler_params=_params(("parallel",)),
        name="in_proj",
    )(h, tables, w_x, g_cq, g_ckv, w_uq, w_ukv)


WIN_BLOCKS_PER_STEP = 8
WIN_KEYS = 3 * BLOCK
WIN_ROWS = GROUP * BLOCK


def _win_mask_bias():
    rel = jnp.arange(WIN_KEYS)[None, :] - (jnp.arange(WIN_ROWS) % BLOCK)[:, None]
    off = -BLOCK * jnp.arange(3)[:, None, None]
    return jnp.where(jnp.abs(rel[None] + off) <= WINDOW, 0.0, -1e30).astype(F32)


def _win_attn_kernel(sink_ref, bias_ref, q_ref, k_ref, v_ref, o_ref):
    hk = pl.program_id(1)
    nb = SEQ // BLOCK
    c = HEAD_DIM ** -0.5 * LOG2E
    row = lax.broadcasted_iota(jnp.int32, (WIN_ROWS, 1), 0)
    sk = jnp.zeros((WIN_ROWS, 1), F32)
    for g in range(GROUP):
        sk = jnp.where(row // BLOCK == g, sink_ref[hk * GROUP + g] * LOG2E, sk)
    for j in range(WIN_BLOCKS_PER_STEP):
        n = pl.program_id(2) * WIN_BLOCKS_PER_STEP + j
        first = jnp.clip(n - 1, 0, nb - 3)
        start = pl.multiple_of(first * BLOCK, BLOCK)
        kw = k_ref[pl.ds(start, WIN_KEYS), :]
        vw = v_ref[pl.ds(start, WIN_KEYS), :]
        q = jnp.concatenate([q_ref[j * BLOCK:(j + 1) * BLOCK, g * HEAD_DIM:(g + 1) * HEAD_DIM]
                             for g in range(GROUP)], axis=0)
        t = _dot_nt(q, kw) * c + bias_ref[n - first]
        m = jnp.maximum(jnp.max(t, axis=-1, keepdims=True), sk)
        p = jnp.exp2(t - m)
        denom = jnp.sum(p, axis=-1, keepdims=True) + jnp.exp2(sk - m)
        o = (_dot(p.astype(BF16), vw) * (1.0 / denom)).astype(BF16)
        for g in range(GROUP):
            o_ref[j * BLOCK:(j + 1) * BLOCK, g * HEAD_DIM:(g + 1) * HEAD_DIM] = o[g * BLOCK:(g + 1) * BLOCK]


def _win_attn(qa, ka, va, sink):
    tq = WIN_BLOCKS_PER_STEP * BLOCK
    nq = SEQ // tq
    gw = GROUP * HEAD_DIM
    return pl.pallas_call(
        _win_attn_kernel,
        grid=(BATCH, HA_KV, nq),
        in_specs=[pl.BlockSpec(memory_space=pltpu.SMEM),
                  pl.BlockSpec((3, WIN_ROWS, WIN_KEYS), lambda b, h, n: (0, 0, 0)),
                  pl.BlockSpec((tq, gw), lambda b, h, n: (b * nq + n, h)),
                  pl.BlockSpec((SEQ, HEAD_DIM), lambda b, h, n: (b, h)),
                  pl.BlockSpec((SEQ, HEAD_DIM), lambda b, h, n: (b, h))],
        out_specs=pl.BlockSpec((tq, gw), lambda b, h, n: (b * nq + n, h)),
        out_shape=jax.ShapeDtypeStruct((TOKENS, COLS_QA), BF16),
        compiler_params=_params(("parallel", "parallel", "arbitrary")),
        name="win_attn",
    )(sink, _win_mask_bias(), qa, ka, va)


MLA_KEY_CHUNK = 512
MLA_ONES_ROWS = 16


def _mla_attn_kernel(qt_ref, k_ref, vt_ref, o_ref):
    c = (QK_NOPE + QK_ROPE) ** -0.5 * LOG2E
    qt = qt_ref[...]
    tq = qt.shape[1]
    m = jnp.full((1, tq), -1e30, F32)
    acc = jnp.zeros((V_DIM + MLA_ONES_ROWS, tq), F32)
    ones = jnp.ones((MLA_ONES_ROWS, MLA_KEY_CHUNK), BF16)
    n_chunks = SEQ // MLA_KEY_CHUNK
    chunk = lambda ci: slice(ci * MLA_KEY_CHUNK, (ci + 1) * MLA_KEY_CHUNK)
    st_next = _dot(k_ref[chunk(0), :], qt)
    for ci in range(n_chunks):
        ks = chunk(ci)
        st = st_next
        if ci + 1 < n_chunks:
            st_next = _dot(k_ref[chunk(ci + 1), :], qt)
        m_new = jnp.maximum(m, jnp.max(st, axis=0, keepdims=True))
        alpha = jnp.exp2((m - m_new) * c)
        p = jnp.exp2((st - m_new) * c).astype(BF16)
        vt1 = jnp.concatenate([vt_ref[:, ks], ones], axis=0)
        acc = acc * alpha + _dot(vt1, p)
        m = m_new
    o_ref[...] = (acc[:V_DIM] * (1.0 / acc[V_DIM:V_DIM + 1])).T.astype(BF16)


def _mla_attn(q_pad_t, k_pad, v_t):
    tq = 2048
    nq = SEQ // tq
    return pl.pallas_call(
        _mla_attn_kernel,
        grid=(BATCH, HB, nq),
        in_specs=[pl.BlockSpec((MLA_PAD, tq), lambda b, h, i: (h, b * nq + i)),
                  pl.BlockSpec((SEQ, MLA_PAD), lambda b, h, i: (b, h)),
                  pl.BlockSpec((V_DIM, SEQ), lambda b, h, i: (h, b))],
        out_specs=pl.BlockSpec((tq, V_DIM), lambda b, h, i: (b * nq + i, h)),
        out_shape=jax.ShapeDtypeStruct((TOKENS, HB * V_DIM), BF16),
        compiler_params=_params(("parallel", "parallel", "arbitrary")),
        name="mla_attn",
    )(q_pad_t, k_pad, v_t)


MIX_COL_CHUNK = 512


def _mix_out_kernel(x_ref, h_ref, gt_ref, oa_ref, ob_ref, woa_ref, wob_ref, wg_ref, bg_ref, wout_ref, o_ref):
    h = h_ref[...]
    o_a = oa_ref[...]
    o_b = ob_ref[...]

    def merge(c):
        a_cols = slice(c, c + MIX_COL_CHUNK)
        b_cols = slice(D_MODEL + c, D_MODEL + c + MIX_COL_CHUNK)
        g_a = jax.nn.sigmoid(_dot(h, wg_ref[:, a_cols]) + bg_ref[:, a_cols])
        g_b = jax.nn.sigmoid(_dot(h, wg_ref[:, b_cols]) + bg_ref[:, b_cols])
        return (g_a * _dot(o_a, woa_ref[:, a_cols]) + g_b * _dot(o_b, wob_ref[:, a_cols])).astype(BF16)

    acc = None
    mix = merge(0)
    for c in range(0, D_MODEL, MIX_COL_CHUNK):
        nxt = merge(c + MIX_COL_CHUNK) if c + MIX_COL_CHUNK < D_MODEL else None
        part = _dot(mix, wout_ref[c:c + MIX_COL_CHUNK, :])
        acc = part if acc is None else acc + part
        mix = nxt
    o_ref[...] = x_ref[...] + gt_ref[...] * acc


def _mix_out(x, h, mod, k_gate, o_a, o_b, w_oa, w_ob, w_gate, b_gate, w_out):
    tm = 256
    row = lambda i: (i, 0)
    resident = lambda a: pl.BlockSpec(a.shape, lambda i: (0, 0), pipeline_mode=pl.Buffered(1))
    return pl.pallas_call(
        _mix_out_kernel,
        grid=(TOKENS // tm,),
        in_specs=[pl.BlockSpec((tm, D_MODEL), row),
                  pl.BlockSpec((tm, D_MODEL), row),
                  _mod_spec(tm, k_gate),
                  pl.BlockSpec((tm, COLS_QA), row),
                  pl.BlockSpec((tm, HB * V_DIM), row),
                  resident(w_oa), resident(w_ob), resident(w_gate), resident(b_gate), resident(w_out)],
        out_specs=pl.BlockSpec((tm, D_MODEL), row),
        out_shape=jax.ShapeDtypeStruct((TOKENS, D_MODEL), F32),
        compiler_params=_params(("parallel",)),
        name="mix_out",
    )(x, h, mod, o_a, o_b, w_oa, w_ob, w_gate, b_gate, w_out)


def _pad_heads(w, n_heads, width, pad_to):
    k = w.shape[0]
    w = w.reshape(k, n_heads, width)
    w = jnp.pad(w, ((0, 0), (0, 0), (0, pad_to - width)))
    return w.reshape(k, n_heads * pad_to)


def _layer_weights(l, w_in, w_uq, w_ukv):
    w = w_in[l].astype(BF16)
    w_x = jnp.pad(w[:, :OFF_GATE], ((0, 0), (0, LANES - QK_ROPE)))
    w_gate = w[:, OFF_GATE:]
    w_uq_pad = _pad_heads(w_uq[l], HB, QK_NOPE + QK_ROPE, MLA_PAD).astype(BF16)
    ukv = w_ukv[l].reshape(KV_RANK, HB, QK_NOPE + V_DIM)
    w_ukv_perm = jnp.concatenate([ukv[:, :, :QK_NOPE].reshape(KV_RANK, HB * QK_NOPE),
                                  ukv[:, :, QK_NOPE:].reshape(KV_RANK, HB * V_DIM)], axis=1).astype(BF16)
    return w_x, w_gate, w_uq_pad, w_ukv_perm


def kernel(x, c, positions, norm_g, w_ada, b_ada, w_ffn1_gu, w_ffn1_d, w_ffn2_gu, w_ffn2_d, w_in, b_gate, sink,
           g_cq, g_ckv, w_uq, w_ukv, w_oa, w_ob, w_out, g_final):
    tables, w_gu = _rope_tables(positions, w_ffn1_gu)
    mods = _ada_mod(c, w_ada, b_ada)
    xt = x.reshape(TOKENS, D_MODEL)
    for l in range(DEPTH):
        mod = mods[l].reshape(BATCH, 1, N_MOD * D_MODEL)
        g = norm_g[l]
        w_x, w_gate, w_uq_pad, w_ukv_perm = _layer_weights(l, w_in, w_uq, w_ukv)

        act, w_d = _ffn_up(xt, g[0:1], mod, 0, w_gu, w_ffn1_d, l)
        xt, h, w_gu = _ffn_down(act, w_d, xt, mod, 2, g[1:2], 3, "mixer_norm", w_ffn2_gu, l)

        qa, ka, va, q_pad_t, k_pad, v_t = _in_proj(h, tables, w_x, g_cq[l][None, :], g_ckv[l][None, :],
                                                   w_uq_pad, w_ukv_perm)
        o_a = _win_attn(qa, ka, va, sink[l])
        o_b = _mla_attn(q_pad_t, k_pad, v_t)
        xt = _mix_out(xt, h, mod, 5, o_a, o_b, w_oa[l].astype(BF16), w_ob[l].astype(BF16), w_gate,
                      b_gate[l][None, :], w_out[l].astype(BF16))

        act, w_d = _ffn_up(xt, g[2:3], mod, 6, w_gu, w_ffn2_d, l)
        if l + 1 < DEPTH:
            xt, w_gu = _ffn_down(act, w_d, xt, mod, 8, g_final[None, :], 0, "plain", w_ffn1_gu, l + 1)
        else:
            xt, = _ffn_down(act, w_d, xt, mod, 8, g_final[None, :], 0, "final_norm")
    return xt.reshape(BATCH, SEQ, D_MODEL)
```

```python
import functools

import jax
import jax.numpy as jnp
import numpy as np
from jax import lax
from jax.experimental import pallas as pl
from jax.experimental.pallas import tpu as pltpu

D_MODEL = 2048
BATCH = 8
SEQ = 4096
DEPTH = 2
HA_Q = 8
HA_KV = 2
GROUP = HA_Q // HA_KV
HEAD_DIM = 128
WINDOW = 128
BLOCK = 128
HB = 8
QK_NOPE = 128
QK_ROPE = 64
V_DIM = 128
Q_RANK = 512
KV_RANK = 512
D_FF = 5632
ROPE_THETA = 10000.0
EPS = 1e-6
N_MOD = 9
TOKENS = BATCH * SEQ

COLS_QA = HA_Q * HEAD_DIM
COLS_KA = HA_KV * HEAD_DIM
COLS_VA = HA_KV * HEAD_DIM
OFF_CQ = COLS_QA + COLS_KA + COLS_VA
OFF_CKV = OFF_CQ + Q_RANK
OFF_KR = OFF_CKV + KV_RANK
OFF_GATE = OFF_KR + QK_ROPE

LANES = 128
MLA_PAD = 256
N_ROPE_TABLES = 5
VMEM_LIMIT = 56 * 1024 * 1024
LOG2E = float(np.log2(np.e))
MLA_Q_SCALE = (QK_NOPE + QK_ROPE) ** -0.5 * LOG2E
WIN_Q_SCALE = HEAD_DIM ** -0.5 * LOG2E

BF16 = jnp.bfloat16
F32 = jnp.float32


def _params(sem):
    return pltpu.CompilerParams(dimension_semantics=sem, vmem_limit_bytes=VMEM_LIMIT)


def _dot(a, b):
    return jnp.dot(a, b, preferred_element_type=F32)


def _dot_nt(a, b):
    return lax.dot_general(a, b, (((1,), (1,)), ((), ())), preferred_element_type=F32)


def _rms(x, g):
    return (x * lax.rsqrt(jnp.mean(x * x, axis=-1, keepdims=True) + EPS)) * g


def _norm_mod(x, g, shift, scale):
    return _rms(x, g) * (1.0 + scale) + shift


def _cast_specs(w_stack, layer, n_steps, step_index):
    _, rows, cols = w_stack.shape
    assert rows % n_steps == 0
    slab = rows // n_steps
    in_spec = pl.BlockSpec((None, slab, cols), lambda *ids: (layer, step_index(*ids), 0))
    out_spec = pl.BlockSpec((slab, cols), lambda *ids: (step_index(*ids), 0))
    return in_spec, out_spec, jax.ShapeDtypeStruct((rows, cols), BF16)


def _rope_table_kernel(pos_ref, freq_ref, cast_ref, o_ref, cast_out_ref):
    cast_out_ref[...] = cast_ref[...].astype(BF16)
    ang = pos_ref[...].astype(F32) * freq_ref[...]
    cs = jnp.cos(ang)
    sn = jnp.sin(ang)
    lane = lax.broadcasted_iota(jnp.int32, cs.shape, 1)
    cs64 = pltpu.roll(cs, 64, 1)
    sn64 = pltpu.roll(sn, 64, 1)
    cs96 = pltpu.roll(cs, 96, 1)
    sn96 = pltpu.roll(sn, 96, 1)
    zero = jnp.zeros_like(cs)
    o_ref[:, 0 * LANES:1 * LANES] = jnp.where(lane < 64, cs, cs64)
    o_ref[:, 1 * LANES:2 * LANES] = jnp.where(lane < 64, -sn, sn64)
    o_ref[:, 2 * LANES:3 * LANES] = jnp.where(lane < 32, cs64, jnp.where(lane < 64, cs96, zero))
    o_ref[:, 3 * LANES:4 * LANES] = jnp.where(lane < 32, -sn64, zero)
    o_ref[:, 4 * LANES:5 * LANES] = jnp.where((lane >= 32) & (lane < 64), sn96, zero)


def _rope_tables(positions, w_gu_stack):
    tm = 1024
    ni = TOKENS // tm
    fa = ROPE_THETA ** (-jnp.arange(0, HEAD_DIM, 2, dtype=F32) / HEAD_DIM)
    fb = ROPE_THETA ** (-jnp.arange(0, QK_ROPE, 2, dtype=F32) / QK_ROPE)
    freq = jnp.concatenate([fa, fb, jnp.zeros((32,), F32)])[None, :]
    pos = positions.reshape(TOKENS, 1)
    cast_in, cast_out, cast_shape = _cast_specs(w_gu_stack, 0, ni, lambda i: i)
    return pl.pallas_call(
        _rope_table_kernel,
        grid=(ni,),
        in_specs=[pl.BlockSpec((tm, 1), lambda i: (i, 0)),
                  pl.BlockSpec((1, LANES), lambda i: (0, 0)),
                  cast_in],
        out_specs=[pl.BlockSpec((tm, N_ROPE_TABLES * LANES), lambda i: (i, 0)), cast_out],
        out_shape=[jax.ShapeDtypeStruct((TOKENS, N_ROPE_TABLES * LANES), F32), cast_shape],
        compiler_params=_params(("parallel",)),
        name="rope_tables",
    )(pos, freq, w_gu_stack)


def _ada_kernel(c_ref, w_ref, b_ref, o_ref):
    c = c_ref[...]
    c_act = (c * jax.nn.sigmoid(c)).astype(BF16)
    o_ref[...] = _dot(c_act, w_ref[...].astype(BF16)) + b_ref[...]


def _ada_mod(c, w_ada, b_ada):
    tn = 1024
    n = N_MOD * D_MODEL
    return pl.pallas_call(
        _ada_kernel,
        grid=(DEPTH, n // tn),
        in_specs=[pl.BlockSpec((BATCH, D_MODEL), lambda l, j: (0, 0)),
                  pl.BlockSpec((None, D_MODEL, tn), lambda l, j: (l, 0, j)),
                  pl.BlockSpec((None, 1, tn), lambda l, j: (l, 0, j))],
        out_specs=pl.BlockSpec((None, BATCH, tn), lambda l, j: (l, 0, j)),
        out_shape=jax.ShapeDtypeStruct((DEPTH, BATCH, n), F32),
        compiler_params=_params(("parallel", "parallel")),
        name="ada_mod",
    )(c, w_ada, b_ada.reshape(DEPTH, 1, n))


def _mod_spec(tm, k):
    return pl.BlockSpec((None, 1, D_MODEL), lambda i, *_: (i * tm // SEQ, 0, k))


FFN_ROW_CHUNK = 256


def _ffn_up_kernel(x_ref, g_ref, sh_ref, sc_ref, wg_ref, wu_ref, cast_ref, o_ref, cast_out_ref, h_ref):
    cast_out_ref[...] = cast_ref[...].astype(BF16)

    def swiglu(h):
        g = _dot(h, wg_ref[...])
        u = _dot(h, wu_ref[...])
        return (g * jax.nn.sigmoid(g) * u).astype(BF16)

    @pl.when(pl.program_id(1) == 0)
    def _():
        for r in range(0, x_ref.shape[0], FFN_ROW_CHUNK):
            rows = slice(r, r + FFN_ROW_CHUNK)
            h = _norm_mod(x_ref[rows, :], g_ref[...], sh_ref[...], sc_ref[...]).astype(BF16)
            h_ref[rows, :] = h
            o_ref[rows, :] = swiglu(h)

    @pl.when(pl.program_id(1) != 0)
    def _():
        o_ref[...] = swiglu(h_ref[...])


def _ffn_up(x, g, mod, k_shift, w_gu, w_d_stack, layer):
    tm, tn = 1024, 512
    nj = D_FF // tn
    ni = TOKENS // tm
    cast_in, cast_out, cast_shape = _cast_specs(w_d_stack, layer, ni * nj, lambda i, j: i * nj + j)
    return pl.pallas_call(
        _ffn_up_kernel,
        grid=(ni, nj),
        in_specs=[pl.BlockSpec((tm, D_MODEL), lambda i, j: (i, 0)),
                  pl.BlockSpec((1, D_MODEL), lambda i, j: (0, 0)),
                  _mod_spec(tm, k_shift), _mod_spec(tm, k_shift + 1),
                  pl.BlockSpec((D_MODEL, tn), lambda i, j: (0, j)),
                  pl.BlockSpec((D_MODEL, tn), lambda i, j: (0, j + nj)),
                  cast_in],
        out_specs=[pl.BlockSpec((tm, tn), lambda i, j: (i, j)), cast_out],
        out_shape=[jax.ShapeDtypeStruct((TOKENS, D_FF), BF16), cast_shape],
        scratch_shapes=[pltpu.VMEM((tm, D_MODEL), BF16)],
        compiler_params=_params(("parallel", "arbitrary")),
        name="ffn_up",
    )(x, g, mod, mod, w_gu, w_gu, w_d_stack)


FFN_DOWN_COL_CHUNK = 512


def _ffn_down_kernel(a_ref, w_ref, x_ref, gt_ref, g_ref, sh_ref, sc_ref, *refs, tail, cast):
    if cast:
        cast_ref, *out_refs, cast_out_ref = refs
        cast_out_ref[...] = cast_ref[...].astype(BF16)
    else:
        out_refs = refs
    x_out = out_refs[0]
    a = a_ref[...]
    for c in range(0, D_MODEL, FFN_DOWN_COL_CHUNK):
        cols = slice(c, c + FFN_DOWN_COL_CHUNK)
        x_out[:, cols] = x_ref[:, cols] + (0.5 * gt_ref[:, cols]) * _dot(a, w_ref[:, cols])
    if tail == "final_norm":
        x_out[...] = _rms(x_out[...], g_ref[...])
    elif tail == "mixer_norm":
        out_refs[1][...] = _norm_mod(x_out[...], g_ref[...], sh_ref[...], sc_ref[...]).astype(BF16)


def _ffn_down(act, w_d, x, mod, k_gate, g_tail, k_shift_tail, tail, w_gu_stack=None, layer=None):
    tm = 256
    ni = TOKENS // tm
    row = lambda i: (i, 0)
    x_spec = pl.BlockSpec((tm, D_MODEL), row)
    in_specs = [pl.BlockSpec((tm, D_FF), row),
                pl.BlockSpec((D_FF, D_MODEL), lambda i: (0, 0), pipeline_mode=pl.Buffered(1)),
                x_spec,
                _mod_spec(tm, k_gate),
                pl.BlockSpec((1, D_MODEL), lambda i: (0, 0)),
                _mod_spec(tm, k_shift_tail), _mod_spec(tm, k_shift_tail + 1)]
    operands = [act, w_d, x, mod, g_tail, mod, mod]
    out_specs = [x_spec]
    out_shape = [jax.ShapeDtypeStruct((TOKENS, D_MODEL), F32)]
    if tail == "mixer_norm":
        out_specs.append(pl.BlockSpec((tm, D_MODEL), row))
        out_shape.append(jax.ShapeDtypeStruct((TOKENS, D_MODEL), BF16))
    cast = w_gu_stack is not None
    if cast:
        cast_in, cast_out, cast_shape = _cast_specs(w_gu_stack, layer, ni, lambda i: i)
        in_specs.append(cast_in)
        operands.append(w_gu_stack)
        out_specs.append(cast_out)
        out_shape.append(cast_shape)
    return pl.pallas_call(
        functools.partial(_ffn_down_kernel, tail=tail, cast=cast),
        grid=(ni,),
        in_specs=in_specs,
        out_specs=out_specs,
        out_shape=out_shape,
        compiler_params=_params(("parallel",)),
        name="ffn_down",
    )(*operands)


def _in_proj_kernel(h_ref, tab_ref, wx_ref, gcq_ref, gckv_ref, wuq_ref, wukv_ref,
                    qa_ref, ka_ref, va_ref, qp_ref, kp_ref, vb_ref):
    h = h_ref[...]
    cos_a = tab_ref[:, 0 * LANES:1 * LANES]
    sin_a = tab_ref[:, 1 * LANES:2 * LANES]
    cos_b = tab_ref[:, 2 * LANES:3 * LANES]
    sin_lo = tab_ref[:, 3 * LANES:4 * LANES]
    sin_hi = tab_ref[:, 4 * LANES:5 * LANES]

    def rope_a(t):
        return t * cos_a + pltpu.roll(t, 64, 1) * sin_a

    def rope_b(t):
        return t * cos_b + pltpu.roll(t, 96, 1) * sin_lo + pltpu.roll(t, 32, 1) * sin_hi

    proj = _dot(h, wx_ref[...])

    for hd in range(HA_Q):
        sl = slice(hd * HEAD_DIM, (hd + 1) * HEAD_DIM)
        qa_ref[:, sl] = (rope_a(proj[:, sl]) * WIN_Q_SCALE).astype(BF16)
    for hd in range(HA_KV):
        sl = slice(hd * HEAD_DIM, (hd + 1) * HEAD_DIM)
        ka_ref[:, sl] = rope_a(proj[:, COLS_QA + hd * HEAD_DIM:COLS_QA + (hd + 1) * HEAD_DIM]).astype(BF16)
    va_ref[...] = proj[:, COLS_QA + COLS_KA:OFF_CQ].astype(BF16)

    cq = _rms(proj[:, OFF_CQ:OFF_CKV], gcq_ref[...]).astype(BF16)
    q_all = _dot(cq, wuq_ref[...]) * MLA_Q_SCALE
    for hd in range(HB):
        base = hd * MLA_PAD
        qp_ref[base:base + QK_NOPE, :] = q_all[:, base:base + QK_NOPE].T.astype(BF16)
        qp_ref[base + QK_NOPE:base + MLA_PAD, :] = rope_b(q_all[:, base + QK_NOPE:base + MLA_PAD]).T.astype(BF16)

    ckv = _rms(proj[:, OFF_CKV:OFF_KR], gckv_ref[...]).astype(BF16)
    kv_all = _dot(ckv, wukv_ref[...])
    kr = rope_b(proj[:, OFF_KR:OFF_KR + LANES]).astype(BF16)
    for hd in range(HB):
        base = hd * MLA_PAD
        kp_ref[:, base:base + QK_NOPE] = kv_all[:, hd * QK_NOPE:(hd + 1) * QK_NOPE].astype(BF16)
        kp_ref[:, base + QK_NOPE:base + MLA_PAD] = kr
        vsl = slice(HB * QK_NOPE + hd * V_DIM, HB * QK_NOPE + (hd + 1) * V_DIM)
        vb_ref[hd * V_DIM:(hd + 1) * V_DIM, :] = kv_all[:, vsl].T.astype(BF16)


def _in_proj(h, tables, w_x, g_cq, g_ckv, w_uq, w_ukv):
    tm = 512
    const = lambda i: (0, 0)
    row = lambda i: (i, 0)
    nx = w_x.shape[1]
    col = lambda i: (0, i)
    resident = lambda shape: pl.BlockSpec(shape, const, pipeline_mode=pl.Buffered(1))
    outs = [(COLS_QA, True), (COLS_KA, True), (COLS_VA, True), (HB * MLA_PAD, False),
            (HB * MLA_PAD, True), (HB * V_DIM, False)]
    out_specs = [pl.BlockSpec((tm, n), row) if tok else pl.BlockSpec((n, tm), col) for n, tok in outs]
    out_shape = [jax.ShapeDtypeStruct((TOKENS, n) if tok else (n, TOKENS), BF16) for n, tok in outs]
    return pl.pallas_call(
        _in_proj_kernel,
        grid=(TOKENS // tm,),
        in_specs=[pl.BlockSpec((tm, D_MODEL), row),
                  pl.BlockSpec((tm, N_ROPE_TABLES * LANES), row),
                  resident((D_MODEL, nx)),
                  pl.BlockSpec((1, Q_RANK), const),
                  pl.BlockSpec((1, KV_RANK), const),
                  resident((Q_RANK, HB * MLA_PAD)),
                  resident((KV_RANK, HB * (QK_NOPE + V_DIM)))],
        out_specs=out_specs,
        out_shape=out_shape,
        compiler_params=_params(("parallel",)),
        name="in_proj",
    )(h, tables, w_x, g_cq, g_ckv, w_uq, w_ukv)


WIN_BLOCKS_PER_STEP = 8
WIN_KEYS = 3 * BLOCK
WIN_ROWS = GROUP * BLOCK


def _win_mask_bias():
    rel = jnp.arange(WIN_KEYS)[None, :] - (jnp.arange(WIN_ROWS) % BLOCK)[:, None]
    off = -BLOCK * jnp.arange(3)[:, None, None]
    return jnp.where(jnp.abs(rel[None] + off) <= WINDOW, 0.0, -1e30).astype(F32)


def _win_attn_kernel(sink_ref, bias_ref, q_ref, k_ref, v_ref, o_ref):
    hk = pl.program_id(1)
    nb = SEQ // BLOCK
    row = lax.broadcasted_iota(jnp.int32, (WIN_ROWS, 1), 0)
    sk = jnp.zeros((WIN_ROWS, 1), F32)
    for g in range(GROUP):
        sk = jnp.where(row // BLOCK == g, sink_ref[hk * GROUP + g] * LOG2E, sk)
    for j in range(WIN_BLOCKS_PER_STEP):
        n = pl.program_id(2) * WIN_BLOCKS_PER_STEP + j
        first = jnp.clip(n - 1, 0, nb - 3)
        start = pl.multiple_of(first * BLOCK, BLOCK)
        kw = k_ref[pl.ds(start, WIN_KEYS), :]
        vw = v_ref[pl.ds(start, WIN_KEYS), :]
        q = jnp.concatenate([q_ref[j * BLOCK:(j + 1) * BLOCK, g * HEAD_DIM:(g + 1) * HEAD_DIM]
                             for g in range(GROUP)], axis=0)
        t = _dot_nt(q, kw) + bias_ref[n - first]
        m = jnp.maximum(jnp.max(t, axis=-1, keepdims=True), sk)
        p = jnp.exp2(t - m)
        denom = jnp.sum(p, axis=-1, keepdims=True) + jnp.exp2(sk - m)
        o = (_dot(p.astype(BF16), vw) * (1.0 / denom)).astype(BF16)
        for g in range(GROUP):
            o_ref[j * BLOCK:(j + 1) * BLOCK, g * HEAD_DIM:(g + 1) * HEAD_DIM] = o[g * BLOCK:(g + 1) * BLOCK]


def _win_attn(qa, ka, va, sink):
    tq = WIN_BLOCKS_PER_STEP * BLOCK
    nq = SEQ // tq
    gw = GROUP * HEAD_DIM
    return pl.pallas_call(
        _win_attn_kernel,
        grid=(BATCH, HA_KV, nq),
        in_specs=[pl.BlockSpec(memory_space=pltpu.SMEM),
                  pl.BlockSpec((3, WIN_ROWS, WIN_KEYS), lambda b, h, n: (0, 0, 0)),
                  pl.BlockSpec((tq, gw), lambda b, h, n: (b * nq + n, h)),
                  pl.BlockSpec((SEQ, HEAD_DIM), lambda b, h, n: (b, h)),
                  pl.BlockSpec((SEQ, HEAD_DIM), lambda b, h, n: (b, h))],
        out_specs=pl.BlockSpec((tq, gw), lambda b, h, n: (b * nq + n, h)),
        out_shape=jax.ShapeDtypeStruct((TOKENS, COLS_QA), BF16),
        compiler_params=_params(("parallel", "parallel", "arbitrary")),
        name="win_attn",
    )(sink, _win_mask_bias(), qa, ka, va)


MLA_KEY_CHUNK = 512
MLA_ONES_ROWS = 16


def _mla_attn_kernel(qt_ref, k_ref, vt_ref, o_ref):
    qt = qt_ref[...]
    tq = qt.shape[1]
    m = jnp.full((1, tq), -1e30, F32)
    acc = jnp.zeros((V_DIM + MLA_ONES_ROWS, tq), F32)
    ones = jnp.ones((MLA_ONES_ROWS, MLA_KEY_CHUNK), BF16)
    n_chunks = SEQ // MLA_KEY_CHUNK
    chunk = lambda ci: slice(ci * MLA_KEY_CHUNK, (ci + 1) * MLA_KEY_CHUNK)
    st_next = _dot(k_ref[chunk(0), :], qt)
    for ci in range(n_chunks):
        ks = chunk(ci)
        st = st_next
        if ci + 1 < n_chunks:
            st_next = _dot(k_ref[chunk(ci + 1), :], qt)
        m_new = jnp.maximum(m, jnp.max(st, axis=0, keepdims=True))
        alpha = jnp.exp2(m - m_new)
        p = jnp.exp2(st - m_new).astype(BF16)
        vt1 = jnp.concatenate([vt_ref[:, ks], ones], axis=0)
        acc = acc * alpha + _dot(vt1, p)
        m = m_new
    o_ref[...] = (acc[:V_DIM] * (1.0 / acc[V_DIM:V_DIM + 1])).T.astype(BF16)


def _mla_attn(q_pad_t, k_pad, v_t):
    tq = 2048
    nq = SEQ // tq
    return pl.pallas_call(
        _mla_attn_kernel,
        grid=(BATCH, HB, nq),
        in_specs=[pl.BlockSpec((MLA_PAD, tq), lambda b, h, i: (h, b * nq + i)),
                  pl.BlockSpec((SEQ, MLA_PAD), lambda b, h, i: (b, h)),
                  pl.BlockSpec((V_DIM, SEQ), lambda b, h, i: (h, b))],
        out_specs=pl.BlockSpec((tq, V_DIM), lambda b, h, i: (b * nq + i, h)),
        out_shape=jax.ShapeDtypeStruct((TOKENS, HB * V_DIM), BF16),
        compiler_params=_params(("parallel", "parallel", "arbitrary")),
        name="mla_attn",
    )(q_pad_t, k_pad, v_t)


MIX_COL_CHUNK = 512


def _mix_out_kernel(x_ref, h_ref, gt_ref, oa_ref, ob_ref, woa_ref, wob_ref, wg_ref, bg_ref, wout_ref, o_ref):
    h = h_ref[...]
    o_a = oa_ref[...]
    o_b = ob_ref[...]

    def merge(c):
        a_cols = slice(c, c + MIX_COL_CHUNK)
        b_cols = slice(D_MODEL + c, D_MODEL + c + MIX_COL_CHUNK)
        g_a = jax.nn.sigmoid(_dot(h, wg_ref[:, a_cols]) + bg_ref[:, a_cols])
        g_b = jax.nn.sigmoid(_dot(h, wg_ref[:, b_cols]) + bg_ref[:, b_cols])
        return (g_a * _dot(o_a, woa_ref[:, a_cols]) + g_b * _dot(o_b, wob_ref[:, a_cols])).astype(BF16)

    acc = None
    mix = merge(0)
    for c in range(0, D_MODEL, MIX_COL_CHUNK):
        nxt = merge(c + MIX_COL_CHUNK) if c + MIX_COL_CHUNK < D_MODEL else None
        part = _dot(mix, wout_ref[c:c + MIX_COL_CHUNK, :])
        acc = part if acc is None else acc + part
        mix = nxt
    o_ref[...] = x_ref[...] + gt_ref[...] * acc


def _mix_out(x, h, mod, k_gate, o_a, o_b, w_oa, w_ob, w_gate, b_gate, w_out):
    tm = 256
    row = lambda i: (i, 0)
    resident = lambda a: pl.BlockSpec(a.shape, lambda i: (0, 0), pipeline_mode=pl.Buffered(1))
    return pl.pallas_call(
        _mix_out_kernel,
        grid=(TOKENS // tm,),
        in_specs=[pl.BlockSpec((tm, D_MODEL), row),
                  pl.BlockSpec((tm, D_MODEL), row),
                  _mod_spec(tm, k_gate),
                  pl.BlockSpec((tm, COLS_QA), row),
                  pl.BlockSpec((tm, HB * V_DIM), row),
                  resident(w_oa), resident(w_ob), resident(w_gate), resident(b_gate), resident(w_out)],
        out_specs=pl.BlockSpec((tm, D_MODEL), row),
        out_shape=jax.ShapeDtypeStruct((TOKENS, D_MODEL), F32),
        compiler_params=_params(("parallel",)),
        name="mix_out",
    )(x, h, mod, o_a, o_b, w_oa, w_ob, w_gate, b_gate, w_out)


def _pad_heads(w, n_heads, width, pad_to):
    k = w.shape[0]
    w = w.reshape(k, n_heads, width)
    w = jnp.pad(w, ((0, 0), (0, 0), (0, pad_to - width)))
    return w.reshape(k, n_heads * pad_to)


def _layer_weights(l, w_in, w_uq, w_ukv):
    w = w_in[l].astype(BF16)
    w_x = jnp.pad(w[:, :OFF_GATE], ((0, 0), (0, LANES - QK_ROPE)))
    w_gate = w[:, OFF_GATE:]
    w_uq_pad = _pad_heads(w_uq[l], HB, QK_NOPE + QK_ROPE, MLA_PAD).astype(BF16)
    ukv = w_ukv[l].reshape(KV_RANK, HB, QK_NOPE + V_DIM)
    w_ukv_perm = jnp.concatenate([ukv[:, :, :QK_NOPE].reshape(KV_RANK, HB * QK_NOPE),
                                  ukv[:, :, QK_NOPE:].reshape(KV_RANK, HB * V_DIM)], axis=1).astype(BF16)
    return w_x, w_gate, w_uq_pad, w_ukv_perm


def kernel(x, c, positions, norm_g, w_ada, b_ada, w_ffn1_gu, w_ffn1_d, w_ffn2_gu, w_ffn2_d, w_in, b_gate, sink,
           g_cq, g_ckv, w_uq, w_ukv, w_oa, w_ob, w_out, g_final):
    tables, w_gu = _rope_tables(positions, w_ffn1_gu)
    mods = _ada_mod(c, w_ada, b_ada)
    xt = x.reshape(TOKENS, D_MODEL)
    for l in range(DEPTH):
        mod = mods[l].reshape(BATCH, 1, N_MOD * D_MODEL)
        g = norm_g[l]
        w_x, w_gate, w_uq_pad, w_ukv_perm = _layer_weights(l, w_in, w_uq, w_ukv)

        act, w_d = _ffn_up(xt, g[0:1], mod, 0, w_gu, w_ffn1_d, l)
        xt, h, w_gu = _ffn_down(act, w_d, xt, mod, 2, g[1:2], 3, "mixer_norm", w_ffn2_gu, l)

        qa, ka, va, q_pad_t, k_pad, v_t = _in_proj(h, tables, w_x, g_cq[l][None, :], g_ckv[l][None, :],
                                                   w_uq_pad, w_ukv_perm)
        o_a = _win_attn(qa, ka, va, sink[l])
        o_b = _mla_attn(q_pad_t, k_pad, v_t)
        xt = _mix_out(xt, h, mod, 5, o_a, o_b, w_oa[l].astype(BF16), w_ob[l].astype(BF16), w_gate,
                      b_gate[l][None, :], w_out[l].astype(BF16))

        act, w_d = _ffn_up(xt, g[2:3], mod, 6, w_gu, w_ffn2_d, l)
        if l + 1 < DEPTH:
            xt, w_gu = _ffn_down(act, w_d, xt, mod, 8, g_final[None, :], 0, "plain", w_ffn1_gu, l + 1)
        else:
            xt, = _ffn_down(act, w_d, xt, mod, 8, g_final[None, :], 0, "final_norm")
    return xt.reshape(BATCH, SEQ, D_MODEL)
```

```python
import functools

import jax
import jax.numpy as jnp
import numpy as np
from jax import lax
from jax.experimental import pallas as pl
from jax.experimental.pallas import tpu as pltpu

D_MODEL = 2048
BATCH = 8
SEQ = 4096
DEPTH = 2
HA_Q = 8
HA_KV = 2
GROUP = HA_Q // HA_KV
HEAD_DIM = 128
WINDOW = 128
BLOCK = 128
HB = 8
QK_NOPE = 128
QK_ROPE = 64
V_DIM = 128
Q_RANK = 512
KV_RANK = 512
D_FF = 5632
ROPE_THETA = 10000.0
EPS = 1e-6
N_MOD = 9
TOKENS = BATCH * SEQ

COLS_QA = HA_Q * HEAD_DIM
COLS_KA = HA_KV * HEAD_DIM
COLS_VA = HA_KV * HEAD_DIM
OFF_CQ = COLS_QA + COLS_KA + COLS_VA
OFF_CKV = OFF_CQ + Q_RANK
OFF_KR = OFF_CKV + KV_RANK
OFF_GATE = OFF_KR + QK_ROPE

LANES = 128
MLA_PAD = 256
N_ROPE_TABLES = 5
VMEM_LIMIT = 56 * 1024 * 1024
LOG2E = float(np.log2(np.e))
MLA_Q_SCALE = (QK_NOPE + QK_ROPE) ** -0.5 * LOG2E
WIN_Q_SCALE = HEAD_DIM ** -0.5 * LOG2E

BF16 = jnp.bfloat16
F32 = jnp.float32


def _params(sem):
    return pltpu.CompilerParams(dimension_semantics=sem, vmem_limit_bytes=VMEM_LIMIT)


def _dot(a, b):
    return jnp.dot(a, b, preferred_element_type=F32)


def _dot_nt(a, b):
    return lax.dot_general(a, b, (((1,), (1,)), ((), ())), preferred_element_type=F32)


def _rms(x, g):
    return (x * lax.rsqrt(jnp.mean(x * x, axis=-1, keepdims=True) + EPS)) * g


def _norm_mod(x, g, shift, scale):
    return _rms(x, g) * (1.0 + scale) + shift


def _cast_specs(w_stack, layer, n_steps, step_index):
    _, rows, cols = w_stack.shape
    assert rows % n_steps == 0
    slab = rows // n_steps
    in_spec = pl.BlockSpec((None, slab, cols), lambda *ids: (layer, step_index(*ids), 0))
    out_spec = pl.BlockSpec((slab, cols), lambda *ids: (step_index(*ids), 0))
    return in_spec, out_spec, jax.ShapeDtypeStruct((rows, cols), BF16)


HALF_A = HEAD_DIM // 2
HALF_B = QK_ROPE // 2


def _prep_kernel(pos_ref, freq_ref, c_ref, wada_ref, bada_ref, cast_ref, tab_ref, mod_ref, cast_out_ref):
    cast_out_ref[...] = cast_ref[...].astype(BF16)
    c = c_ref[...]
    c_act = (c * jax.nn.sigmoid(c)).astype(BF16)
    mod_ref[...] = _dot(c_act, wada_ref[...].astype(BF16)) + bada_ref[...]

    ang = pos_ref[...].astype(F32) * freq_ref[...]
    cs = jnp.cos(ang)
    sn = jnp.sin(ang)
    lane = lax.broadcasted_iota(jnp.int32, cs.shape, 1)
    cs_a, sn_a = pltpu.roll(cs, HALF_A, 1), pltpu.roll(sn, HALF_A, 1)
    cs_b, sn_b = pltpu.roll(cs, LANES - HALF_B, 1), pltpu.roll(sn, LANES - HALF_B, 1)
    zero = jnp.zeros_like(cs)
    tab_ref[:, 0 * LANES:1 * LANES] = jnp.where(lane < HALF_A, cs, cs_a)
    tab_ref[:, 1 * LANES:2 * LANES] = jnp.where(lane < HALF_A, -sn, sn_a)
    tab_ref[:, 2 * LANES:3 * LANES] = jnp.where(lane < HALF_B, cs_a, jnp.where(lane < QK_ROPE, cs_b, zero))
    tab_ref[:, 3 * LANES:4 * LANES] = jnp.where(lane < HALF_B, -sn_a, zero)
    tab_ref[:, 4 * LANES:5 * LANES] = jnp.where((lane >= HALF_B) & (lane < QK_ROPE), sn_b, zero)


def _prep(positions, c, w_ada, b_ada, w_gu_stack):
    tm = 1024
    ni = TOKENS // tm
    n = N_MOD * D_MODEL
    tiles_per_layer = ni // DEPTH
    tn = n // tiles_per_layer
    assert ni % DEPTH == 0 and n % tiles_per_layer == 0 and tn % LANES == 0
    fa = ROPE_THETA ** (-jnp.arange(0, HEAD_DIM, 2, dtype=F32) / HEAD_DIM)
    fb = ROPE_THETA ** (-jnp.arange(0, QK_ROPE, 2, dtype=F32) / QK_ROPE)
    freq = jnp.concatenate([fa, fb, jnp.zeros((LANES - HALF_A - HALF_B,), F32)])[None, :]
    pos = positions.reshape(TOKENS, 1)
    ada_tile = lambda i: (i // tiles_per_layer, 0, i % tiles_per_layer)
    cast_in, cast_out, cast_shape = _cast_specs(w_gu_stack, 0, ni, lambda i: i)
    return pl.pallas_call(
        _prep_kernel,
        grid=(ni,),
        in_specs=[pl.BlockSpec((tm, 1), lambda i: (i, 0)),
                  pl.BlockSpec((1, LANES), lambda i: (0, 0)),
                  pl.BlockSpec((BATCH, D_MODEL), lambda i: (0, 0)),
                  pl.BlockSpec((None, D_MODEL, tn), ada_tile),
                  pl.BlockSpec((None, 1, tn), ada_tile),
                  cast_in],
        out_specs=[pl.BlockSpec((tm, N_ROPE_TABLES * LANES), lambda i: (i, 0)),
                   pl.BlockSpec((None, BATCH, tn), ada_tile),
                   cast_out],
        out_shape=[jax.ShapeDtypeStruct((TOKENS, N_ROPE_TABLES * LANES), F32),
                   jax.ShapeDtypeStruct((DEPTH, BATCH, n), F32),
                   cast_shape],
        compiler_params=_params(("parallel",)),
        name="prep",
    )(pos, freq, c, w_ada, b_ada.reshape(DEPTH, 1, n), w_gu_stack)


def _mod_spec(tm, k):
    return pl.BlockSpec((None, 1, D_MODEL), lambda i, *_: (i * tm // SEQ, 0, k))


FFN_ROW_CHUNK = 256


def _ffn_up_kernel(x_ref, g_ref, sh_ref, sc_ref, wg_ref, wu_ref, cast_ref, o_ref, cast_out_ref, h_ref):
    cast_out_ref[...] = cast_ref[...].astype(BF16)

    def swiglu(h):
        g = _dot(h, wg_ref[...])
        u = _dot(h, wu_ref[...])
        return (g * jax.nn.sigmoid(g) * u).astype(BF16)

    @pl.when(pl.program_id(1) == 0)
    def _():
        for r in range(0, x_ref.shape[0], FFN_ROW_CHUNK):
            rows = slice(r, r + FFN_ROW_CHUNK)
            h = _norm_mod(x_ref[rows, :], g_ref[...], sh_ref[...], sc_ref[...]).astype(BF16)
            h_ref[rows, :] = h
            o_ref[rows, :] = swiglu(h)

    @pl.when(pl.program_id(1) != 0)
    def _():
        o_ref[...] = swiglu(h_ref[...])


def _ffn_up(x, g, mod, k_shift, w_gu, w_d_stack, layer):
    tm, tn = 1024, 512
    nj = D_FF // tn
    ni = TOKENS // tm
    cast_in, cast_out, cast_shape = _cast_specs(w_d_stack, layer, ni * nj, lambda i, j: i * nj + j)
    return pl.pallas_call(
        _ffn_up_kernel,
        grid=(ni, nj),
        in_specs=[pl.BlockSpec((tm, D_MODEL), lambda i, j: (i, 0)),
                  pl.BlockSpec((1, D_MODEL), lambda i, j: (0, 0)),
                  _mod_spec(tm, k_shift), _mod_spec(tm, k_shift + 1),
                  pl.BlockSpec((D_MODEL, tn), lambda i, j: (0, j)),
                  pl.BlockSpec((D_MODEL, tn), lambda i, j: (0, j + nj)),
                  cast_in],
        out_specs=[pl.BlockSpec((tm, tn), lambda i, j: (i, j)), cast_out],
        out_shape=[jax.ShapeDtypeStruct((TOKENS, D_FF), BF16), cast_shape],
        scratch_shapes=[pltpu.VMEM((tm, D_MODEL), BF16)],
        compiler_params=_params(("parallel", "arbitrary")),
        name="ffn_up",
    )(x, g, mod, mod, w_gu, w_gu, w_d_stack)


FFN_DOWN_COL_CHUNK = 512


def _ffn_down_kernel(a_ref, w_ref, x_ref, gt_ref, g_ref, sh_ref, sc_ref, *refs, tail, cast):
    if cast:
        cast_ref, *out_refs, cast_out_ref = refs
        cast_out_ref[...] = cast_ref[...].astype(BF16)
    else:
        out_refs = refs
    x_out = out_refs[0]
    a = a_ref[...]
    for c in range(0, D_MODEL, FFN_DOWN_COL_CHUNK):
        cols = slice(c, c + FFN_DOWN_COL_CHUNK)
        x_out[:, cols] = x_ref[:, cols] + (0.5 * gt_ref[:, cols]) * _dot(a, w_ref[:, cols])
    if tail == "final_norm":
        x_out[...] = _rms(x_out[...], g_ref[...])
    elif tail == "mixer_norm":
        out_refs[1][...] = _norm_mod(x_out[...], g_ref[...], sh_ref[...], sc_ref[...]).astype(BF16)


def _ffn_down(act, w_d, x, mod, k_gate, g_tail, k_shift_tail, tail, w_gu_stack=None, layer=None):
    tm = 256
    ni = TOKENS // tm
    row = lambda i: (i, 0)
    x_spec = pl.BlockSpec((tm, D_MODEL), row)
    in_specs = [pl.BlockSpec((tm, D_FF), row),
                pl.BlockSpec((D_FF, D_MODEL), lambda i: (0, 0), pipeline_mode=pl.Buffered(1)),
                x_spec,
                _mod_spec(tm, k_gate),
                pl.BlockSpec((1, D_MODEL), lambda i: (0, 0)),
                _mod_spec(tm, k_shift_tail), _mod_spec(tm, k_shift_tail + 1)]
    operands = [act, w_d, x, mod, g_tail, mod, mod]
    out_specs = [x_spec]
    out_shape = [jax.ShapeDtypeStruct((TOKENS, D_MODEL), F32)]
    if tail == "mixer_norm":
        out_specs.append(pl.BlockSpec((tm, D_MODEL), row))
        out_shape.append(jax.ShapeDtypeStruct((TOKENS, D_MODEL), BF16))
    cast = w_gu_stack is not None
    if cast:
        cast_in, cast_out, cast_shape = _cast_specs(w_gu_stack, layer, ni, lambda i: i)
        in_specs.append(cast_in)
        operands.append(w_gu_stack)
        out_specs.append(cast_out)
        out_shape.append(cast_shape)
    return pl.pallas_call(
        functools.partial(_ffn_down_kernel, tail=tail, cast=cast),
        grid=(ni,),
        in_specs=in_specs,
        out_specs=out_specs,
        out_shape=out_shape,
        compiler_params=_params(("parallel",)),
        name="ffn_down",
    )(*operands)


def _in_proj_kernel(h_ref, tab_ref, wx_ref, gcq_ref, gckv_ref, wuq_ref, wukv_ref,
                    qa_ref, ka_ref, va_ref, qp_ref, kp_ref, vb_ref):
    h = h_ref[...]
    cos_a = tab_ref[:, 0 * LANES:1 * LANES]
    sin_a = tab_ref[:, 1 * LANES:2 * LANES]
    cos_b = tab_ref[:, 2 * LANES:3 * LANES]
    sin_lo = tab_ref[:, 3 * LANES:4 * LANES]
    sin_hi = tab_ref[:, 4 * LANES:5 * LANES]

    def rope_a(t):
        return t * cos_a + pltpu.roll(t, HALF_A, 1) * sin_a

    def rope_b(t):
        return t * cos_b + pltpu.roll(t, LANES - HALF_B, 1) * sin_lo + pltpu.roll(t, HALF_B, 1) * sin_hi

    proj = _dot(h, wx_ref[...])

    for hd in range(HA_Q):
        sl = slice(hd * HEAD_DIM, (hd + 1) * HEAD_DIM)
        qa_ref[:, sl] = (rope_a(proj[:, sl]) * WIN_Q_SCALE).astype(BF16)
    for hd in range(HA_KV):
        sl = slice(hd * HEAD_DIM, (hd + 1) * HEAD_DIM)
        ka_ref[:, sl] = rope_a(proj[:, COLS_QA + hd * HEAD_DIM:COLS_QA + (hd + 1) * HEAD_DIM]).astype(BF16)
    va_ref[...] = proj[:, COLS_QA + COLS_KA:OFF_CQ].astype(BF16)

    cq = _rms(proj[:, OFF_CQ:OFF_CKV], gcq_ref[...]).astype(BF16)
    q_all = _dot(cq, wuq_ref[...]) * MLA_Q_SCALE
    for hd in range(HB):
        base = hd * MLA_PAD
        qp_ref[base:base + QK_NOPE, :] = q_all[:, base:base + QK_NOPE].T.astype(BF16)
        qp_ref[base + QK_NOPE:base + MLA_PAD, :] = rope_b(q_all[:, base + QK_NOPE:base + MLA_PAD]).T.astype(BF16)

    ckv = _rms(proj[:, OFF_CKV:OFF_KR], gckv_ref[...]).astype(BF16)
    kv_all = _dot(ckv, wukv_ref[...])
    kr = rope_b(proj[:, OFF_KR:OFF_KR + LANES]).astype(BF16)
    for hd in range(HB):
        base = hd * MLA_PAD
        kp_ref[:, base:base + QK_NOPE] = kv_all[:, hd * QK_NOPE:(hd + 1) * QK_NOPE].astype(BF16)
        kp_ref[:, base + QK_NOPE:base + MLA_PAD] = kr
        vsl = slice(HB * QK_NOPE + hd * V_DIM, HB * QK_NOPE + (hd + 1) * V_DIM)
        vb_ref[hd * V_DIM:(hd + 1) * V_DIM, :] = kv_all[:, vsl].T.astype(BF16)


def _in_proj(h, tables, w_x, g_cq, g_ckv, w_uq, w_ukv):
    tm = 512
    const = lambda i: (0, 0)
    row = lambda i: (i, 0)
    nx = w_x.shape[1]
    col = lambda i: (0, i)
    resident = lambda shape: pl.BlockSpec(shape, const, pipeline_mode=pl.Buffered(1))
    outs = [(COLS_QA, True), (COLS_KA, True), (COLS_VA, True), (HB * MLA_PAD, False),
            (HB * MLA_PAD, True), (HB * V_DIM, False)]
    out_specs = [pl.BlockSpec((tm, n), row) if tok else pl.BlockSpec((n, tm), col) for n, tok in outs]
    out_shape = [jax.ShapeDtypeStruct((TOKENS, n) if tok else (n, TOKENS), BF16) for n, tok in outs]
    return pl.pallas_call(
        _in_proj_kernel,
        grid=(TOKENS // tm,),
        in_specs=[pl.BlockSpec((tm, D_MODEL), row),
                  pl.BlockSpec((tm, N_ROPE_TABLES * LANES), row),
                  resident((D_MODEL, nx)),
                  pl.BlockSpec((1, Q_RANK), const),
                  pl.BlockSpec((1, KV_RANK), const),
                  resident((Q_RANK, HB * MLA_PAD)),
                  resident((KV_RANK, HB * (QK_NOPE + V_DIM)))],
        out_specs=out_specs,
        out_shape=out_shape,
        compiler_params=_params(("parallel",)),
        name="in_proj",
    )(h, tables, w_x, g_cq, g_ckv, w_uq, w_ukv)


WIN_KEYS = 3 * BLOCK
WIN_ROWS = GROUP * BLOCK


def _win_mask_bias():
    rel = jnp.arange(WIN_KEYS)[None, :] - (jnp.arange(WIN_ROWS) % BLOCK)[:, None]
    off = -BLOCK * jnp.arange(3)[:, None, None]
    return jnp.where(jnp.abs(rel[None] + off) <= WINDOW, 0.0, -1e30).astype(F32)


def _win_scores(n, j, bias_ref, q_ref, k_ref):
    nb = SEQ // BLOCK
    first = jnp.clip(n - 1, 0, nb - 3)
    start = pl.multiple_of(first * BLOCK, BLOCK)
    q = jnp.concatenate([q_ref[j * BLOCK:(j + 1) * BLOCK, g * HEAD_DIM:(g + 1) * HEAD_DIM]
                         for g in range(GROUP)], axis=0)
    return _dot_nt(q, k_ref[pl.ds(start, WIN_KEYS), :]) + bias_ref[n - first], start


def _win_finish(t, start, j, sk, v_ref, o_ref):
    m = jnp.maximum(jnp.max(t, axis=-1, keepdims=True), sk)
    p = jnp.exp2(t - m)
    denom = jnp.sum(p, axis=-1, keepdims=True) + jnp.exp2(sk - m)
    o = (_dot(p.astype(BF16), v_ref[pl.ds(start, WIN_KEYS), :]) * (1.0 / denom)).astype(BF16)
    for g in range(GROUP):
        o_ref[j * BLOCK:(j + 1) * BLOCK, g * HEAD_DIM:(g + 1) * HEAD_DIM] = o[g * BLOCK:(g + 1) * BLOCK]


MLA_KEY_CHUNK = 512
MLA_ONES_ROWS = 16
MLA_TQ = 2048
ATTN_STEPS_PER_BATCH = HB * (SEQ // MLA_TQ)
WIN_TILES_PER_KV = ATTN_STEPS_PER_BATCH // HA_KV
WIN_TQ = SEQ // WIN_TILES_PER_KV
MLA_CHUNKS_PER_WIN_BLOCK = (SEQ // MLA_KEY_CHUNK) // (WIN_TQ // BLOCK)


def _attn_kernel(sink_ref, bias_ref, qa_ref, ka_ref, va_ref, qt_ref, k_ref, vt_ref, oa_ref, ob_ref):
    h = pl.program_id(1)
    heads_per_kv = HB // HA_KV
    hk = h // heads_per_kv
    win_tile = (h % heads_per_kv) * pl.num_programs(2) + pl.program_id(2)
    row = lax.broadcasted_iota(jnp.int32, (WIN_ROWS, 1), 0)
    sk = jnp.zeros((WIN_ROWS, 1), F32)
    for g in range(GROUP):
        sk = jnp.where(row // BLOCK == g, sink_ref[hk * GROUP + g] * LOG2E, sk)

    qt = qt_ref[...]
    tq = qt.shape[1]
    m = jnp.full((1, tq), -1e30, F32)
    acc = jnp.zeros((V_DIM + MLA_ONES_ROWS, tq), F32)
    ones = jnp.ones((MLA_ONES_ROWS, MLA_KEY_CHUNK), BF16)
    n_chunks = SEQ // MLA_KEY_CHUNK
    chunk = lambda ci: slice(ci * MLA_KEY_CHUNK, (ci + 1) * MLA_KEY_CHUNK)
    st_next = _dot(k_ref[chunk(0), :], qt)
    win = None
    for ci in range(n_chunks):
        ks = chunk(ci)
        st = st_next
        if ci + 1 < n_chunks:
            st_next = _dot(k_ref[chunk(ci + 1), :], qt)
        j = ci // MLA_CHUNKS_PER_WIN_BLOCK
        if ci % MLA_CHUNKS_PER_WIN_BLOCK == 0:
            win = _win_scores(win_tile * (WIN_TQ // BLOCK) + j, j, bias_ref, qa_ref, ka_ref)
        m_new = jnp.maximum(m, jnp.max(st, axis=0, keepdims=True))
        alpha = jnp.exp2(m - m_new)
        p = jnp.exp2(st - m_new).astype(BF16)
        vt1 = jnp.concatenate([vt_ref[:, ks], ones], axis=0)
        acc = acc * alpha + _dot(vt1, p)
        m = m_new
        if ci % MLA_CHUNKS_PER_WIN_BLOCK == MLA_CHUNKS_PER_WIN_BLOCK - 1:
            _win_finish(*win, j, sk, va_ref, oa_ref)
    ob_ref[...] = (acc[:V_DIM] * (1.0 / acc[V_DIM:V_DIM + 1])).T.astype(BF16)


def _attn(qa, ka, va, sink, q_pad_t, k_pad, v_t):
    nq = SEQ // MLA_TQ
    heads_per_kv = HB // HA_KV
    gw = GROUP * HEAD_DIM
    win_rows = lambda b, h, i: (b * WIN_TILES_PER_KV + (h % heads_per_kv) * nq + i, h // heads_per_kv)
    win_kv = lambda b, h, i: (b, h // heads_per_kv)
    return pl.pallas_call(
        _attn_kernel,
        grid=(BATCH, HB, nq),
        in_specs=[pl.BlockSpec(memory_space=pltpu.SMEM),
                  pl.BlockSpec((3, WIN_ROWS, WIN_KEYS), lambda b, h, i: (0, 0, 0)),
                  pl.BlockSpec((WIN_TQ, gw), win_rows),
                  pl.BlockSpec((SEQ, HEAD_DIM), win_kv),
                  pl.BlockSpec((SEQ, HEAD_DIM), win_kv),
                  pl.BlockSpec((MLA_PAD, MLA_TQ), lambda b, h, i: (h, b * nq + i)),
                  pl.BlockSpec((SEQ, MLA_PAD), lambda b, h, i: (b, h)),
                  pl.BlockSpec((V_DIM, SEQ), lambda b, h, i: (h, b))],
        out_specs=[pl.BlockSpec((WIN_TQ, gw), win_rows),
                   pl.BlockSpec((MLA_TQ, V_DIM), lambda b, h, i: (b * nq + i, h))],
        out_shape=[jax.ShapeDtypeStruct((TOKENS, COLS_QA), BF16),
                   jax.ShapeDtypeStruct((TOKENS, HB * V_DIM), BF16)],
        compiler_params=_params(("parallel", "parallel", "arbitrary")),
        name="attn",
    )(sink, _win_mask_bias(), qa, ka, va, q_pad_t, k_pad, v_t)


MIX_COL_CHUNK = 512


def _mix_out_kernel(x_ref, h_ref, gt_ref, oa_ref, ob_ref, woa_ref, wob_ref, wg_ref, bg_ref, wout_ref, o_ref):
    h = h_ref[...]
    o_a = oa_ref[...]
    o_b = ob_ref[...]

    def merge(c):
        a_cols = slice(c, c + MIX_COL_CHUNK)
        b_cols = slice(D_MODEL + c, D_MODEL + c + MIX_COL_CHUNK)
        g_a = jax.nn.sigmoid(_dot(h, wg_ref[:, a_cols]) + bg_ref[:, a_cols])
        g_b = jax.nn.sigmoid(_dot(h, wg_ref[:, b_cols]) + bg_ref[:, b_cols])
        return (g_a * _dot(o_a, woa_ref[:, a_cols]) + g_b * _dot(o_b, wob_ref[:, a_cols])).astype(BF16)

    acc = None
    mix = merge(0)
    for c in range(0, D_MODEL, MIX_COL_CHUNK):
        nxt = merge(c + MIX_COL_CHUNK) if c + MIX_COL_CHUNK < D_MODEL else None
        part = _dot(mix, wout_ref[c:c + MIX_COL_CHUNK, :])
        acc = part if acc is None else acc + part
        mix = nxt
    o_ref[...] = x_ref[...] + gt_ref[...] * acc


def _mix_out(x, h, mod, k_gate, o_a, o_b, w_oa, w_ob, w_gate, b_gate, w_out):
    tm = 256
    row = lambda i: (i, 0)
    resident = lambda a: pl.BlockSpec(a.shape, lambda i: (0, 0), pipeline_mode=pl.Buffered(1))
    return pl.pallas_call(
        _mix_out_kernel,
        grid=(TOKENS // tm,),
        in_specs=[pl.BlockSpec((tm, D_MODEL), row),
                  pl.BlockSpec((tm, D_MODEL), row),
                  _mod_spec(tm, k_gate),
                  pl.BlockSpec((tm, COLS_QA), row),
                  pl.BlockSpec((tm, HB * V_DIM), row),
                  resident(w_oa), resident(w_ob), resident(w_gate), resident(b_gate), resident(w_out)],
        out_specs=pl.BlockSpec((tm, D_MODEL), row),
        out_shape=jax.ShapeDtypeStruct((TOKENS, D_MODEL), F32),
        compiler_params=_params(("parallel",)),
        name="mix_out",
    )(x, h, mod, o_a, o_b, w_oa, w_ob, w_gate, b_gate, w_out)


def _pad_heads(w, n_heads, width, pad_to):
    k = w.shape[0]
    w = w.reshape(k, n_heads, width)
    w = jnp.pad(w, ((0, 0), (0, 0), (0, pad_to - width)))
    return w.reshape(k, n_heads * pad_to)


def _layer_weights(l, w_in, w_uq, w_ukv):
    w = w_in[l].astype(BF16)
    w_x = jnp.pad(w[:, :OFF_GATE], ((0, 0), (0, LANES - QK_ROPE)))
    w_gate = w[:, OFF_GATE:]
    w_uq_pad = _pad_heads(w_uq[l], HB, QK_NOPE + QK_ROPE, MLA_PAD).astype(BF16)
    ukv = w_ukv[l].reshape(KV_RANK, HB, QK_NOPE + V_DIM)
    w_ukv_perm = jnp.concatenate([ukv[:, :, :QK_NOPE].reshape(KV_RANK, HB * QK_NOPE),
                                  ukv[:, :, QK_NOPE:].reshape(KV_RANK, HB * V_DIM)], axis=1).astype(BF16)
    return w_x, w_gate, w_uq_pad, w_ukv_perm


def kernel(x, c, positions, norm_g, w_ada, b_ada, w_ffn1_gu, w_ffn1_d, w_ffn2_gu, w_ffn2_d, w_in, b_gate, sink,
           g_cq, g_ckv, w_uq, w_ukv, w_oa, w_ob, w_out, g_final):
    tables, mods, w_gu = _prep(positions, c, w_ada, b_ada, w_ffn1_gu)
    xt = x.reshape(TOKENS, D_MODEL)
    for l in range(DEPTH):
        mod = mods[l].reshape(BATCH, 1, N_MOD * D_MODEL)
        g = norm_g[l]
        w_x, w_gate, w_uq_pad, w_ukv_perm = _layer_weights(l, w_in, w_uq, w_ukv)

        act, w_d = _ffn_up(xt, g[0:1], mod, 0, w_gu, w_ffn1_d, l)
        xt, h, w_gu = _ffn_down(act, w_d, xt, mod, 2, g[1:2], 3, "mixer_norm", w_ffn2_gu, l)

        qa, ka, va, q_pad_t, k_pad, v_t = _in_proj(h, tables, w_x, g_cq[l][None, :], g_ckv[l][None, :],
                                                   w_uq_pad, w_ukv_perm)
        o_a, o_b = _attn(qa, ka, va, sink[l], q_pad_t, k_pad, v_t)
        xt = _mix_out(xt, h, mod, 5, o_a, o_b, w_oa[l].astype(BF16), w_ob[l].astype(BF16), w_gate,
                      b_gate[l][None, :], w_out[l].astype(BF16))

        act, w_d = _ffn_up(xt, g[2:3], mod, 6, w_gu, w_ffn2_d, l)
        if l + 1 < DEPTH:
            xt, w_gu = _ffn_down(act, w_d, xt, mod, 8, g_final[None, :], 0, "plain", w_ffn1_gu, l + 1)
        else:
            xt, = _ffn_down(act, w_d, xt, mod, 8, g_final[None, :], 0, "final_norm")
    return xt.reshape(BATCH, SEQ, D_MODEL)
```

```python
import functools

import jax
import jax.numpy as jnp
import numpy as np
from jax import lax
from jax.experimental import pallas as pl
from jax.experimental.pallas import tpu as pltpu

D_MODEL = 2048
BATCH = 8
SEQ = 4096
DEPTH = 2
HA_Q = 8
HA_KV = 2
GROUP = HA_Q // HA_KV
HEAD_DIM = 128
WINDOW = 128
BLOCK = 128
HB = 8
QK_NOPE = 128
QK_ROPE = 64
V_DIM = 128
Q_RANK = 512
KV_RANK = 512
D_FF = 5632
ROPE_THETA = 10000.0
EPS = 1e-6
N_MOD = 9
TOKENS = BATCH * SEQ

COLS_QA = HA_Q * HEAD_DIM
COLS_KA = HA_KV * HEAD_DIM
COLS_VA = HA_KV * HEAD_DIM
OFF_CQ = COLS_QA + COLS_KA + COLS_VA
OFF_CKV = OFF_CQ + Q_RANK
OFF_KR = OFF_CKV + KV_RANK
OFF_GATE = OFF_KR + QK_ROPE

LANES = 128
MLA_PAD = 256
N_ROPE_TABLES = 5
VMEM_LIMIT = 56 * 1024 * 1024
LOG2E = float(np.log2(np.e))
MLA_Q_SCALE = (QK_NOPE + QK_ROPE) ** -0.5 * LOG2E
WIN_Q_SCALE = HEAD_DIM ** -0.5 * LOG2E

BF16 = jnp.bfloat16
F32 = jnp.float32


def _params(sem):
    return pltpu.CompilerParams(dimension_semantics=sem, vmem_limit_bytes=VMEM_LIMIT)


def _dot(a, b):
    return jnp.dot(a, b, preferred_element_type=F32)


def _dot_nt(a, b):
    return lax.dot_general(a, b, (((1,), (1,)), ((), ())), preferred_element_type=F32)


def _rms(x, g):
    return (x * lax.rsqrt(jnp.mean(x * x, axis=-1, keepdims=True) + EPS)) * g


def _norm_mod(x, g, shift, scale):
    return _rms(x, g) * (1.0 + scale) + shift


def _cast_specs(w_stack, layer, n_steps, step_index):
    _, rows, cols = w_stack.shape
    assert rows % n_steps == 0
    slab = rows // n_steps
    in_spec = pl.BlockSpec((None, slab, cols), lambda *ids: (layer, step_index(*ids), 0))
    out_spec = pl.BlockSpec((slab, cols), lambda *ids: (step_index(*ids), 0))
    return in_spec, out_spec, jax.ShapeDtypeStruct((rows, cols), BF16)


HALF_A = HEAD_DIM // 2
HALF_B = QK_ROPE // 2


def _prep_kernel(pos_ref, freq_ref, c_ref, wada_ref, bada_ref, cast_ref, tab_ref, mod_ref, cast_out_ref):
    cast_out_ref[...] = cast_ref[...].astype(BF16)
    c = c_ref[...]
    c_act = (c * jax.nn.sigmoid(c)).astype(BF16)
    mod_ref[...] = _dot(c_act, wada_ref[...].astype(BF16)) + bada_ref[...]

    ang = pos_ref[...].astype(F32) * freq_ref[...]
    cs = jnp.cos(ang)
    sn = jnp.sin(ang)
    lane = lax.broadcasted_iota(jnp.int32, cs.shape, 1)
    cs_a, sn_a = pltpu.roll(cs, HALF_A, 1), pltpu.roll(sn, HALF_A, 1)
    cs_b, sn_b = pltpu.roll(cs, LANES - HALF_B, 1), pltpu.roll(sn, LANES - HALF_B, 1)
    zero = jnp.zeros_like(cs)
    tab_ref[:, 0 * LANES:1 * LANES] = jnp.where(lane < HALF_A, cs, cs_a)
    tab_ref[:, 1 * LANES:2 * LANES] = jnp.where(lane < HALF_A, -sn, sn_a)
    tab_ref[:, 2 * LANES:3 * LANES] = jnp.where(lane < HALF_B, cs_a, jnp.where(lane < QK_ROPE, cs_b, zero))
    tab_ref[:, 3 * LANES:4 * LANES] = jnp.where(lane < HALF_B, -sn_a, zero)
    tab_ref[:, 4 * LANES:5 * LANES] = jnp.where((lane >= HALF_B) & (lane < QK_ROPE), sn_b, zero)


def _prep(positions, c, w_ada, b_ada, w_gu_stack):
    tm = 1024
    ni = TOKENS // tm
    n = N_MOD * D_MODEL
    tiles_per_layer = ni // DEPTH
    tn = n // tiles_per_layer
    assert ni % DEPTH == 0 and n % tiles_per_layer == 0 and tn % LANES == 0
    fa = ROPE_THETA ** (-jnp.arange(0, HEAD_DIM, 2, dtype=F32) / HEAD_DIM)
    fb = ROPE_THETA ** (-jnp.arange(0, QK_ROPE, 2, dtype=F32) / QK_ROPE)
    freq = jnp.concatenate([fa, fb, jnp.zeros((LANES - HALF_A - HALF_B,), F32)])[None, :]
    pos = positions.reshape(TOKENS, 1)
    ada_tile = lambda i: (i // tiles_per_layer, 0, i % tiles_per_layer)
    cast_in, cast_out, cast_shape = _cast_specs(w_gu_stack, 0, ni, lambda i: i)
    return pl.pallas_call(
        _prep_kernel,
        grid=(ni,),
        in_specs=[pl.BlockSpec((tm, 1), lambda i: (i, 0)),
                  pl.BlockSpec((1, LANES), lambda i: (0, 0)),
                  pl.BlockSpec((BATCH, D_MODEL), lambda i: (0, 0)),
                  pl.BlockSpec((None, D_MODEL, tn), ada_tile),
                  pl.BlockSpec((None, 1, tn), ada_tile),
                  cast_in],
        out_specs=[pl.BlockSpec((tm, N_ROPE_TABLES * LANES), lambda i: (i, 0)),
                   pl.BlockSpec((None, BATCH, tn), ada_tile),
                   cast_out],
        out_shape=[jax.ShapeDtypeStruct((TOKENS, N_ROPE_TABLES * LANES), F32),
                   jax.ShapeDtypeStruct((DEPTH, BATCH, n), F32),
                   cast_shape],
        compiler_params=_params(("parallel",)),
        name="prep",
    )(pos, freq, c, w_ada, b_ada.reshape(DEPTH, 1, n), w_gu_stack)


def _mod_spec(tm, k):
    return pl.BlockSpec((None, 1, D_MODEL), lambda i, *_: (i * tm // SEQ, 0, k))


FFN_ROW_CHUNK = 256


def _ffn_up_kernel(x_ref, g_ref, sh_ref, sc_ref, wg_ref, wu_ref, cast_ref, o_ref, cast_out_ref, h_ref):
    cast_out_ref[...] = cast_ref[...].astype(BF16)

    def swiglu(h):
        g = _dot(h, wg_ref[...])
        u = _dot(h, wu_ref[...])
        return (g * jax.nn.sigmoid(g) * u).astype(BF16)

    @pl.when(pl.program_id(1) == 0)
    def _():
        for r in range(0, x_ref.shape[0], FFN_ROW_CHUNK):
            rows = slice(r, r + FFN_ROW_CHUNK)
            h = _norm_mod(x_ref[rows, :], g_ref[...], sh_ref[...], sc_ref[...]).astype(BF16)
            h_ref[rows, :] = h
            o_ref[rows, :] = swiglu(h)

    @pl.when(pl.program_id(1) != 0)
    def _():
        o_ref[...] = swiglu(h_ref[...])


def _ffn_up(x, g, mod, k_shift, w_gu, w_d_stack, layer):
    tm, tn = 1024, 512
    nj = D_FF // tn
    ni = TOKENS // tm
    cast_in, cast_out, cast_shape = _cast_specs(w_d_stack, layer, ni * nj, lambda i, j: i * nj + j)
    return pl.pallas_call(
        _ffn_up_kernel,
        grid=(ni, nj),
        in_specs=[pl.BlockSpec((tm, D_MODEL), lambda i, j: (i, 0)),
                  pl.BlockSpec((1, D_MODEL), lambda i, j: (0, 0)),
                  _mod_spec(tm, k_shift), _mod_spec(tm, k_shift + 1),
                  pl.BlockSpec((D_MODEL, tn), lambda i, j: (0, j)),
                  pl.BlockSpec((D_MODEL, tn), lambda i, j: (0, j + nj)),
                  cast_in],
        out_specs=[pl.BlockSpec((tm, tn), lambda i, j: (i, j)), cast_out],
        out_shape=[jax.ShapeDtypeStruct((TOKENS, D_FF), BF16), cast_shape],
        scratch_shapes=[pltpu.VMEM((tm, D_MODEL), BF16)],
        compiler_params=_params(("parallel", "arbitrary")),
        name="ffn_up",
    )(x, g, mod, mod, w_gu, w_gu, w_d_stack)


FFN_DOWN_COL_CHUNK = 512


def _ffn_down_kernel(a_ref, w_ref, x_ref, gt_ref, g_ref, sh_ref, sc_ref, *refs, tail, cast):
    if cast:
        cast_ref, *out_refs, cast_out_ref = refs
        cast_out_ref[...] = cast_ref[...].astype(BF16)
    else:
        out_refs = refs
    x_out = out_refs[0]
    a = a_ref[...]
    for c in range(0, D_MODEL, FFN_DOWN_COL_CHUNK):
        cols = slice(c, c + FFN_DOWN_COL_CHUNK)
        x_out[:, cols] = x_ref[:, cols] + (0.5 * gt_ref[:, cols]) * _dot(a, w_ref[:, cols])
    if tail == "final_norm":
        x_out[...] = _rms(x_out[...], g_ref[...])
    elif tail == "mixer_norm":
        out_refs[1][...] = _norm_mod(x_out[...], g_ref[...], sh_ref[...], sc_ref[...]).astype(BF16)


def _ffn_down(act, w_d, x, mod, k_gate, g_tail, k_shift_tail, tail, w_gu_stack=None, layer=None):
    tm = 256
    ni = TOKENS // tm
    row = lambda i: (i, 0)
    x_spec = pl.BlockSpec((tm, D_MODEL), row)
    in_specs = [pl.BlockSpec((tm, D_FF), row),
                pl.BlockSpec((D_FF, D_MODEL), lambda i: (0, 0), pipeline_mode=pl.Buffered(1)),
                x_spec,
                _mod_spec(tm, k_gate),
                pl.BlockSpec((1, D_MODEL), lambda i: (0, 0)),
                _mod_spec(tm, k_shift_tail), _mod_spec(tm, k_shift_tail + 1)]
    operands = [act, w_d, x, mod, g_tail, mod, mod]
    out_specs = [x_spec]
    out_shape = [jax.ShapeDtypeStruct((TOKENS, D_MODEL), F32)]
    if tail == "mixer_norm":
        out_specs.append(pl.BlockSpec((tm, D_MODEL), row))
        out_shape.append(jax.ShapeDtypeStruct((TOKENS, D_MODEL), BF16))
    cast = w_gu_stack is not None
    if cast:
        cast_in, cast_out, cast_shape = _cast_specs(w_gu_stack, layer, ni, lambda i: i)
        in_specs.append(cast_in)
        operands.append(w_gu_stack)
        out_specs.append(cast_out)
        out_shape.append(cast_shape)
    return pl.pallas_call(
        functools.partial(_ffn_down_kernel, tail=tail, cast=cast),
        grid=(ni,),
        in_specs=in_specs,
        out_specs=out_specs,
        out_shape=out_shape,
        compiler_params=_params(("parallel",)),
        name="ffn_down",
    )(*operands)


def _in_proj_kernel(h_ref, tab_ref, wx_ref, gcq_ref, gckv_ref, wuq_ref, wukv_ref,
                    qa_ref, ka_ref, va_ref, qp_ref, kp_ref, vb_ref):
    h = h_ref[...]
    cos_a = tab_ref[:, 0 * LANES:1 * LANES]
    sin_a = tab_ref[:, 1 * LANES:2 * LANES]
    cos_b = tab_ref[:, 2 * LANES:3 * LANES]
    sin_lo = tab_ref[:, 3 * LANES:4 * LANES]
    sin_hi = tab_ref[:, 4 * LANES:5 * LANES]

    def rope_a(t):
        return t * cos_a + pltpu.roll(t, HALF_A, 1) * sin_a

    def rope_b(t):
        return t * cos_b + pltpu.roll(t, LANES - HALF_B, 1) * sin_lo + pltpu.roll(t, HALF_B, 1) * sin_hi

    proj = _dot(h, wx_ref[...])

    for hd in range(HA_Q):
        sl = slice(hd * HEAD_DIM, (hd + 1) * HEAD_DIM)
        qa_ref[:, sl] = (rope_a(proj[:, sl]) * WIN_Q_SCALE).astype(BF16)
    for hd in range(HA_KV):
        sl = slice(hd * HEAD_DIM, (hd + 1) * HEAD_DIM)
        ka_ref[:, sl] = rope_a(proj[:, COLS_QA + hd * HEAD_DIM:COLS_QA + (hd + 1) * HEAD_DIM]).astype(BF16)
    va_ref[...] = proj[:, COLS_QA + COLS_KA:OFF_CQ].astype(BF16)

    cq = _rms(proj[:, OFF_CQ:OFF_CKV], gcq_ref[...]).astype(BF16)
    q_all = _dot(cq, wuq_ref[...]) * MLA_Q_SCALE
    for hd in range(HB):
        base = hd * MLA_PAD
        qp_ref[base:base + QK_NOPE, :] = q_all[:, base:base + QK_NOPE].T.astype(BF16)
        qp_ref[base + QK_NOPE:base + MLA_PAD, :] = rope_b(q_all[:, base + QK_NOPE:base + MLA_PAD]).T.astype(BF16)

    ckv = _rms(proj[:, OFF_CKV:OFF_KR], gckv_ref[...]).astype(BF16)
    kv_all = _dot(ckv, wukv_ref[...])
    kr = rope_b(proj[:, OFF_KR:OFF_KR + LANES]).astype(BF16)
    for hd in range(HB):
        base = hd * MLA_PAD
        kp_ref[:, base:base + QK_NOPE] = kv_all[:, hd * QK_NOPE:(hd + 1) * QK_NOPE].astype(BF16)
        kp_ref[:, base + QK_NOPE:base + MLA_PAD] = kr
        vsl = slice(HB * QK_NOPE + hd * V_DIM, HB * QK_NOPE + (hd + 1) * V_DIM)
        vb_ref[hd * V_DIM:(hd + 1) * V_DIM, :] = kv_all[:, vsl].T.astype(BF16)


def _in_proj(h, tables, w_x, g_cq, g_ckv, w_uq, w_ukv):
    tm = 512
    const = lambda i: (0, 0)
    row = lambda i: (i, 0)
    nx = w_x.shape[1]
    col = lambda i: (0, i)
    resident = lambda shape: pl.BlockSpec(shape, const, pipeline_mode=pl.Buffered(1))
    outs = [(COLS_QA, True), (COLS_KA, True), (COLS_VA, True), (HB * MLA_PAD, False),
            (HB * MLA_PAD, True), (HB * V_DIM, False)]
    out_specs = [pl.BlockSpec((tm, n), row) if tok else pl.BlockSpec((n, tm), col) for n, tok in outs]
    out_shape = [jax.ShapeDtypeStruct((TOKENS, n) if tok else (n, TOKENS), BF16) for n, tok in outs]
    return pl.pallas_call(
        _in_proj_kernel,
        grid=(TOKENS // tm,),
        in_specs=[pl.BlockSpec((tm, D_MODEL), row),
                  pl.BlockSpec((tm, N_ROPE_TABLES * LANES), row),
                  resident((D_MODEL, nx)),
                  pl.BlockSpec((1, Q_RANK), const),
                  pl.BlockSpec((1, KV_RANK), const),
                  resident((Q_RANK, HB * MLA_PAD)),
                  resident((KV_RANK, HB * (QK_NOPE + V_DIM)))],
        out_specs=out_specs,
        out_shape=out_shape,
        compiler_params=_params(("parallel",)),
        name="in_proj",
    )(h, tables, w_x, g_cq, g_ckv, w_uq, w_ukv)


WIN_KEYS = 3 * BLOCK
WIN_ROWS = GROUP * BLOCK


def _win_mask_bias():
    rel = jnp.arange(WIN_KEYS)[None, :] - (jnp.arange(WIN_ROWS) % BLOCK)[:, None]
    off = -BLOCK * jnp.arange(3)[:, None, None]
    return jnp.where(jnp.abs(rel[None] + off) <= WINDOW, 0.0, -1e30).astype(F32)


def _win_scores(n, j, bias_ref, q_ref, k_ref):
    nb = SEQ // BLOCK
    first = jnp.clip(n - 1, 0, nb - 3)
    start = pl.multiple_of(first * BLOCK, BLOCK)
    q = jnp.concatenate([q_ref[j * BLOCK:(j + 1) * BLOCK, g * HEAD_DIM:(g + 1) * HEAD_DIM]
                         for g in range(GROUP)], axis=0)
    return _dot_nt(q, k_ref[pl.ds(start, WIN_KEYS), :]) + bias_ref[n - first], start


def _win_finish(t, start, j, sk, v_ref, o_ref):
    m = jnp.maximum(jnp.max(t, axis=-1, keepdims=True), sk)
    p = jnp.exp2(t - m)
    denom = jnp.sum(p, axis=-1, keepdims=True) + jnp.exp2(sk - m)
    o = (_dot(p.astype(BF16), v_ref[pl.ds(start, WIN_KEYS), :]) * (1.0 / denom)).astype(BF16)
    for g in range(GROUP):
        o_ref[j * BLOCK:(j + 1) * BLOCK, g * HEAD_DIM:(g + 1) * HEAD_DIM] = o[g * BLOCK:(g + 1) * BLOCK]


MLA_KEY_CHUNK = 512
MLA_ONES_ROWS = 16
MLA_TQ = 2048
MLA_Q_PIECE = 512
ATTN_STEPS_PER_BATCH = HB * (SEQ // MLA_TQ)
WIN_TILES_PER_KV = ATTN_STEPS_PER_BATCH // HA_KV
WIN_TQ = SEQ // WIN_TILES_PER_KV
MLA_CHUNKS_PER_WIN_BLOCK = (SEQ // MLA_KEY_CHUNK) // (WIN_TQ // BLOCK)


def _attn_kernel(sink_ref, bias_ref, qa_ref, ka_ref, va_ref, qt_ref, k_ref, vt_ref, oa_ref, ob_ref):
    h = pl.program_id(1)
    heads_per_kv = HB // HA_KV
    hk = h // heads_per_kv
    win_tile = (h % heads_per_kv) * pl.num_programs(2) + pl.program_id(2)
    row = lax.broadcasted_iota(jnp.int32, (WIN_ROWS, 1), 0)
    sk = jnp.zeros((WIN_ROWS, 1), F32)
    for g in range(GROUP):
        sk = jnp.where(row // BLOCK == g, sink_ref[hk * GROUP + g] * LOG2E, sk)

    tq = qt_ref.shape[1]
    n_pieces = tq // MLA_Q_PIECE
    piece = lambda qc: slice(qc * MLA_Q_PIECE, (qc + 1) * MLA_Q_PIECE)
    m = [jnp.full((1, MLA_Q_PIECE), -1e30, F32) for _ in range(n_pieces)]
    acc = [jnp.zeros((V_DIM + MLA_ONES_ROWS, MLA_Q_PIECE), F32) for _ in range(n_pieces)]
    ones = jnp.ones((MLA_ONES_ROWS, MLA_KEY_CHUNK), BF16)
    n_chunks = SEQ // MLA_KEY_CHUNK
    chunk = lambda ci: slice(ci * MLA_KEY_CHUNK, (ci + 1) * MLA_KEY_CHUNK)
    pairs = [(ci, qc) for ci in range(n_chunks) for qc in range(n_pieces)]
    scores = lambda ci, qc: _dot(k_ref[chunk(ci), :], qt_ref[:, piece(qc)])
    st_next = scores(*pairs[0])
    win = None
    for idx, (ci, qc) in enumerate(pairs):
        st = st_next
        if idx + 1 < len(pairs):
            st_next = scores(*pairs[idx + 1])
        j = ci // MLA_CHUNKS_PER_WIN_BLOCK
        if qc == 0 and ci % MLA_CHUNKS_PER_WIN_BLOCK == 0:
            win = _win_scores(win_tile * (WIN_TQ // BLOCK) + j, j, bias_ref, qa_ref, ka_ref)
        m_new = jnp.maximum(m[qc], jnp.max(st, axis=0, keepdims=True))
        alpha = jnp.exp2(m[qc] - m_new)
        p = jnp.exp2(st - m_new).astype(BF16)
        vt1 = jnp.concatenate([vt_ref[:, chunk(ci)], ones], axis=0)
        acc[qc] = acc[qc] * alpha + _dot(vt1, p)
        m[qc] = m_new
        if qc == n_pieces - 1 and ci % MLA_CHUNKS_PER_WIN_BLOCK == MLA_CHUNKS_PER_WIN_BLOCK - 1:
            _win_finish(*win, j, sk, va_ref, oa_ref)
    for qc in range(n_pieces):
        a = acc[qc]
        ob_ref[piece(qc), :] = (a[:V_DIM] * (1.0 / a[V_DIM:V_DIM + 1])).T.astype(BF16)


def _attn(qa, ka, va, sink, q_pad_t, k_pad, v_t):
    nq = SEQ // MLA_TQ
    heads_per_kv = HB // HA_KV
    gw = GROUP * HEAD_DIM
    win_rows = lambda b, h, i: (b * WIN_TILES_PER_KV + (h % heads_per_kv) * nq + i, h // heads_per_kv)
    win_kv = lambda b, h, i: (b, h // heads_per_kv)
    return pl.pallas_call(
        _attn_kernel,
        grid=(BATCH, HB, nq),
        in_specs=[pl.BlockSpec(memory_space=pltpu.SMEM),
                  pl.BlockSpec((3, WIN_ROWS, WIN_KEYS), lambda b, h, i: (0, 0, 0)),
                  pl.BlockSpec((WIN_TQ, gw), win_rows),
                  pl.BlockSpec((SEQ, HEAD_DIM), win_kv),
                  pl.BlockSpec((SEQ, HEAD_DIM), win_kv),
                  pl.BlockSpec((MLA_PAD, MLA_TQ), lambda b, h, i: (h, b * nq + i)),
                  pl.BlockSpec((SEQ, MLA_PAD), lambda b, h, i: (b, h)),
                  pl.BlockSpec((V_DIM, SEQ), lambda b, h, i: (h, b))],
        out_specs=[pl.BlockSpec((WIN_TQ, gw), win_rows),
                   pl.BlockSpec((MLA_TQ, V_DIM), lambda b, h, i: (b * nq + i, h))],
        out_shape=[jax.ShapeDtypeStruct((TOKENS, COLS_QA), BF16),
                   jax.ShapeDtypeStruct((TOKENS, HB * V_DIM), BF16)],
        compiler_params=_params(("parallel", "parallel", "arbitrary")),
        name="attn",
    )(sink, _win_mask_bias(), qa, ka, va, q_pad_t, k_pad, v_t)


MIX_COL_CHUNK = 512


def _mix_out_kernel(x_ref, h_ref, gt_ref, oa_ref, ob_ref, woa_ref, wob_ref, wg_ref, bg_ref, wout_ref, o_ref):
    h = h_ref[...]
    o_a = oa_ref[...]
    o_b = ob_ref[...]

    def merge(c):
        a_cols = slice(c, c + MIX_COL_CHUNK)
        b_cols = slice(D_MODEL + c, D_MODEL + c + MIX_COL_CHUNK)
        g_a = jax.nn.sigmoid(_dot(h, wg_ref[:, a_cols]) + bg_ref[:, a_cols])
        g_b = jax.nn.sigmoid(_dot(h, wg_ref[:, b_cols]) + bg_ref[:, b_cols])
        return (g_a * _dot(o_a, woa_ref[:, a_cols]) + g_b * _dot(o_b, wob_ref[:, a_cols])).astype(BF16)

    acc = None
    mix = merge(0)
    for c in range(0, D_MODEL, MIX_COL_CHUNK):
        nxt = merge(c + MIX_COL_CHUNK) if c + MIX_COL_CHUNK < D_MODEL else None
        part = _dot(mix, wout_ref[c:c + MIX_COL_CHUNK, :])
        acc = part if acc is None else acc + part
        mix = nxt
    o_ref[...] = x_ref[...] + gt_ref[...] * acc


def _mix_out(x, h, mod, k_gate, o_a, o_b, w_oa, w_ob, w_gate, b_gate, w_out):
    tm = 256
    row = lambda i: (i, 0)
    resident = lambda a: pl.BlockSpec(a.shape, lambda i: (0, 0), pipeline_mode=pl.Buffered(1))
    return pl.pallas_call(
        _mix_out_kernel,
        grid=(TOKENS // tm,),
        in_specs=[pl.BlockSpec((tm, D_MODEL), row),
                  pl.BlockSpec((tm, D_MODEL), row),
                  _mod_spec(tm, k_gate),
                  pl.BlockSpec((tm, COLS_QA), row),
                  pl.BlockSpec((tm, HB * V_DIM), row),
                  resident(w_oa), resident(w_ob), resident(w_gate), resident(b_gate), resident(w_out)],
        out_specs=pl.BlockSpec((tm, D_MODEL), row),
        out_shape=jax.ShapeDtypeStruct((TOKENS, D_MODEL), F32),
        compiler_params=_params(("parallel",)),
        name="mix_out",
    )(x, h, mod, o_a, o_b, w_oa, w_ob, w_gate, b_gate, w_out)


def _pad_heads(w, n_heads, width, pad_to):
    k = w.shape[0]
    w = w.reshape(k, n_heads, width)
    w = jnp.pad(w, ((0, 0), (0, 0), (0, pad_to - width)))
    return w.reshape(k, n_heads * pad_to)


def _layer_weights(l, w_in, w_uq, w_ukv):
    w = w_in[l].astype(BF16)
    w_x = jnp.pad(w[:, :OFF_GATE], ((0, 0), (0, LANES - QK_ROPE)))
    w_gate = w[:, OFF_GATE:]
    w_uq_pad = _pad_heads(w_uq[l], HB, QK_NOPE + QK_ROPE, MLA_PAD).astype(BF16)
    ukv = w_ukv[l].reshape(KV_RANK, HB, QK_NOPE + V_DIM)
    w_ukv_perm = jnp.concatenate([ukv[:, :, :QK_NOPE].reshape(KV_RANK, HB * QK_NOPE),
                                  ukv[:, :, QK_NOPE:].reshape(KV_RANK, HB * V_DIM)], axis=1).astype(BF16)
    return w_x, w_gate, w_uq_pad, w_ukv_perm


def kernel(x, c, positions, norm_g, w_ada, b_ada, w_ffn1_gu, w_ffn1_d, w_ffn2_gu, w_ffn2_d, w_in, b_gate, sink,
           g_cq, g_ckv, w_uq, w_ukv, w_oa, w_ob, w_out, g_final):
    tables, mods, w_gu = _prep(positions, c, w_ada, b_ada, w_ffn1_gu)
    xt = x.reshape(TOKENS, D_MODEL)
    for l in range(DEPTH):
        mod = mods[l].reshape(BATCH, 1, N_MOD * D_MODEL)
        g = norm_g[l]
        w_x, w_gate, w_uq_pad, w_ukv_perm = _layer_weights(l, w_in, w_uq, w_ukv)

        act, w_d = _ffn_up(xt, g[0:1], mod, 0, w_gu, w_ffn1_d, l)
        xt, h, w_gu = _ffn_down(act, w_d, xt, mod, 2, g[1:2], 3, "mixer_norm", w_ffn2_gu, l)

        qa, ka, va, q_pad_t, k_pad, v_t = _in_proj(h, tables, w_x, g_cq[l][None, :], g_ckv[l][None, :],
                                                   w_uq_pad, w_ukv_perm)
        o_a, o_b = _attn(qa, ka, va, sink[l], q_pad_t, k_pad, v_t)
        xt = _mix_out(xt, h, mod, 5, o_a, o_b, w_oa[l].astype(BF16), w_ob[l].astype(BF16), w_gate,
                      b_gate[l][None, :], w_out[l].astype(BF16))

        act, w_d = _ffn_up(xt, g[2:3], mod, 6, w_gu, w_ffn2_d, l)
        if l + 1 < DEPTH:
            xt, w_gu = _ffn_down(act, w_d, xt, mod, 8, g_final[None, :], 0, "plain", w_ffn1_gu, l + 1)
        else:
            xt, = _ffn_down(act, w_d, xt, mod, 8, g_final[None, :], 0, "final_norm")
    return xt.reshape(BATCH, SEQ, D_MODEL)
```

```python
import functools

import jax
import jax.numpy as jnp
import numpy as np
from jax import lax
from jax.experimental import pallas as pl
from jax.experimental.pallas import tpu as pltpu

D_MODEL = 2048
BATCH = 8
SEQ = 4096
DEPTH = 2
HA_Q = 8
HA_KV = 2
GROUP = HA_Q // HA_KV
HEAD_DIM = 128
WINDOW = 128
BLOCK = 128
HB = 8
QK_NOPE = 128
QK_ROPE = 64
V_DIM = 128
Q_RANK = 512
KV_RANK = 512
D_FF = 5632
ROPE_THETA = 10000.0
EPS = 1e-6
N_MOD = 9
TOKENS = BATCH * SEQ

COLS_QA = HA_Q * HEAD_DIM
COLS_KA = HA_KV * HEAD_DIM
COLS_VA = HA_KV * HEAD_DIM
OFF_CQ = COLS_QA + COLS_KA + COLS_VA
OFF_CKV = OFF_CQ + Q_RANK
OFF_KR = OFF_CKV + KV_RANK
OFF_GATE = OFF_KR + QK_ROPE

LANES = 128
MLA_PAD = 256
N_ROPE_TABLES = 5
VMEM_LIMIT = 56 * 1024 * 1024
LOG2E = float(np.log2(np.e))
MLA_Q_SCALE = (QK_NOPE + QK_ROPE) ** -0.5 * LOG2E
WIN_Q_SCALE = HEAD_DIM ** -0.5 * LOG2E

BF16 = jnp.bfloat16
F32 = jnp.float32


def _params(sem):
    return pltpu.CompilerParams(dimension_semantics=sem, vmem_limit_bytes=VMEM_LIMIT)


def _dot(a, b):
    return jnp.dot(a, b, preferred_element_type=F32)


def _dot_nt(a, b):
    return lax.dot_general(a, b, (((1,), (1,)), ((), ())), preferred_element_type=F32)


def _rms(x, g):
    return (x * lax.rsqrt(jnp.mean(x * x, axis=-1, keepdims=True) + EPS)) * g


def _norm_mod(x, g, shift, scale):
    return _rms(x, g) * (1.0 + scale) + shift


def _cast_specs(w_stack, layer, n_steps, step_index):
    _, rows, cols = w_stack.shape
    assert rows % n_steps == 0
    slab = rows // n_steps
    in_spec = pl.BlockSpec((None, slab, cols), lambda *ids: (layer, step_index(*ids), 0))
    out_spec = pl.BlockSpec((slab, cols), lambda *ids: (step_index(*ids), 0))
    return in_spec, out_spec, jax.ShapeDtypeStruct((rows, cols), BF16)


HALF_A = HEAD_DIM // 2
HALF_B = QK_ROPE // 2


def _prep_kernel(pos_ref, freq_ref, c_ref, wada_ref, bada_ref, cast_ref, tab_ref, mod_ref, cast_out_ref):
    cast_out_ref[...] = cast_ref[...].astype(BF16)
    c = c_ref[...]
    c_act = (c * jax.nn.sigmoid(c)).astype(BF16)
    mod_ref[...] = _dot(c_act, wada_ref[...].astype(BF16)) + bada_ref[...]

    ang = pos_ref[...].astype(F32) * freq_ref[...]
    cs = jnp.cos(ang)
    sn = jnp.sin(ang)
    lane = lax.broadcasted_iota(jnp.int32, cs.shape, 1)
    cs_a, sn_a = pltpu.roll(cs, HALF_A, 1), pltpu.roll(sn, HALF_A, 1)
    cs_b, sn_b = pltpu.roll(cs, LANES - HALF_B, 1), pltpu.roll(sn, LANES - HALF_B, 1)
    zero = jnp.zeros_like(cs)
    tab_ref[:, 0 * LANES:1 * LANES] = jnp.where(lane < HALF_A, cs, cs_a)
    tab_ref[:, 1 * LANES:2 * LANES] = jnp.where(lane < HALF_A, -sn, sn_a)
    tab_ref[:, 2 * LANES:3 * LANES] = jnp.where(lane < HALF_B, cs_a, jnp.where(lane < QK_ROPE, cs_b, zero))
    tab_ref[:, 3 * LANES:4 * LANES] = jnp.where(lane < HALF_B, -sn_a, zero)
    tab_ref[:, 4 * LANES:5 * LANES] = jnp.where((lane >= HALF_B) & (lane < QK_ROPE), sn_b, zero)


def _prep(positions, c, w_ada, b_ada, w_gu_stack):
    tm = 1024
    ni = TOKENS // tm
    n = N_MOD * D_MODEL
    tiles_per_layer = ni // DEPTH
    tn = n // tiles_per_layer
    assert ni % DEPTH == 0 and n % tiles_per_layer == 0 and tn % LANES == 0
    fa = ROPE_THETA ** (-jnp.arange(0, HEAD_DIM, 2, dtype=F32) / HEAD_DIM)
    fb = ROPE_THETA ** (-jnp.arange(0, QK_ROPE, 2, dtype=F32) / QK_ROPE)
    freq = jnp.concatenate([fa, fb, jnp.zeros((LANES - HALF_A - HALF_B,), F32)])[None, :]
    pos = positions.reshape(TOKENS, 1)
    ada_tile = lambda i: (i // tiles_per_layer, 0, i % tiles_per_layer)
    cast_in, cast_out, cast_shape = _cast_specs(w_gu_stack, 0, ni, lambda i: i)
    return pl.pallas_call(
        _prep_kernel,
        grid=(ni,),
        in_specs=[pl.BlockSpec((tm, 1), lambda i: (i, 0)),
                  pl.BlockSpec((1, LANES), lambda i: (0, 0)),
                  pl.BlockSpec((BATCH, D_MODEL), lambda i: (0, 0)),
                  pl.BlockSpec((None, D_MODEL, tn), ada_tile),
                  pl.BlockSpec((None, 1, tn), ada_tile),
                  cast_in],
        out_specs=[pl.BlockSpec((tm, N_ROPE_TABLES * LANES), lambda i: (i, 0)),
                   pl.BlockSpec((None, BATCH, tn), ada_tile),
                   cast_out],
        out_shape=[jax.ShapeDtypeStruct((TOKENS, N_ROPE_TABLES * LANES), F32),
                   jax.ShapeDtypeStruct((DEPTH, BATCH, n), F32),
                   cast_shape],
        compiler_params=_params(("parallel",)),
        name="prep",
    )(pos, freq, c, w_ada, b_ada.reshape(DEPTH, 1, n), w_gu_stack)


def _mod_spec(tm, k):
    return pl.BlockSpec((None, 1, D_MODEL), lambda i, *_: (i * tm // SEQ, 0, k))


FFN_ROW_CHUNK = 256


FFN_UP_TM, FFN_UP_TN = 1024, 512


def _ffn_up_kernel(x_ref, g_ref, sh_ref, sc_ref, cast_ref, w_hbm, o_hbm, cast_out_ref, h_ref, step_ref):
    cast_out_ref[...] = cast_ref[...].astype(BF16)
    i = pl.program_id(0)
    nj = D_FF // FFN_UP_TN
    step_ref[0] = 0

    def tile(wg_ref, wu_ref, o_ref):
        def swiglu(h):
            g = _dot(h, wg_ref[...])
            u = _dot(h, wu_ref[...])
            return (g * jax.nn.sigmoid(g) * u).astype(BF16)

        j = step_ref[0]
        step_ref[0] = j + 1

        @pl.when(j == 0)
        def _():
            for r in range(0, FFN_UP_TM, FFN_ROW_CHUNK):
                rows = slice(r, r + FFN_ROW_CHUNK)
                h = _norm_mod(x_ref[rows, :], g_ref[...], sh_ref[...], sc_ref[...]).astype(BF16)
                h_ref[rows, :] = h
                o_ref[rows, :] = swiglu(h)

        @pl.when(j != 0)
        def _():
            o_ref[...] = swiglu(h_ref[...])

    pltpu.emit_pipeline(
        tile,
        grid=(nj,),
        in_specs=[pl.BlockSpec((D_MODEL, FFN_UP_TN), lambda j: (0, j)),
                  pl.BlockSpec((D_MODEL, FFN_UP_TN), lambda j: (0, j + nj))],
        out_specs=[pl.BlockSpec((FFN_UP_TM, FFN_UP_TN), lambda j: (i, j))],
    )(w_hbm, w_hbm, o_hbm)


def _ffn_up(x, g, mod, k_shift, w_gu, w_d_stack, layer):
    tm = FFN_UP_TM
    ni = TOKENS // tm
    cast_in, cast_out, cast_shape = _cast_specs(w_d_stack, layer, ni, lambda i: i)
    return pl.pallas_call(
        _ffn_up_kernel,
        grid=(ni,),
        in_specs=[pl.BlockSpec((tm, D_MODEL), lambda i: (i, 0)),
                  pl.BlockSpec((1, D_MODEL), lambda i: (0, 0)),
                  _mod_spec(tm, k_shift), _mod_spec(tm, k_shift + 1),
                  cast_in,
                  pl.BlockSpec(memory_space=pl.ANY)],
        out_specs=[pl.BlockSpec(memory_space=pl.ANY), cast_out],
        out_shape=[jax.ShapeDtypeStruct((TOKENS, D_FF), BF16), cast_shape],
        scratch_shapes=[pltpu.VMEM((tm, D_MODEL), BF16), pltpu.SMEM((1,), jnp.int32)],
        compiler_params=_params(("arbitrary",)),
        name="ffn_up",
    )(x, g, mod, mod, w_d_stack, w_gu)


FFN_DOWN_COL_CHUNK = 512


def _ffn_down_kernel(a_ref, w_ref, x_ref, gt_ref, g_ref, sh_ref, sc_ref, *refs, tail, cast):
    if cast:
        cast_ref, *out_refs, cast_out_ref = refs
        cast_out_ref[...] = cast_ref[...].astype(BF16)
    else:
        out_refs = refs
    x_out = out_refs[0]
    a = a_ref[...]
    for c in range(0, D_MODEL, FFN_DOWN_COL_CHUNK):
        cols = slice(c, c + FFN_DOWN_COL_CHUNK)
        x_out[:, cols] = x_ref[:, cols] + (0.5 * gt_ref[:, cols]) * _dot(a, w_ref[:, cols])
    if tail == "final_norm":
        x_out[...] = _rms(x_out[...], g_ref[...])
    elif tail == "mixer_norm":
        out_refs[1][...] = _norm_mod(x_out[...], g_ref[...], sh_ref[...], sc_ref[...]).astype(BF16)


def _ffn_down(act, w_d, x, mod, k_gate, g_tail, k_shift_tail, tail, w_gu_stack=None, layer=None):
    tm = 256
    ni = TOKENS // tm
    row = lambda i: (i, 0)
    x_spec = pl.BlockSpec((tm, D_MODEL), row)
    in_specs = [pl.BlockSpec((tm, D_FF), row),
                pl.BlockSpec((D_FF, D_MODEL), lambda i: (0, 0), pipeline_mode=pl.Buffered(1)),
                x_spec,
                _mod_spec(tm, k_gate),
                pl.BlockSpec((1, D_MODEL), lambda i: (0, 0)),
                _mod_spec(tm, k_shift_tail), _mod_spec(tm, k_shift_tail + 1)]
    operands = [act, w_d, x, mod, g_tail, mod, mod]
    out_specs = [x_spec]
    out_shape = [jax.ShapeDtypeStruct((TOKENS, D_MODEL), F32)]
    if tail == "mixer_norm":
        out_specs.append(pl.BlockSpec((tm, D_MODEL), row))
        out_shape.append(jax.ShapeDtypeStruct((TOKENS, D_MODEL), BF16))
    cast = w_gu_stack is not None
    if cast:
        cast_in, cast_out, cast_shape = _cast_specs(w_gu_stack, layer, ni, lambda i: i)
        in_specs.append(cast_in)
        operands.append(w_gu_stack)
        out_specs.append(cast_out)
        out_shape.append(cast_shape)
    return pl.pallas_call(
        functools.partial(_ffn_down_kernel, tail=tail, cast=cast),
        grid=(ni,),
        in_specs=in_specs,
        out_specs=out_specs,
        out_shape=out_shape,
        compiler_params=_params(("parallel",)),
        name="ffn_down",
    )(*operands)


def _in_proj_kernel(h_ref, tab_ref, wx_ref, gcq_ref, gckv_ref, wuq_ref, wukv_ref,
                    qa_ref, ka_ref, va_ref, qp_ref, kp_ref, vb_ref):
    h = h_ref[...]
    cos_a = tab_ref[:, 0 * LANES:1 * LANES]
    sin_a = tab_ref[:, 1 * LANES:2 * LANES]
    cos_b = tab_ref[:, 2 * LANES:3 * LANES]
    sin_lo = tab_ref[:, 3 * LANES:4 * LANES]
    sin_hi = tab_ref[:, 4 * LANES:5 * LANES]

    def rope_a(t):
        return t * cos_a + pltpu.roll(t, HALF_A, 1) * sin_a

    def rope_b(t):
        return t * cos_b + pltpu.roll(t, LANES - HALF_B, 1) * sin_lo + pltpu.roll(t, HALF_B, 1) * sin_hi

    proj = _dot(h, wx_ref[...])

    for hd in range(HA_Q):
        sl = slice(hd * HEAD_DIM, (hd + 1) * HEAD_DIM)
        qa_ref[:, sl] = (rope_a(proj[:, sl]) * WIN_Q_SCALE).astype(BF16)
    for hd in range(HA_KV):
        sl = slice(hd * HEAD_DIM, (hd + 1) * HEAD_DIM)
        ka_ref[:, sl] = rope_a(proj[:, COLS_QA + hd * HEAD_DIM:COLS_QA + (hd + 1) * HEAD_DIM]).astype(BF16)
    va_ref[...] = proj[:, COLS_QA + COLS_KA:OFF_CQ].astype(BF16)

    cq = _rms(proj[:, OFF_CQ:OFF_CKV], gcq_ref[...]).astype(BF16)
    q_all = _dot(cq, wuq_ref[...]) * MLA_Q_SCALE
    for hd in range(HB):
        base = hd * MLA_PAD
        qp_ref[base:base + QK_NOPE, :] = q_all[:, base:base + QK_NOPE].T.astype(BF16)
        qp_ref[base + QK_NOPE:base + MLA_PAD, :] = rope_b(q_all[:, base + QK_NOPE:base + MLA_PAD]).T.astype(BF16)

    ckv = _rms(proj[:, OFF_CKV:OFF_KR], gckv_ref[...]).astype(BF16)
    kv_all = _dot(ckv, wukv_ref[...])
    kr = rope_b(proj[:, OFF_KR:OFF_KR + LANES]).astype(BF16)
    for hd in range(HB):
        base = hd * MLA_PAD
        kp_ref[:, base:base + QK_NOPE] = kv_all[:, hd * QK_NOPE:(hd + 1) * QK_NOPE].astype(BF16)
        kp_ref[:, base + QK_NOPE:base + MLA_PAD] = kr
        vsl = slice(HB * QK_NOPE + hd * V_DIM, HB * QK_NOPE + (hd + 1) * V_DIM)
        vb_ref[hd * V_DIM:(hd + 1) * V_DIM, :] = kv_all[:, vsl].T.astype(BF16)


def _in_proj(h, tables, w_x, g_cq, g_ckv, w_uq, w_ukv):
    tm = 512
    const = lambda i: (0, 0)
    row = lambda i: (i, 0)
    nx = w_x.shape[1]
    col = lambda i: (0, i)
    resident = lambda shape: pl.BlockSpec(shape, const, pipeline_mode=pl.Buffered(1))
    outs = [(COLS_QA, True), (COLS_KA, True), (COLS_VA, True), (HB * MLA_PAD, False),
            (HB * MLA_PAD, True), (HB * V_DIM, False)]
    out_specs = [pl.BlockSpec((tm, n), row) if tok else pl.BlockSpec((n, tm), col) for n, tok in outs]
    out_shape = [jax.ShapeDtypeStruct((TOKENS, n) if tok else (n, TOKENS), BF16) for n, tok in outs]
    return pl.pallas_call(
        _in_proj_kernel,
        grid=(TOKENS // tm,),
        in_specs=[pl.BlockSpec((tm, D_MODEL), row),
                  pl.BlockSpec((tm, N_ROPE_TABLES * LANES), row),
                  resident((D_MODEL, nx)),
                  pl.BlockSpec((1, Q_RANK), const),
                  pl.BlockSpec((1, KV_RANK), const),
                  resident((Q_RANK, HB * MLA_PAD)),
                  resident((KV_RANK, HB * (QK_NOPE + V_DIM)))],
        out_specs=out_specs,
        out_shape=out_shape,
        compiler_params=_params(("parallel",)),
        name="in_proj",
    )(h, tables, w_x, g_cq, g_ckv, w_uq, w_ukv)


WIN_KEYS = 3 * BLOCK
WIN_ROWS = GROUP * BLOCK


def _win_mask_bias():
    rel = jnp.arange(WIN_KEYS)[None, :] - (jnp.arange(WIN_ROWS) % BLOCK)[:, None]
    off = -BLOCK * jnp.arange(3)[:, None, None]
    return jnp.where(jnp.abs(rel[None] + off) <= WINDOW, 0.0, -1e30).astype(F32)


def _win_scores(n, j, bias_ref, q_ref, k_ref):
    nb = SEQ // BLOCK
    first = jnp.clip(n - 1, 0, nb - 3)
    start = pl.multiple_of(first * BLOCK, BLOCK)
    q = jnp.concatenate([q_ref[j * BLOCK:(j + 1) * BLOCK, g * HEAD_DIM:(g + 1) * HEAD_DIM]
                         for g in range(GROUP)], axis=0)
    return _dot_nt(q, k_ref[pl.ds(start, WIN_KEYS), :]) + bias_ref[n - first], start


def _win_finish(t, start, j, sk, v_ref, o_ref):
    m = jnp.maximum(jnp.max(t, axis=-1, keepdims=True), sk)
    p = jnp.exp2(t - m)
    denom = jnp.sum(p, axis=-1, keepdims=True) + jnp.exp2(sk - m)
    o = (_dot(p.astype(BF16), v_ref[pl.ds(start, WIN_KEYS), :]) * (1.0 / denom)).astype(BF16)
    for g in range(GROUP):
        o_ref[j * BLOCK:(j + 1) * BLOCK, g * HEAD_DIM:(g + 1) * HEAD_DIM] = o[g * BLOCK:(g + 1) * BLOCK]


MLA_KEY_CHUNK = 512
MLA_ONES_ROWS = 16
MLA_TQ = 2048
MLA_Q_PIECE = 512
ATTN_STEPS_PER_BATCH = HB * (SEQ // MLA_TQ)
WIN_TILES_PER_KV = ATTN_STEPS_PER_BATCH // HA_KV
WIN_TQ = SEQ // WIN_TILES_PER_KV
MLA_CHUNKS_PER_WIN_BLOCK = (SEQ // MLA_KEY_CHUNK) // (WIN_TQ // BLOCK)


def _attn_kernel(sink_ref, bias_ref, qa_ref, ka_ref, va_ref, qt_ref, k_ref, vt_ref, oa_ref, ob_ref):
    h = pl.program_id(1)
    heads_per_kv = HB // HA_KV
    hk = h // heads_per_kv
    win_tile = (h % heads_per_kv) * pl.num_programs(2) + pl.program_id(2)
    row = lax.broadcasted_iota(jnp.int32, (WIN_ROWS, 1), 0)
    sk = jnp.zeros((WIN_ROWS, 1), F32)
    for g in range(GROUP):
        sk = jnp.where(row // BLOCK == g, sink_ref[hk * GROUP + g] * LOG2E, sk)

    tq = qt_ref.shape[1]
    n_pieces = tq // MLA_Q_PIECE
    piece = lambda qc: slice(qc * MLA_Q_PIECE, (qc + 1) * MLA_Q_PIECE)
    m = [jnp.full((1, MLA_Q_PIECE), -1e30, F32) for _ in range(n_pieces)]
    acc = [jnp.zeros((V_DIM + MLA_ONES_ROWS, MLA_Q_PIECE), F32) for _ in range(n_pieces)]
    ones = jnp.ones((MLA_ONES_ROWS, MLA_KEY_CHUNK), BF16)
    n_chunks = SEQ // MLA_KEY_CHUNK
    chunk = lambda ci: slice(ci * MLA_KEY_CHUNK, (ci + 1) * MLA_KEY_CHUNK)
    pairs = [(ci, qc) for ci in range(n_chunks) for qc in range(n_pieces)]
    scores = lambda ci, qc: _dot(k_ref[chunk(ci), :], qt_ref[:, piece(qc)])
    st_next = scores(*pairs[0])
    win = None
    for idx, (ci, qc) in enumerate(pairs):
        st = st_next
        if idx + 1 < len(pairs):
            st_next = scores(*pairs[idx + 1])
        j = ci // MLA_CHUNKS_PER_WIN_BLOCK
        if qc == 0 and ci % MLA_CHUNKS_PER_WIN_BLOCK == 0:
            win = _win_scores(win_tile * (WIN_TQ // BLOCK) + j, j, bias_ref, qa_ref, ka_ref)
        m_new = jnp.maximum(m[qc], jnp.max(st, axis=0, keepdims=True))
        alpha = jnp.exp2(m[qc] - m_new)
        p = jnp.exp2(st - m_new).astype(BF16)
        vt1 = jnp.concatenate([vt_ref[:, chunk(ci)], ones], axis=0)
        acc[qc] = acc[qc] * alpha + _dot(vt1, p)
        m[qc] = m_new
        if qc == n_pieces - 1 and ci % MLA_CHUNKS_PER_WIN_BLOCK == MLA_CHUNKS_PER_WIN_BLOCK - 1:
            _win_finish(*win, j, sk, va_ref, oa_ref)
    for qc in range(n_pieces):
        a = acc[qc]
        ob_ref[piece(qc), :] = (a[:V_DIM] * (1.0 / a[V_DIM:V_DIM + 1])).T.astype(BF16)


def _attn(qa, ka, va, sink, q_pad_t, k_pad, v_t):
    nq = SEQ // MLA_TQ
    heads_per_kv = HB // HA_KV
    gw = GROUP * HEAD_DIM
    win_rows = lambda b, h, i: (b * WIN_TILES_PER_KV + (h % heads_per_kv) * nq + i, h // heads_per_kv)
    win_kv = lambda b, h, i: (b, h // heads_per_kv)
    return pl.pallas_call(
        _attn_kernel,
        grid=(BATCH, HB, nq),
        in_specs=[pl.BlockSpec(memory_space=pltpu.SMEM),
                  pl.BlockSpec((3, WIN_ROWS, WIN_KEYS), lambda b, h, i: (0, 0, 0)),
                  pl.BlockSpec((WIN_TQ, gw), win_rows),
                  pl.BlockSpec((SEQ, HEAD_DIM), win_kv),
                  pl.BlockSpec((SEQ, HEAD_DIM), win_kv),
                  pl.BlockSpec((MLA_PAD, MLA_TQ), lambda b, h, i: (h, b * nq + i)),
                  pl.BlockSpec((SEQ, MLA_PAD), lambda b, h, i: (b, h)),
                  pl.BlockSpec((V_DIM, SEQ), lambda b, h, i: (h, b))],
        out_specs=[pl.BlockSpec((WIN_TQ, gw), win_rows),
                   pl.BlockSpec((MLA_TQ, V_DIM), lambda b, h, i: (b * nq + i, h))],
        out_shape=[jax.ShapeDtypeStruct((TOKENS, COLS_QA), BF16),
                   jax.ShapeDtypeStruct((TOKENS, HB * V_DIM), BF16)],
        compiler_params=_params(("parallel", "parallel", "arbitrary")),
        name="attn",
    )(sink, _win_mask_bias(), qa, ka, va, q_pad_t, k_pad, v_t)


MIX_COL_CHUNK = 512


def _mix_out_kernel(x_ref, h_ref, gt_ref, oa_ref, ob_ref, woa_ref, wob_ref, wg_ref, bg_ref, wout_ref, o_ref):
    h = h_ref[...]
    o_a = oa_ref[...]
    o_b = ob_ref[...]

    def merge(c):
        a_cols = slice(c, c + MIX_COL_CHUNK)
        b_cols = slice(D_MODEL + c, D_MODEL + c + MIX_COL_CHUNK)
        g_a = jax.nn.sigmoid(_dot(h, wg_ref[:, a_cols]) + bg_ref[:, a_cols])
        g_b = jax.nn.sigmoid(_dot(h, wg_ref[:, b_cols]) + bg_ref[:, b_cols])
        return (g_a * _dot(o_a, woa_ref[:, a_cols]) + g_b * _dot(o_b, wob_ref[:, a_cols])).astype(BF16)

    acc = None
    mix = merge(0)
    for c in range(0, D_MODEL, MIX_COL_CHUNK):
        nxt = merge(c + MIX_COL_CHUNK) if c + MIX_COL_CHUNK < D_MODEL else None
        part = _dot(mix, wout_ref[c:c + MIX_COL_CHUNK, :])
        acc = part if acc is None else acc + part
        mix = nxt
    o_ref[...] = x_ref[...] + gt_ref[...] * acc


def _mix_out(x, h, mod, k_gate, o_a, o_b, w_oa, w_ob, w_gate, b_gate, w_out):
    tm = 256
    row = lambda i: (i, 0)
    resident = lambda a: pl.BlockSpec(a.shape, lambda i: (0, 0), pipeline_mode=pl.Buffered(1))
    return pl.pallas_call(
        _mix_out_kernel,
        grid=(TOKENS // tm,),
        in_specs=[pl.BlockSpec((tm, D_MODEL), row),
                  pl.BlockSpec((tm, D_MODEL), row),
                  _mod_spec(tm, k_gate),
                  pl.BlockSpec((tm, COLS_QA), row),
                  pl.BlockSpec((tm, HB * V_DIM), row),
                  resident(w_oa), resident(w_ob), resident(w_gate), resident(b_gate), resident(w_out)],
        out_specs=pl.BlockSpec((tm, D_MODEL), row),
        out_shape=jax.ShapeDtypeStruct((TOKENS, D_MODEL), F32),
        compiler_params=_params(("parallel",)),
        name="mix_out",
    )(x, h, mod, o_a, o_b, w_oa, w_ob, w_gate, b_gate, w_out)


def _pad_heads(w, n_heads, width, pad_to):
    k = w.shape[0]
    w = w.reshape(k, n_heads, width)
    w = jnp.pad(w, ((0, 0), (0, 0), (0, pad_to - width)))
    return w.reshape(k, n_heads * pad_to)


def _layer_weights(l, w_in, w_uq, w_ukv):
    w = w_in[l].astype(BF16)
    w_x = jnp.pad(w[:, :OFF_GATE], ((0, 0), (0, LANES - QK_ROPE)))
    w_gate = w[:, OFF_GATE:]
    w_uq_pad = _pad_heads(w_uq[l], HB, QK_NOPE + QK_ROPE, MLA_PAD).astype(BF16)
    ukv = w_ukv[l].reshape(KV_RANK, HB, QK_NOPE + V_DIM)
    w_ukv_perm = jnp.concatenate([ukv[:, :, :QK_NOPE].reshape(KV_RANK, HB * QK_NOPE),
                                  ukv[:, :, QK_NOPE:].reshape(KV_RANK, HB * V_DIM)], axis=1).astype(BF16)
    return w_x, w_gate, w_uq_pad, w_ukv_perm


def kernel(x, c, positions, norm_g, w_ada, b_ada, w_ffn1_gu, w_ffn1_d, w_ffn2_gu, w_ffn2_d, w_in, b_gate, sink,
           g_cq, g_ckv, w_uq, w_ukv, w_oa, w_ob, w_out, g_final):
    tables, mods, w_gu = _prep(positions, c, w_ada, b_ada, w_ffn1_gu)
    xt = x.reshape(TOKENS, D_MODEL)
    for l in range(DEPTH):
        mod = mods[l].reshape(BATCH, 1, N_MOD * D_MODEL)
        g = norm_g[l]
        w_x, w_gate, w_uq_pad, w_ukv_perm = _layer_weights(l, w_in, w_uq, w_ukv)

        act, w_d = _ffn_up(xt, g[0:1], mod, 0, w_gu, w_ffn1_d, l)
        xt, h, w_gu = _ffn_down(act, w_d, xt, mod, 2, g[1:2], 3, "mixer_norm", w_ffn2_gu, l)

        qa, ka, va, q_pad_t, k_pad, v_t = _in_proj(h, tables, w_x, g_cq[l][None, :], g_ckv[l][None, :],
                                                   w_uq_pad, w_ukv_perm)
        o_a, o_b = _attn(qa, ka, va, sink[l], q_pad_t, k_pad, v_t)
        xt = _mix_out(xt, h, mod, 5, o_a, o_b, w_oa[l].astype(BF16), w_ob[l].astype(BF16), w_gate,
                      b_gate[l][None, :], w_out[l].astype(BF16))

        act, w_d = _ffn_up(xt, g[2:3], mod, 6, w_gu, w_ffn2_d, l)
        if l + 1 < DEPTH:
            xt, w_gu = _ffn_down(act, w_d, xt, mod, 8, g_final[None, :], 0, "plain", w_ffn1_gu, l + 1)
        else:
            xt, = _ffn_down(act, w_d, xt, mod, 8, g_final[None, :], 0, "final_norm")
    return xt.reshape(BATCH, SEQ, D_MODEL)
```

```python
import functools

import jax
import jax.numpy as jnp
import numpy as np
from jax import lax
from jax.experimental import pallas as pl
from jax.experimental.pallas import tpu as pltpu

D_MODEL = 2048
BATCH = 8
SEQ = 4096
DEPTH = 2
HA_Q = 8
HA_KV = 2
GROUP = HA_Q // HA_KV
HEAD_DIM = 128
WINDOW = 128
BLOCK = 128
HB = 8
QK_NOPE = 128
QK_ROPE = 64
V_DIM = 128
Q_RANK = 512
KV_RANK = 512
D_FF = 5632
ROPE_THETA = 10000.0
EPS = 1e-6
N_MOD = 9
TOKENS = BATCH * SEQ

COLS_QA = HA_Q * HEAD_DIM
COLS_KA = HA_KV * HEAD_DIM
COLS_VA = HA_KV * HEAD_DIM
OFF_CQ = COLS_QA + COLS_KA + COLS_VA
OFF_CKV = OFF_CQ + Q_RANK
OFF_KR = OFF_CKV + KV_RANK
OFF_GATE = OFF_KR + QK_ROPE

LANES = 128
MLA_PAD = 256
N_ROPE_TABLES = 5
VMEM_LIMIT = 56 * 1024 * 1024
LOG2E = float(np.log2(np.e))
MLA_Q_SCALE = (QK_NOPE + QK_ROPE) ** -0.5 * LOG2E
WIN_Q_SCALE = HEAD_DIM ** -0.5 * LOG2E

BF16 = jnp.bfloat16
F32 = jnp.float32


def _params(sem):
    return pltpu.CompilerParams(dimension_semantics=sem, vmem_limit_bytes=VMEM_LIMIT)


def _dot(a, b):
    return jnp.dot(a, b, preferred_element_type=F32)


def _dot_nt(a, b):
    return lax.dot_general(a, b, (((1,), (1,)), ((), ())), preferred_element_type=F32)


def _rms(x, g):
    return (x * lax.rsqrt(jnp.mean(x * x, axis=-1, keepdims=True) + EPS)) * g


def _norm_mod(x, g, shift, scale):
    return _rms(x, g) * (1.0 + scale) + shift


def _cast_specs(w_stack, layer, n_steps, step_index):
    _, rows, cols = w_stack.shape
    assert rows % n_steps == 0
    slab = rows // n_steps
    in_spec = pl.BlockSpec((None, slab, cols), lambda *ids: (layer, step_index(*ids), 0))
    out_spec = pl.BlockSpec((slab, cols), lambda *ids: (step_index(*ids), 0))
    return in_spec, out_spec, jax.ShapeDtypeStruct((rows, cols), BF16)


HALF_A = HEAD_DIM // 2
HALF_B = QK_ROPE // 2


def _prep_kernel(pos_ref, freq_ref, c_ref, wada_ref, bada_ref, cast_ref, tab_ref, mod_ref, cast_out_ref):
    cast_out_ref[...] = cast_ref[...].astype(BF16)
    c = c_ref[...]
    c_act = (c * jax.nn.sigmoid(c)).astype(BF16)
    mod_ref[...] = _dot(c_act, wada_ref[...].astype(BF16)) + bada_ref[...]

    ang = pos_ref[...].astype(F32) * freq_ref[...]
    cs = jnp.cos(ang)
    sn = jnp.sin(ang)
    lane = lax.broadcasted_iota(jnp.int32, cs.shape, 1)
    cs_a, sn_a = pltpu.roll(cs, HALF_A, 1), pltpu.roll(sn, HALF_A, 1)
    cs_b, sn_b = pltpu.roll(cs, LANES - HALF_B, 1), pltpu.roll(sn, LANES - HALF_B, 1)
    zero = jnp.zeros_like(cs)
    tab_ref[:, 0 * LANES:1 * LANES] = jnp.where(lane < HALF_A, cs, cs_a)
    tab_ref[:, 1 * LANES:2 * LANES] = jnp.where(lane < HALF_A, -sn, sn_a)
    tab_ref[:, 2 * LANES:3 * LANES] = jnp.where(lane < HALF_B, cs_a, jnp.where(lane < QK_ROPE, cs_b, zero))
    tab_ref[:, 3 * LANES:4 * LANES] = jnp.where(lane < HALF_B, -sn_a, zero)
    tab_ref[:, 4 * LANES:5 * LANES] = jnp.where((lane >= HALF_B) & (lane < QK_ROPE), sn_b, zero)


def _prep(positions, c, w_ada, b_ada, w_gu_stack):
    tm = 1024
    ni = TOKENS // tm
    n = N_MOD * D_MODEL
    tiles_per_layer = ni // DEPTH
    tn = n // tiles_per_layer
    assert ni % DEPTH == 0 and n % tiles_per_layer == 0 and tn % LANES == 0
    fa = ROPE_THETA ** (-jnp.arange(0, HEAD_DIM, 2, dtype=F32) / HEAD_DIM)
    fb = ROPE_THETA ** (-jnp.arange(0, QK_ROPE, 2, dtype=F32) / QK_ROPE)
    freq = jnp.concatenate([fa, fb, jnp.zeros((LANES - HALF_A - HALF_B,), F32)])[None, :]
    pos = positions.reshape(TOKENS, 1)
    ada_tile = lambda i: (i // tiles_per_layer, 0, i % tiles_per_layer)
    cast_in, cast_out, cast_shape = _cast_specs(w_gu_stack, 0, ni, lambda i: i)
    return pl.pallas_call(
        _prep_kernel,
        grid=(ni,),
        in_specs=[pl.BlockSpec((tm, 1), lambda i: (i, 0)),
                  pl.BlockSpec((1, LANES), lambda i: (0, 0)),
                  pl.BlockSpec((BATCH, D_MODEL), lambda i: (0, 0)),
                  pl.BlockSpec((None, D_MODEL, tn), ada_tile),
                  pl.BlockSpec((None, 1, tn), ada_tile),
                  cast_in],
        out_specs=[pl.BlockSpec((tm, N_ROPE_TABLES * LANES), lambda i: (i, 0)),
                   pl.BlockSpec((None, BATCH, tn), ada_tile),
                   cast_out],
        out_shape=[jax.ShapeDtypeStruct((TOKENS, N_ROPE_TABLES * LANES), F32),
                   jax.ShapeDtypeStruct((DEPTH, BATCH, n), F32),
                   cast_shape],
        compiler_params=_params(("parallel",)),
        name="prep",
    )(pos, freq, c, w_ada, b_ada.reshape(DEPTH, 1, n), w_gu_stack)


def _mod_spec(tm, k):
    return pl.BlockSpec((None, 1, D_MODEL), lambda i, *_: (i * tm // SEQ, 0, k))


FFN_ROW_CHUNK = 256


FFN_UP_TM, FFN_UP_TN = 1024, 512


def _ffn_up_kernel(x_hbm, g_hbm, mod_hbm, w_hbm, cast_hbm, o_hbm, cast_out_hbm, h_ref, step_ref, *, k_shift, layer):
    nj = D_FF // FFN_UP_TN
    ni = TOKENS // FFN_UP_TM
    step_ref[0] = 0

    def tile(x_ref, g_ref, sh_ref, sc_ref, wg_ref, wu_ref, cast_ref, o_ref, cast_out_ref):
        cast_out_ref[...] = cast_ref[...].astype(BF16)

        def swiglu(h):
            g = _dot(h, wg_ref[...])
            u = _dot(h, wu_ref[...])
            return (g * jax.nn.sigmoid(g) * u).astype(BF16)

        j = step_ref[0]
        step_ref[0] = jnp.where(j == nj - 1, 0, j + 1)

        @pl.when(j == 0)
        def _():
            for r in range(0, FFN_UP_TM, FFN_ROW_CHUNK):
                rows = slice(r, r + FFN_ROW_CHUNK)
                h = _norm_mod(x_ref[rows, :], g_ref[...], sh_ref[0], sc_ref[0]).astype(BF16)
                h_ref[rows, :] = h
                o_ref[rows, :] = swiglu(h)

        @pl.when(j != 0)
        def _():
            o_ref[...] = swiglu(h_ref[...])

    slab = D_FF // (ni * nj)
    batch = lambda i: i * FFN_UP_TM // SEQ
    pltpu.emit_pipeline(
        tile,
        grid=(ni, nj),
        in_specs=[pl.BlockSpec((FFN_UP_TM, D_MODEL), lambda i, j: (i, 0)),
                  pl.BlockSpec((1, D_MODEL), lambda i, j: (0, 0)),
                  pl.BlockSpec((1, 1, D_MODEL), lambda i, j: (batch(i), 0, k_shift)),
                  pl.BlockSpec((1, 1, D_MODEL), lambda i, j: (batch(i), 0, k_shift + 1)),
                  pl.BlockSpec((D_MODEL, FFN_UP_TN), lambda i, j: (0, j)),
                  pl.BlockSpec((D_MODEL, FFN_UP_TN), lambda i, j: (0, j + nj)),
                  pl.BlockSpec((slab, D_MODEL), lambda i, j: (layer * (D_FF // slab) + i * nj + j, 0))],
        out_specs=[pl.BlockSpec((FFN_UP_TM, FFN_UP_TN), lambda i, j: (i, j)),
                   pl.BlockSpec((slab, D_MODEL), lambda i, j: (i * nj + j, 0))],
    )(x_hbm, g_hbm, mod_hbm, mod_hbm, w_hbm, w_hbm, cast_hbm, o_hbm, cast_out_hbm)


def _ffn_up(x, g, mod, k_shift, w_gu, w_d_stack, layer):
    any_spec = pl.BlockSpec(memory_space=pl.ANY)
    w_d_rows = w_d_stack.reshape(DEPTH * D_FF, D_MODEL)
    return pl.pallas_call(
        functools.partial(_ffn_up_kernel, k_shift=k_shift, layer=layer),
        in_specs=[any_spec] * 5,
        out_specs=[any_spec, any_spec],
        out_shape=[jax.ShapeDtypeStruct((TOKENS, D_FF), BF16), jax.ShapeDtypeStruct((D_FF, D_MODEL), BF16)],
        scratch_shapes=[pltpu.VMEM((FFN_UP_TM, D_MODEL), BF16), pltpu.SMEM((1,), jnp.int32)],
        compiler_params=pltpu.CompilerParams(vmem_limit_bytes=VMEM_LIMIT),
        name="ffn_up",
    )(x, g, mod, w_gu, w_d_rows)


FFN_DOWN_COL_CHUNK = 512


def _ffn_down_kernel(a_ref, w_ref, x_ref, gt_ref, g_ref, sh_ref, sc_ref, *refs, tail, cast):
    if cast:
        cast_ref, *out_refs, cast_out_ref = refs
        cast_out_ref[...] = cast_ref[...].astype(BF16)
    else:
        out_refs = refs
    x_out = out_refs[0]
    a = a_ref[...]
    for c in range(0, D_MODEL, FFN_DOWN_COL_CHUNK):
        cols = slice(c, c + FFN_DOWN_COL_CHUNK)
        x_out[:, cols] = x_ref[:, cols] + (0.5 * gt_ref[:, cols]) * _dot(a, w_ref[:, cols])
    if tail == "final_norm":
        x_out[...] = _rms(x_out[...], g_ref[...])
    elif tail == "mixer_norm":
        out_refs[1][...] = _norm_mod(x_out[...], g_ref[...], sh_ref[...], sc_ref[...]).astype(BF16)


def _ffn_down(act, w_d, x, mod, k_gate, g_tail, k_shift_tail, tail, w_gu_stack=None, layer=None):
    tm = 256
    ni = TOKENS // tm
    row = lambda i: (i, 0)
    x_spec = pl.BlockSpec((tm, D_MODEL), row)
    in_specs = [pl.BlockSpec((tm, D_FF), row),
                pl.BlockSpec((D_FF, D_MODEL), lambda i: (0, 0), pipeline_mode=pl.Buffered(1)),
                x_spec,
                _mod_spec(tm, k_gate),
                pl.BlockSpec((1, D_MODEL), lambda i: (0, 0)),
                _mod_spec(tm, k_shift_tail), _mod_spec(tm, k_shift_tail + 1)]
    operands = [act, w_d, x, mod, g_tail, mod, mod]
    out_specs = [x_spec]
    out_shape = [jax.ShapeDtypeStruct((TOKENS, D_MODEL), F32)]
    if tail == "mixer_norm":
        out_specs.append(pl.BlockSpec((tm, D_MODEL), row))
        out_shape.append(jax.ShapeDtypeStruct((TOKENS, D_MODEL), BF16))
    cast = w_gu_stack is not None
    if cast:
        cast_in, cast_out, cast_shape = _cast_specs(w_gu_stack, layer, ni, lambda i: i)
        in_specs.append(cast_in)
        operands.append(w_gu_stack)
        out_specs.append(cast_out)
        out_shape.append(cast_shape)
    return pl.pallas_call(
        functools.partial(_ffn_down_kernel, tail=tail, cast=cast),
        grid=(ni,),
        in_specs=in_specs,
        out_specs=out_specs,
        out_shape=out_shape,
        compiler_params=_params(("parallel",)),
        name="ffn_down",
    )(*operands)


def _in_proj_kernel(h_ref, tab_ref, wx_ref, gcq_ref, gckv_ref, wuq_ref, wukv_ref,
                    qa_ref, ka_ref, va_ref, qp_ref, kp_ref, vb_ref):
    h = h_ref[...]
    cos_a = tab_ref[:, 0 * LANES:1 * LANES]
    sin_a = tab_ref[:, 1 * LANES:2 * LANES]
    cos_b = tab_ref[:, 2 * LANES:3 * LANES]
    sin_lo = tab_ref[:, 3 * LANES:4 * LANES]
    sin_hi = tab_ref[:, 4 * LANES:5 * LANES]

    def rope_a(t):
        return t * cos_a + pltpu.roll(t, HALF_A, 1) * sin_a

    def rope_b(t):
        return t * cos_b + pltpu.roll(t, LANES - HALF_B, 1) * sin_lo + pltpu.roll(t, HALF_B, 1) * sin_hi

    proj = _dot(h, wx_ref[...])

    for hd in range(HA_Q):
        sl = slice(hd * HEAD_DIM, (hd + 1) * HEAD_DIM)
        qa_ref[:, sl] = (rope_a(proj[:, sl]) * WIN_Q_SCALE).astype(BF16)
    for hd in range(HA_KV):
        sl = slice(hd * HEAD_DIM, (hd + 1) * HEAD_DIM)
        ka_ref[:, sl] = rope_a(proj[:, COLS_QA + hd * HEAD_DIM:COLS_QA + (hd + 1) * HEAD_DIM]).astype(BF16)
    va_ref[...] = proj[:, COLS_QA + COLS_KA:OFF_CQ].astype(BF16)

    cq = _rms(proj[:, OFF_CQ:OFF_CKV], gcq_ref[...]).astype(BF16)
    q_all = _dot(cq, wuq_ref[...]) * MLA_Q_SCALE
    for hd in range(HB):
        base = hd * MLA_PAD
        qp_ref[base:base + QK_NOPE, :] = q_all[:, base:base + QK_NOPE].T.astype(BF16)
        qp_ref[base + QK_NOPE:base + MLA_PAD, :] = rope_b(q_all[:, base + QK_NOPE:base + MLA_PAD]).T.astype(BF16)

    ckv = _rms(proj[:, OFF_CKV:OFF_KR], gckv_ref[...]).astype(BF16)
    kv_all = _dot(ckv, wukv_ref[...])
    kr = rope_b(proj[:, OFF_KR:OFF_KR + LANES]).astype(BF16)
    for hd in range(HB):
        base = hd * MLA_PAD
        kp_ref[:, base:base + QK_NOPE] = kv_all[:, hd * QK_NOPE:(hd + 1) * QK_NOPE].astype(BF16)
        kp_ref[:, base + QK_NOPE:base + MLA_PAD] = kr
        vsl = slice(HB * QK_NOPE + hd * V_DIM, HB * QK_NOPE + (hd + 1) * V_DIM)
        vb_ref[hd * V_DIM:(hd + 1) * V_DIM, :] = kv_all[:, vsl].T.astype(BF16)


def _in_proj(h, tables, w_x, g_cq, g_ckv, w_uq, w_ukv):
    tm = 512
    const = lambda i: (0, 0)
    row = lambda i: (i, 0)
    nx = w_x.shape[1]
    col = lambda i: (0, i)
    resident = lambda shape: pl.BlockSpec(shape, const, pipeline_mode=pl.Buffered(1))
    outs = [(COLS_QA, True), (COLS_KA, True), (COLS_VA, True), (HB * MLA_PAD, False),
            (HB * MLA_PAD, True), (HB * V_DIM, False)]
    out_specs = [pl.BlockSpec((tm, n), row) if tok else pl.BlockSpec((n, tm), col) for n, tok in outs]
    out_shape = [jax.ShapeDtypeStruct((TOKENS, n) if tok else (n, TOKENS), BF16) for n, tok in outs]
    return pl.pallas_call(
        _in_proj_kernel,
        grid=(TOKENS // tm,),
        in_specs=[pl.BlockSpec((tm, D_MODEL), row),
                  pl.BlockSpec((tm, N_ROPE_TABLES * LANES), row),
                  resident((D_MODEL, nx)),
                  pl.BlockSpec((1, Q_RANK), const),
                  pl.BlockSpec((1, KV_RANK), const),
                  resident((Q_RANK, HB * MLA_PAD)),
                  resident((KV_RANK, HB * (QK_NOPE + V_DIM)))],
        out_specs=out_specs,
        out_shape=out_shape,
        compiler_params=_params(("parallel",)),
        name="in_proj",
    )(h, tables, w_x, g_cq, g_ckv, w_uq, w_ukv)


WIN_KEYS = 3 * BLOCK
WIN_ROWS = GROUP * BLOCK


def _win_mask_bias():
    rel = jnp.arange(WIN_KEYS)[None, :] - (jnp.arange(WIN_ROWS) % BLOCK)[:, None]
    off = -BLOCK * jnp.arange(3)[:, None, None]
    return jnp.where(jnp.abs(rel[None] + off) <= WINDOW, 0.0, -1e30).astype(F32)


def _win_scores(n, j, bias_ref, q_ref, k_ref):
    nb = SEQ // BLOCK
    first = jnp.clip(n - 1, 0, nb - 3)
    start = pl.multiple_of(first * BLOCK, BLOCK)
    q = jnp.concatenate([q_ref[j * BLOCK:(j + 1) * BLOCK, g * HEAD_DIM:(g + 1) * HEAD_DIM]
                         for g in range(GROUP)], axis=0)
    return _dot_nt(q, k_ref[pl.ds(start, WIN_KEYS), :]) + bias_ref[n - first], start


def _win_finish(t, start, j, sk, v_ref, o_ref):
    m = jnp.maximum(jnp.max(t, axis=-1, keepdims=True), sk)
    p = jnp.exp2(t - m)
    denom = jnp.sum(p, axis=-1, keepdims=True) + jnp.exp2(sk - m)
    o = (_dot(p.astype(BF16), v_ref[pl.ds(start, WIN_KEYS), :]) * (1.0 / denom)).astype(BF16)
    for g in range(GROUP):
        o_ref[j * BLOCK:(j + 1) * BLOCK, g * HEAD_DIM:(g + 1) * HEAD_DIM] = o[g * BLOCK:(g + 1) * BLOCK]


MLA_KEY_CHUNK = 512
MLA_ONES_ROWS = 16
MLA_TQ = 2048
MLA_Q_PIECE = 512
ATTN_STEPS_PER_BATCH = HB * (SEQ // MLA_TQ)
WIN_TILES_PER_KV = ATTN_STEPS_PER_BATCH // HA_KV
WIN_TQ = SEQ // WIN_TILES_PER_KV
MLA_CHUNKS_PER_WIN_BLOCK = (SEQ // MLA_KEY_CHUNK) // (WIN_TQ // BLOCK)


def _attn_kernel(sink_ref, bias_ref, qa_ref, ka_ref, va_ref, qt_ref, k_ref, vt_ref, oa_ref, ob_ref):
    h = pl.program_id(1)
    heads_per_kv = HB // HA_KV
    hk = h // heads_per_kv
    win_tile = (h % heads_per_kv) * pl.num_programs(2) + pl.program_id(2)
    row = lax.broadcasted_iota(jnp.int32, (WIN_ROWS, 1), 0)
    sk = jnp.zeros((WIN_ROWS, 1), F32)
    for g in range(GROUP):
        sk = jnp.where(row // BLOCK == g, sink_ref[hk * GROUP + g] * LOG2E, sk)

    tq = qt_ref.shape[1]
    n_pieces = tq // MLA_Q_PIECE
    piece = lambda qc: slice(qc * MLA_Q_PIECE, (qc + 1) * MLA_Q_PIECE)
    m = [jnp.full((1, MLA_Q_PIECE), -1e30, F32) for _ in range(n_pieces)]
    acc = [jnp.zeros((V_DIM + MLA_ONES_ROWS, MLA_Q_PIECE), F32) for _ in range(n_pieces)]
    ones = jnp.ones((MLA_ONES_ROWS, MLA_KEY_CHUNK), BF16)
    n_chunks = SEQ // MLA_KEY_CHUNK
    chunk = lambda ci: slice(ci * MLA_KEY_CHUNK, (ci + 1) * MLA_KEY_CHUNK)
    pairs = [(ci, qc) for ci in range(n_chunks) for qc in range(n_pieces)]
    scores = lambda ci, qc: _dot(k_ref[chunk(ci), :], qt_ref[:, piece(qc)])
    st_next = scores(*pairs[0])
    win = None
    for idx, (ci, qc) in enumerate(pairs):
        st = st_next
        if idx + 1 < len(pairs):
            st_next = scores(*pairs[idx + 1])
        j = ci // MLA_CHUNKS_PER_WIN_BLOCK
        if qc == 0 and ci % MLA_CHUNKS_PER_WIN_BLOCK == 0:
            win = _win_scores(win_tile * (WIN_TQ // BLOCK) + j, j, bias_ref, qa_ref, ka_ref)
        m_new = jnp.maximum(m[qc], jnp.max(st, axis=0, keepdims=True))
        alpha = jnp.exp2(m[qc] - m_new)
        p = jnp.exp2(st - m_new).astype(BF16)
        vt1 = jnp.concatenate([vt_ref[:, chunk(ci)], ones], axis=0)
        acc[qc] = acc[qc] * alpha + _dot(vt1, p)
        m[qc] = m_new
        if qc == n_pieces - 1 and ci % MLA_CHUNKS_PER_WIN_BLOCK == MLA_CHUNKS_PER_WIN_BLOCK - 1:
            _win_finish(*win, j, sk, va_ref, oa_ref)
    for qc in range(n_pieces):
        a = acc[qc]
        ob_ref[piece(qc), :] = (a[:V_DIM] * (1.0 / a[V_DIM:V_DIM + 1])).T.astype(BF16)


def _attn(qa, ka, va, sink, q_pad_t, k_pad, v_t):
    nq = SEQ // MLA_TQ
    heads_per_kv = HB // HA_KV
    gw = GROUP * HEAD_DIM
    win_rows = lambda b, h, i: (b * WIN_TILES_PER_KV + (h % heads_per_kv) * nq + i, h // heads_per_kv)
    win_kv = lambda b, h, i: (b, h // heads_per_kv)
    return pl.pallas_call(
        _attn_kernel,
        grid=(BATCH, HB, nq),
        in_specs=[pl.BlockSpec(memory_space=pltpu.SMEM),
                  pl.BlockSpec((3, WIN_ROWS, WIN_KEYS), lambda b, h, i: (0, 0, 0)),
                  pl.BlockSpec((WIN_TQ, gw), win_rows),
                  pl.BlockSpec((SEQ, HEAD_DIM), win_kv),
                  pl.BlockSpec((SEQ, HEAD_DIM), win_kv),
                  pl.BlockSpec((MLA_PAD, MLA_TQ), lambda b, h, i: (h, b * nq + i)),
                  pl.BlockSpec((SEQ, MLA_PAD), lambda b, h, i: (b, h)),
                  pl.BlockSpec((V_DIM, SEQ), lambda b, h, i: (h, b))],
        out_specs=[pl.BlockSpec((WIN_TQ, gw), win_rows),
                   pl.BlockSpec((MLA_TQ, V_DIM), lambda b, h, i: (b * nq + i, h))],
        out_shape=[jax.ShapeDtypeStruct((TOKENS, COLS_QA), BF16),
                   jax.ShapeDtypeStruct((TOKENS, HB * V_DIM), BF16)],
        compiler_params=_params(("parallel", "parallel", "arbitrary")),
        name="attn",
    )(sink, _win_mask_bias(), qa, ka, va, q_pad_t, k_pad, v_t)


MIX_COL_CHUNK = 512


def _mix_out_kernel(x_ref, h_ref, gt_ref, oa_ref, ob_ref, woa_ref, wob_ref, wg_ref, bg_ref, wout_ref, o_ref):
    h = h_ref[...]
    o_a = oa_ref[...]
    o_b = ob_ref[...]

    def merge(c):
        a_cols = slice(c, c + MIX_COL_CHUNK)
        b_cols = slice(D_MODEL + c, D_MODEL + c + MIX_COL_CHUNK)
        g_a = jax.nn.sigmoid(_dot(h, wg_ref[:, a_cols]) + bg_ref[:, a_cols])
        g_b = jax.nn.sigmoid(_dot(h, wg_ref[:, b_cols]) + bg_ref[:, b_cols])
        return (g_a * _dot(o_a, woa_ref[:, a_cols]) + g_b * _dot(o_b, wob_ref[:, a_cols])).astype(BF16)

    acc = None
    mix = merge(0)
    for c in range(0, D_MODEL, MIX_COL_CHUNK):
        nxt = merge(c + MIX_COL_CHUNK) if c + MIX_COL_CHUNK < D_MODEL else None
        part = _dot(mix, wout_ref[c:c + MIX_COL_CHUNK, :])
        acc = part if acc is None else acc + part
        mix = nxt
    o_ref[...] = x_ref[...] + gt_ref[...] * acc


def _mix_out(x, h, mod, k_gate, o_a, o_b, w_oa, w_ob, w_gate, b_gate, w_out):
    tm = 256
    row = lambda i: (i, 0)
    resident = lambda a: pl.BlockSpec(a.shape, lambda i: (0, 0), pipeline_mode=pl.Buffered(1))
    return pl.pallas_call(
        _mix_out_kernel,
        grid=(TOKENS // tm,),
        in_specs=[pl.BlockSpec((tm, D_MODEL), row),
                  pl.BlockSpec((tm, D_MODEL), row),
                  _mod_spec(tm, k_gate),
                  pl.BlockSpec((tm, COLS_QA), row),
                  pl.BlockSpec((tm, HB * V_DIM), row),
                  resident(w_oa), resident(w_ob), resident(w_gate), resident(b_gate), resident(w_out)],
        out_specs=pl.BlockSpec((tm, D_MODEL), row),
        out_shape=jax.ShapeDtypeStruct((TOKENS, D_MODEL), F32),
        compiler_params=_params(("parallel",)),
        name="mix_out",
    )(x, h, mod, o_a, o_b, w_oa, w_ob, w_gate, b_gate, w_out)


def _pad_heads(w, n_heads, width, pad_to):
    k = w.shape[0]
    w = w.reshape(k, n_heads, width)
    w = jnp.pad(w, ((0, 0), (0, 0), (0, pad_to - width)))
    return w.reshape(k, n_heads * pad_to)


def _layer_weights(l, w_in, w_uq, w_ukv):
    w = w_in[l].astype(BF16)
    w_x = jnp.pad(w[:, :OFF_GATE], ((0, 0), (0, LANES - QK_ROPE)))
    w_gate = w[:, OFF_GATE:]
    w_uq_pad = _pad_heads(w_uq[l], HB, QK_NOPE + QK_ROPE, MLA_PAD).astype(BF16)
    ukv = w_ukv[l].reshape(KV_RANK, HB, QK_NOPE + V_DIM)
    w_ukv_perm = jnp.concatenate([ukv[:, :, :QK_NOPE].reshape(KV_RANK, HB * QK_NOPE),
                                  ukv[:, :, QK_NOPE:].reshape(KV_RANK, HB * V_DIM)], axis=1).astype(BF16)
    return w_x, w_gate, w_uq_pad, w_ukv_perm


def kernel(x, c, positions, norm_g, w_ada, b_ada, w_ffn1_gu, w_ffn1_d, w_ffn2_gu, w_ffn2_d, w_in, b_gate, sink,
           g_cq, g_ckv, w_uq, w_ukv, w_oa, w_ob, w_out, g_final):
    tables, mods, w_gu = _prep(positions, c, w_ada, b_ada, w_ffn1_gu)
    xt = x.reshape(TOKENS, D_MODEL)
    for l in range(DEPTH):
        mod = mods[l].reshape(BATCH, 1, N_MOD * D_MODEL)
        g = norm_g[l]
        w_x, w_gate, w_uq_pad, w_ukv_perm = _layer_weights(l, w_in, w_uq, w_ukv)

        act, w_d = _ffn_up(xt, g[0:1], mod, 0, w_gu, w_ffn1_d, l)
        xt, h, w_gu = _ffn_down(act, w_d, xt, mod, 2, g[1:2], 3, "mixer_norm", w_ffn2_gu, l)

        qa, ka, va, q_pad_t, k_pad, v_t = _in_proj(h, tables, w_x, g_cq[l][None, :], g_ckv[l][None, :],
                                                   w_uq_pad, w_ukv_perm)
        o_a, o_b = _attn(qa, ka, va, sink[l], q_pad_t, k_pad, v_t)
        xt = _mix_out(xt, h, mod, 5, o_a, o_b, w_oa[l].astype(BF16), w_ob[l].astype(BF16), w_gate,
                      b_gate[l][None, :], w_out[l].astype(BF16))

        act, w_d = _ffn_up(xt, g[2:3], mod, 6, w_gu, w_ffn2_d, l)
        if l + 1 < DEPTH:
            xt, w_gu = _ffn_down(act, w_d, xt, mod, 8, g_final[None, :], 0, "plain", w_ffn1_gu, l + 1)
        else:
            xt, = _ffn_down(act, w_d, xt, mod, 8, g_final[None, :], 0, "final_norm")
    return xt.reshape(BATCH, SEQ, D_MODEL)
```
